```python
import math
import jax, jax.numpy as jnp
from jax import lax
import numpy as np

D_MODEL = 2048
BATCH = 4
SEQ = 2048
DEPTH = 4

D_MIX = D_MODEL
GROUP_W = D_MIX // 4

SSD_HEAD_DIM = 64
SSD_HEADS = GROUP_W // SSD_HEAD_DIM
SSD_NGROUPS = 2
SSD_STATE = 64
SSD_CONV = 4
SSD_CHUNK = 128
SSD_XBC = GROUP_W + 2 * SSD_NGROUPS * SSD_STATE
DT_MIN = 0.001
DT_MAX = 0.1

S5_CH = 16
S5_GROUPS = GROUP_W // S5_CH
S5_STATE = 64

GLA_HEADS = 4
GLA_DK = GROUP_W // 2 // GLA_HEADS
GLA_DV = GROUP_W // GLA_HEADS
GLA_GATE_RANK = 16
GLA_GATE_NORM = 16.0
GLA_CHUNK = 16

MLA_HEADS = 4
MLA_NOPE = 128
MLA_ROPE = 64
MLA_V = GROUP_W // MLA_HEADS
MLA_Q_RANK = 384
MLA_KV_RANK = 128
MLA_QBLOCK = 128
ROPE_THETA = 10000.0

NORM_EPS = 1e-6

IN_WIDTHS = (
    GROUP_W,
    SSD_XBC,
    SSD_HEADS,
    GROUP_W,
    GROUP_W,
    GLA_HEADS * GLA_DK,
    GLA_HEADS * GLA_DK,
    GLA_HEADS * GLA_DV,
    GROUP_W,
    GLA_GATE_RANK,
    MLA_Q_RANK,
    MLA_KV_RANK,
    MLA_ROPE,
    GROUP_W,
)
D_IN = sum(IN_WIDTHS)

kernel_name = "hybrid_ssd_s5_gla_mla_parallel_heads"


def rmsnorm(x, w):
    xf = x.astype(jnp.float32)
    xf = xf * lax.rsqrt(jnp.mean(xf * xf, axis=-1, keepdims=True) + NORM_EPS)
    return (xf * w.astype(jnp.float32)).astype(x.dtype)


def split_columns(p):
    idx = np.cumsum(np.array(IN_WIDTHS))[:-1].tolist()
    return jnp.split(p, idx, axis=-1)


def causal_depthwise_conv(x, w, b):
    k, c = w.shape
    y = lax.conv_general_dilated(
        x, w[:, None, :].astype(x.dtype), window_strides=(1,), padding=((k - 1, 0),),
        dimension_numbers=('NWC', 'WIO', 'NWC'), feature_group_count=c)
    return y + b


def segsum(a):
    t = a.shape[-1]
    cs = jnp.cumsum(a, axis=-1)
    diff = cs[..., :, None] - cs[..., None, :]
    mask = jnp.tril(jnp.ones((t, t), dtype=bool))
    return jnp.where(mask, diff, -jnp.inf)


def ssd_chunked(x, a, bm, cm):
    bsz, s, h, p = x.shape
    n = bm.shape[-1]
    nc = s // SSD_CHUNK
    x = x.reshape(bsz, nc, SSD_CHUNK, h, p)
    bm = bm.reshape(bsz, nc, SSD_CHUNK, h, n)
    cm = cm.reshape(bsz, nc, SSD_CHUNK, h, n)
    a = a.astype(jnp.float32).reshape(bsz, nc, SSD_CHUNK, h).transpose(0, 3, 1, 2)
    a_cs = jnp.cumsum(a, axis=-1)
    lmat = jnp.exp(segsum(a))
    scores = jnp.einsum('bclhn,bcshn->bhcls', cm, bm) * lmat
    y_diag = jnp.einsum('bhcls,bcshp->bclhp', scores, x)
    decay_states = jnp.exp(a_cs[..., -1:] - a_cs)
    states = jnp.einsum('bclhn,bhcl,bclhp->bchpn', bm, decay_states, x)
    states = jnp.concatenate([jnp.zeros_like(states[:, :1]), states], axis=1)
    decay_chunk = jnp.exp(segsum(jnp.pad(a_cs[..., -1], ((0, 0), (0, 0), (1, 0)))))
    states = jnp.einsum('bhzc,bchpn->bzhpn', decay_chunk, states)[:, :-1]
    y_off = jnp.einsum('bclhn,bchpn,bhcl->bclhp', cm, states, jnp.exp(a_cs))
    return (y_diag + y_off).reshape(bsz, s, h, p)


def ssd_branch(z, xbc, dt_raw, conv_w, conv_b, dt_bias, a_log, d_skip, norm_w):
    bsz, s, _ = xbc.shape
    xbc = jax.nn.silu(causal_depthwise_conv(xbc, conv_w, conv_b))
    xs, bm, cm = jnp.split(xbc, [GROUP_W, GROUP_W + SSD_NGROUPS * SSD_STATE], axis=-1)
    xs = xs.reshape(bsz, s, SSD_HEADS, SSD_HEAD_DIM)
    rep = SSD_HEADS // SSD_NGROUPS
    bm = jnp.repeat(bm.reshape(bsz, s, SSD_NGROUPS, SSD_STATE), rep, axis=2)
    cm = jnp.repeat(cm.reshape(bsz, s, SSD_NGROUPS, SSD_STATE), rep, axis=2)
    dt = jax.nn.softplus(dt_raw.astype(jnp.float32) + dt_bias.astype(jnp.float32))
    a = -jnp.exp(a_log.astype(jnp.float32))
    y = ssd_chunked(xs * dt[..., None], dt * a, bm, cm)
    y = y + d_skip[:, None] * xs
    y = y.reshape(bsz, s, GROUP_W)
    return rmsnorm(y * jax.nn.silu(z), norm_w)


def complex_affine_combine(e1, e2):
    a1r, a1i, b1r, b1i = e1
    a2r, a2i, b2r, b2i = e2
    return (a2r * a1r - a2i * a1i,
            a2r * a1i + a2i * a1r,
            a2r * b1r - a2i * b1i + b2r,
            a2r * b1i + a2i * b1r + b2i)


def s5_branch(u, gate, a_re, a_im, log_dt, b_re, b_im, c_re, c_im, d_skip, glu_w, glu_b):
    bsz, s, _ = u.shape
    f32 = jnp.float32
    uf = u.astype(f32).reshape(bsz, s, S5_GROUPS, S5_CH)
    a_re = a_re.astype(f32)
    a_im = a_im.astype(f32)
    delta = jnp.exp(log_dt.astype(f32))[:, None]
    mag = jnp.exp(a_re * delta)
    ab_re = mag * jnp.cos(a_im * delta)
    ab_im = mag * jnp.sin(a_im * delta)
    den = a_re * a_re + a_im * a_im
    coef_re = ((ab_re - 1.0) * a_re + ab_im * a_im) / den
    coef_im = (ab_im * a_re - (ab_re - 1.0) * a_im) / den
    b_re = b_re.astype(f32)
    b_im = b_im.astype(f32)
    bb_re = coef_re[..., None] * b_re - coef_im[..., None] * b_im
    bb_im = coef_re[..., None] * b_im + coef_im[..., None] * b_re
    bu_re = jnp.einsum('bsgi,gni->bsgn', uf, bb_re)
    bu_im = jnp.einsum('bsgi,gni->bsgn', uf, bb_im)
    shape = bu_re.shape
    elems = (jnp.broadcast_to(ab_re, shape), jnp.broadcast_to(ab_im, shape), bu_re, bu_im)
    _, _, h_re, h_im = lax.associative_scan(complex_affine_combine, elems, axis=1)
    y = (jnp.einsum('bsgn,gin->bsgi', h_re, c_re.astype(f32))
         - jnp.einsum('bsgn,gin->bsgi', h_im, c_im.astype(f32)))
    y = y.reshape(bsz, s, GROUP_W) + d_skip * uf.reshape(bsz, s, GROUP_W)
    y = jax.nn.gelu(y)
    y = y * jax.nn.sigmoid(y @ glu_w.astype(f32) + glu_b)
    return y * jax.nn.silu(gate)


def gla_branch(q, k, v, gate, glr, w2, gb, norm_w):
    bsz, s, _ = q.shape
    nc = s // GLA_CHUNK
    f32 = jnp.float32

    def heads(t, d):
        return t.reshape(bsz, nc, GLA_CHUNK, GLA_HEADS, d).transpose(0, 3, 1, 2, 4)

    q = heads(q, GLA_DK).astype(f32) * (GLA_DK ** -0.5)
    k = heads(k, GLA_DK).astype(f32)
    v = heads(v, GLA_DV).astype(f32)
    g = jax.nn.log_sigmoid((glr @ w2 + gb).astype(f32)) / GLA_GATE_NORM
    bcs = jnp.cumsum(heads(g, GLA_DK), axis=3)
    causal = jnp.tril(jnp.ones((GLA_CHUNK, GLA_CHUNK), dtype=bool))
    diff = bcs[..., :, None, :] - bcs[..., None, :, :]
    decay = jnp.exp(jnp.where(causal[..., None], diff, -jnp.inf))
    attn = jnp.einsum('bhcid,bhcjd,bhcijd->bhcij', q, k, decay)
    o_intra = jnp.einsum('bhcij,bhcjd->bhcid', attn, v)
    b_last = bcs[..., -1, :]
    q_dec = q * jnp.exp(bcs)
    k_dec = k * jnp.exp(b_last[..., None, :] - bcs)

    def step(state, inp):
        qd, kd, vc, bl = inp
        o = jnp.einsum('bhld,bhde->bhle', qd, state)
        state = jnp.exp(bl)[..., None] * state + jnp.einsum('bhld,bhle->bhde', kd, vc)
        return state, o

    xs = (jnp.moveaxis(q_dec, 2, 0), jnp.moveaxis(k_dec, 2, 0), jnp.moveaxis(v, 2, 0), jnp.moveaxis(b_last, 2, 0))
    init = jnp.zeros((bsz, GLA_HEADS, GLA_DK, GLA_DV), f32)
    _, o_inter = lax.scan(step, init, xs)
    o = o_intra + jnp.moveaxis(o_inter, 0, 2)
    o = rmsnorm(o, norm_w)
    o = o.transpose(0, 2, 3, 1, 4).reshape(bsz, s, GLA_HEADS * GLA_DV)
    return o * jax.nn.silu(gate)


def rope_tables(positions):
    inv_freq = ROPE_THETA ** (-jnp.arange(0, MLA_ROPE, 2, dtype=jnp.float32) / MLA_ROPE)
    ang = positions.astype(jnp.float32)[..., None] * inv_freq
    return jnp.cos(ang)[:, :, None, :], jnp.sin(ang)[:, :, None, :]


def apply_rope_tail(t, cos, sin):
    t_nope, t_rope = jnp.split(t, [t.shape[-1] - MLA_ROPE], axis=-1)
    x1, x2 = jnp.split(t_rope.astype(jnp.float32), 2, axis=-1)
    rot = jnp.concatenate([x1 * cos - x2 * sin, x2 * cos + x1 * sin], axis=-1).astype(t.dtype)
    return jnp.concatenate([t_nope, rot], axis=-1)


def causal_block_attention(q, k, v):
    bsz, h, s, dq = q.shape
    nb = s // MLA_QBLOCK
    scale = dq ** -0.5
    qb = q.reshape(bsz, h, nb, MLA_QBLOCK, dq).transpose(2, 0, 1, 3, 4)
    kpos = jnp.arange(s)

    def one(args):
        qblk, i = args
        sc = jnp.einsum('bhqd,bhkd->bhqk', qblk, k).astype(jnp.float32) * scale
        qpos = i * MLA_QBLOCK + jnp.arange(MLA_QBLOCK)
        sc = jnp.where(kpos[None, :] <= qpos[:, None], sc, -jnp.inf)
        p = jax.nn.softmax(sc, axis=-1)
        return jnp.einsum('bhqk,bhkd->bhqd', p.astype(v.dtype), v)

    out = lax.map(one, (qb, jnp.arange(nb)))
    return out.transpose(1, 2, 0, 3, 4).reshape(bsz, h, s, v.shape[-1])


def mla_branch(c_q, c_kv, k_pe, gate, cos, sin, q_norm_w, w_uq, kv_norm_w, w_ukv, qh_w, kh_w):
    bsz, s, _ = c_q.shape
    q = (rmsnorm(c_q, q_norm_w) @ w_uq).reshape(bsz, s, MLA_HEADS, MLA_NOPE + MLA_ROPE)
    kv = (rmsnorm(c_kv, kv_norm_w) @ w_ukv).reshape(bsz, s, MLA_HEADS, MLA_NOPE + MLA_V)
    k_nope, v = jnp.split(kv, [MLA_NOPE], axis=-1)
    k = jnp.concatenate([k_nope, jnp.broadcast_to(k_pe[:, :, None, :], (bsz, s, MLA_HEADS, MLA_ROPE))], axis=-1)
    q = apply_rope_tail(rmsnorm(q, qh_w), cos, sin)
    k = apply_rope_tail(rmsnorm(k, kh_w), cos, sin)
    out = causal_block_attention(q.transpose(0, 2, 1, 3), k.transpose(0, 2, 1, 3), v.transpose(0, 2, 1, 3))
    out = out.transpose(0, 2, 1, 3).reshape(bsz, s, MLA_HEADS * MLA_V)
    return out * jax.nn.silu(gate)


def setup_inputs(seed: int = 0) -> dict:
    key = jax.random.key(seed)
    ks = iter(jax.random.split(key, 40))
    f32 = jnp.float32

    def nrm(shape, scale):
        return scale * jax.random.normal(next(ks), shape, f32)

    def gain(shape):
        return 1.0 + nrm(shape, 0.02)

    L = DEPTH
    x = jax.random.normal(next(ks), (BATCH, SEQ, D_MODEL), f32)
    positions = (jnp.arange(SEQ, dtype=jnp.int32)[None, :]
                 + jax.random.randint(next(ks), (BATCH, 1), 0, 1024, dtype=jnp.int32))
    u_dt = jax.random.uniform(next(ks), (L, SSD_HEADS), f32)
    dt0 = jnp.exp(u_dt * (math.log(DT_MAX) - math.log(DT_MIN)) + math.log(DT_MIN))
    ssd_dt_bias = dt0 + jnp.log(-jnp.expm1(-dt0))
    ssd_a_log = jnp.log(jax.random.uniform(next(ks), (L, SSD_HEADS), f32, 1.0, 16.0))
    s5_a_re = -0.5 * jnp.exp(nrm((L, S5_GROUPS, S5_STATE), 0.02))
    s5_a_im = math.pi * jnp.arange(S5_STATE, dtype=f32) + nrm((L, S5_GROUPS, S5_STATE), 0.02)
    s5_log_dt = (jax.random.uniform(next(ks), (L, S5_GROUPS), f32)
                 * (math.log(DT_MAX) - math.log(DT_MIN)) + math.log(DT_MIN))
    return {
        'x': x,
        'positions': positions,
        'norm_w': gain((L, D_MODEL)),
        'w_in': nrm((L, D_MODEL, D_IN), D_MODEL ** -0.5),
        'w_out': nrm((L, D_MIX, D_MODEL), D_MIX ** -0.5),
        'ssd_conv_w': nrm((L, SSD_CONV, SSD_XBC), SSD_CONV ** -0.5),
        'ssd_conv_b': nrm((L, SSD_XBC), 0.02),
        'ssd_dt_bias': ssd_dt_bias,
        'ssd_a_log': ssd_a_log,
        'ssd_d': gain((L, SSD_HEADS)),
        'ssd_norm_w': gain((L, GROUP_W)),
        's5_a_re': s5_a_re,
        's5_a_im': s5_a_im,
        's5_log_dt': s5_log_dt,
        's5_b_re': nrm((L, S5_GROUPS, S5_STATE, S5_CH), (2 * S5_CH) ** -0.5),
        's5_b_im': nrm((L, S5_GROUPS, S5_STATE, S5_CH), (2 * S5_CH) ** -0.5),
        's5_c_re': nrm((L, S5_GROUPS, S5_CH, S5_STATE), S5_STATE ** -0.5),
        's5_c_im': nrm((L, S5_GROUPS, S5_CH, S5_STATE), S5_STATE ** -0.5),
        's5_d': nrm((L, GROUP_W), 1.0),
        's5_glu_w': nrm((L, GROUP_W, GROUP_W), GROUP_W ** -0.5),
        's5_glu_b': nrm((L, GROUP_W), 0.02),
        'gla_gate_w2': nrm((L, GLA_GATE_RANK, GLA_HEADS * GLA_DK), GLA_GATE_RANK ** -0.5),
        'gla_gate_b': nrm((L, GLA_HEADS * GLA_DK), 0.02),
        'gla_norm_w': gain((L, GLA_DV)),
        'mla_q_norm_w': gain((L, MLA_Q_RANK)),
        'mla_w_uq': nrm((L, MLA_Q_RANK, MLA_HEADS * (MLA_NOPE + MLA_ROPE)), MLA_Q_RANK ** -0.5),
        'mla_kv_norm_w': gain((L, MLA_KV_RANK)),
        'mla_w_ukv': nrm((L, MLA_KV_RANK, MLA_HEADS * (MLA_NOPE + MLA_V)), MLA_KV_RANK ** -0.5),
        'mla_q_head_norm_w': gain((L, MLA_NOPE + MLA_ROPE)),
        'mla_k_head_norm_w': gain((L, MLA_NOPE + MLA_ROPE)),
    }


def reference(x, positions, norm_w, w_in, w_out, ssd_conv_w, ssd_conv_b, ssd_dt_bias, ssd_a_log,
              ssd_d, ssd_norm_w, s5_a_re, s5_a_im, s5_log_dt, s5_b_re, s5_b_im, s5_c_re, s5_c_im,
              s5_d, s5_glu_w, s5_glu_b, gla_gate_w2, gla_gate_b, gla_norm_w, mla_q_norm_w, mla_w_uq,
              mla_kv_norm_w, mla_w_ukv, mla_q_head_norm_w, mla_k_head_norm_w):
    cos, sin = rope_tables(positions)
    h = x
    for l in range(DEPTH):
        u = rmsnorm(h, norm_w[l])
        (z, xbc, dt_raw, s5_u, s5_g, gq, gk, gv, gg, glr,
         cq, ckv, kpe, mg) = split_columns(u @ w_in[l])
        y_a = ssd_branch(z, xbc, dt_raw, ssd_conv_w[l], ssd_conv_b[l], ssd_dt_bias[l], ssd_a_log[l],
                         ssd_d[l], ssd_norm_w[l])
        y_b = s5_branch(s5_u, s5_g, s5_a_re[l], s5_a_im[l], s5_log_dt[l], s5_b_re[l], s5_b_im[l],
                        s5_c_re[l], s5_c_im[l], s5_d[l], s5_glu_w[l], s5_glu_b[l])
        y_c = gla_branch(gq, gk, gv, gg, glr, gla_gate_w2[l], gla_gate_b[l], gla_norm_w[l])
        y_d = mla_branch(cq, ckv, kpe, mg, cos, sin, mla_q_norm_w[l], mla_w_uq[l], mla_kv_norm_w[l],
                         mla_w_ukv[l], mla_q_head_norm_w[l], mla_k_head_norm_w[l])
        mix = jnp.concatenate([y_a.astype(h.dtype), y_b.astype(h.dtype), y_c.astype(h.dtype), y_d.astype(h.dtype)], axis=-1)
        h = h + mix @ w_out[l]
    return h
```

```python
import functools
import math

import jax
import jax.numpy as jnp
from jax import lax
from jax.experimental import pallas as pl
from jax.experimental.pallas import tpu as pltpu

F32 = jnp.float32
BF16 = jnp.bfloat16
NORM_EPS = 1e-6
HI = lax.Precision.HIGHEST

D_MODEL = 2048
GROUP_W = 512
SSD_HEADS = 8
SSD_HEAD_DIM = 64
SSD_STATE = 64
SSD_CHUNK = 128
SSD_XBC = 768
SSD_CONV = 4
S5_GROUPS = 32
S5_CH = 16
S5_STATE = 64
S5_NS = S5_GROUPS * S5_STATE
S5_ROWS = 256
GLA_HEADS = 4
GLA_DK = 64
GLA_DV = 128
GLA_CHUNK = 16
GLA_TILE = 128
GLA_GATE_NORM = 16.0
MLA_HEADS = 4
MLA_NOPE = 128
MLA_ROPE = 64
MLA_V = 128
MLA_Q_RANK = 384
MLA_KV_RANK = 128
MLA_HEAD_PAD = 256
ROPE_THETA = 10000.0
FLASH_TQ = 256

W_SSD = 512 + 768 + 512
W_S5 = 1024
W_GLA = 256 + 256 + 512 + 512 + 128
W_MLA = 384 + 128 + 128 + 512

VMEM_LIMIT_BYTES = 56 * 1024 * 1024


def _params(*sem):
    return pltpu.CompilerParams(dimension_semantics=sem, vmem_limit_bytes=VMEM_LIMIT_BYTES)


def _silu(x):
    return x / (1.0 + jnp.exp(-x))


def _softplus(x):
    return jnp.maximum(x, 0.0) + jnp.log(1.0 + jnp.exp(-jnp.abs(x)))


def _dot(a, b):
    return jnp.dot(a, b, preferred_element_type=F32)


def _dot_nt(a, b):
    return lax.dot_general(a, b, (((1,), (1,)), ((), ())), preferred_element_type=F32)


def _dot_tn(a, b):
    return lax.dot_general(a, b, (((0,), (0,)), ((), ())), preferred_element_type=F32)


def _rms_kernel(x_ref, w_ref, o_ref):
    x = x_ref[...]
    ms = jnp.mean(x * x, axis=-1, keepdims=True)
    o_ref[...] = (x * lax.rsqrt(ms + NORM_EPS) * w_ref[...]).astype(BF16)


def _rmsnorm_bf16(x, w, tm=512):
    t, d = x.shape
    return pl.pallas_call(
        _rms_kernel,
        grid=(t // tm,),
        in_specs=[pl.BlockSpec((tm, d), lambda i: (i, 0)), pl.BlockSpec((1, d), lambda i: (0, 0))],
        out_specs=pl.BlockSpec((tm, d), lambda i: (i, 0)),
        out_shape=jax.ShapeDtypeStruct((t, d), BF16),
        compiler_params=_params("arbitrary"),
        name="rmsnorm_in",
    )(x, w)


def _mm_kernel(x_ref, w_ref, o_ref):
    o_ref[...] = _dot(x_ref[...], w_ref[...])


def _in_proj(u, w, name, tm=512):
    t, k = u.shape
    n = w.shape[1]
    return pl.pallas_call(
        _mm_kernel,
        grid=(t // tm,),
        in_specs=[pl.BlockSpec((tm, k), lambda i: (i, 0)), pl.BlockSpec((k, n), lambda i: (0, 0))],
        out_specs=pl.BlockSpec((tm, n), lambda i: (i, 0)),
        out_shape=jax.ShapeDtypeStruct((t, n), F32),
        compiler_params=_params("arbitrary"),
        name=name,
    )(u, w)


def _out_proj_kernel(ya_ref, yb_ref, yc_ref, yd_ref, w_ref, h_ref, nw_ref, ho_ref, uo_ref):
    acc = h_ref[...]
    for i, y_ref in enumerate((ya_ref, yb_ref, yc_ref, yd_ref)):
        acc = acc + _dot(y_ref[...], w_ref[GROUP_W * i:GROUP_W * (i + 1), :])
    ho_ref[...] = acc
    ms = jnp.mean(acc * acc, axis=-1, keepdims=True)
    uo_ref[...] = (acc * lax.rsqrt(ms + NORM_EPS) * nw_ref[...]).astype(BF16)


def _out_proj(ys, w, h, nw_next, tm=256):
    t, d = h.shape
    yspec = pl.BlockSpec((tm, GROUP_W), lambda i: (i, 0))
    return pl.pallas_call(
        _out_proj_kernel,
        grid=(t // tm,),
        in_specs=[yspec, yspec, yspec, yspec,
                  pl.BlockSpec((d, d), lambda i: (0, 0)),
                  pl.BlockSpec((tm, d), lambda i: (i, 0)),
                  pl.BlockSpec((1, d), lambda i: (0, 0))],
        out_specs=[pl.BlockSpec((tm, d), lambda i: (i, 0)), pl.BlockSpec((tm, d), lambda i: (i, 0))],
        out_shape=[jax.ShapeDtypeStruct((t, d), F32), jax.ShapeDtypeStruct((t, d), BF16)],
        compiler_params=_params("arbitrary"),
        name="out_proj",
    )(*ys, w, h, nw_next)


def _ssd_kernel(p_ref, cw_ref, cb_ref, dtb_ref, alog_ref, d_ref, nw_ref, o_ref, cbuf, st_ref):
    L = SSD_CHUNK

    @pl.when(pl.program_id(1) == 0)
    def _():
        cbuf[0:8, :] = jnp.zeros((8, SSD_XBC), F32)
        st_ref[...] = jnp.zeros_like(st_ref)

    z = p_ref[:, 0:512]
    cbuf[8:8 + L, :] = p_ref[:, 512:512 + SSD_XBC]
    acc = cb_ref[...] + cbuf[pl.ds(8 - (SSD_CONV - 1), L), :] * cw_ref[0:1, :]
    for k in range(1, SSD_CONV):
        acc = acc + cbuf[pl.ds(8 - (SSD_CONV - 1) + k, L), :] * cw_ref[k:k + 1, :]
    cbuf[0:8, :] = cbuf[L:L + 8, :]
    xbc = _silu(acc)
    xs = xbc[:, 0:512]
    bm = xbc[:, 512:640]
    cm = xbc[:, 640:768]

    dt = _softplus(p_ref[:, 1280:1792] + dtb_ref[...])
    a = -jnp.exp(alog_ref[...])
    adt = dt * a
    row = lax.broadcasted_iota(jnp.int32, (L, L), 0)
    col = lax.broadcasted_iota(jnp.int32, (L, L), 1)
    causal = col <= row
    tri = causal.astype(F32)
    cs = jnp.dot(tri, adt, preferred_element_type=F32, precision=HI)
    cs_t = jnp.transpose(cs)
    xdt = (xs * dt).astype(BF16)

    ys = []
    for g in range(2):
        bg = bm[:, 64 * g:64 * g + 64]
        cg = cm[:, 64 * g:64 * g + 64]
        gmat = _dot_nt(cg.astype(BF16), bg.astype(BF16))
        for hh in range(4):
            h = 4 * g + hh
            lo = 64 * h
            cs_col = cs[:, lo:lo + 1]
            cs_row = cs_t[lo:lo + 1, :]
            lmat = jnp.exp(jnp.where(causal, cs_col - cs_row, -jnp.inf))
            xh = xdt[:, lo:lo + 64]
            y = _dot((gmat * lmat).astype(BF16), xh)
            st = st_ref[h]
            y = y + _dot((cg * jnp.exp(cs_col)).astype(BF16), st.astype(BF16))
            cs_last = cs[L - 1:L, lo:lo + 1]
            bdec = bg * jnp.exp(cs_last - cs_col)
            st_ref[h] = jnp.exp(cs_last) * st + _dot_tn(bdec.astype(BF16), xh)
            ys.append(y)
    y = jnp.concatenate(ys, axis=1) + d_ref[...] * xs
    y = y * _silu(z)
    ms = jnp.mean(y * y, axis=-1, keepdims=True)
    o_ref[...] = (y * lax.rsqrt(ms + NORM_EPS) * nw_ref[...]).astype(BF16)


def _ssd_branch(p, cw, cb, dtb, alog, dsk, nw, bsz, s):
    nc = s // SSD_CHUNK
    vec = lambda n: pl.BlockSpec((1, n), lambda b, c: (0, 0))
    return pl.pallas_call(
        _ssd_kernel,
        grid=(bsz, nc),
        in_specs=[pl.BlockSpec((SSD_CHUNK, W_SSD), lambda b, c: (b * nc + c, 0)),
                  pl.BlockSpec((SSD_CONV, SSD_XBC), lambda b, c: (0, 0)),
                  vec(SSD_XBC), vec(512), vec(512), vec(512), vec(512)],
        out_specs=pl.BlockSpec((SSD_CHUNK, GROUP_W), lambda b, c: (b * nc + c, 0)),
        out_shape=jax.ShapeDtypeStruct((bsz * s, GROUP_W), BF16),
        scratch_shapes=[pltpu.VMEM((SSD_CHUNK + 8, SSD_XBC), F32),
                        pltpu.VMEM((SSD_HEADS, SSD_STATE, SSD_HEAD_DIM), F32)],
        compiler_params=_params("arbitrary", "arbitrary"),
        name="ssd_branch",
    )(p, cw, cb, dtb, alog, dsk, nw)


def _cmul(ar, ai, br, bi):
    return ar * br - ai * bi, ar * bi + ai * br


def _s5_prep_kernel(are_ref, aim_ref, ldt_ref, bre_ref, bim_ref, bb_ref, tab_ref):
    are = are_ref[...]
    aim = aim_ref[...]
    delta = jnp.exp(ldt_ref[...])
    mag = jnp.exp(are * delta)
    ar = mag * jnp.cos(aim * delta)
    ai = mag * jnp.sin(aim * delta)
    den = are * are + aim * aim
    coef_re = ((ar - 1.0) * are + ai * aim) / den
    coef_im = (ai * are - (ar - 1.0) * aim) / den
    bre = bre_ref[...]
    bim = bim_ref[...]
    bb_ref[:, 0:S5_NS] = (coef_re * bre - coef_im * bim).astype(BF16)
    bb_ref[:, S5_NS:2 * S5_NS] = (coef_re * bim + coef_im * bre).astype(BF16)

    pw = [(ar, ai)]
    for k in range(2, 9):
        pw.append(_cmul(*pw[k // 2 - 1], *pw[k - k // 2 - 1]))
    row = lax.broadcasted_iota(jnp.int32, (8, S5_NS), 0)
    for i, s in enumerate((1, 2, 4)):
        tab_ref[2 * i] = jnp.where(row >= s, pw[s - 1][0], 0.0)
        tab_ref[2 * i + 1] = jnp.where(row >= s, pw[s - 1][1], 0.0)
    cr = jnp.zeros((8, S5_NS), F32)
    ci = jnp.zeros((8, S5_NS), F32)
    for r in range(8):
        cr = jnp.where(row == r, pw[r][0], cr)
        ci = jnp.where(row == r, pw[r][1], ci)
    tab_ref[6] = cr
    tab_ref[7] = ci


def _s5_prep(are, aim, ldt, bre, bim):
    return pl.pallas_call(
        _s5_prep_kernel,
        out_shape=[jax.ShapeDtypeStruct((GROUP_W, 2 * S5_NS), BF16),
                   jax.ShapeDtypeStruct((8, 8, S5_NS), F32)],
        compiler_params=pltpu.CompilerParams(vmem_limit_bytes=VMEM_LIMIT_BYTES),
        name="s5_prep",
    )(are, aim, ldt, bre, bim)


def _s5_kernel(p_ref, bb_ref, tab_ref, cre_ref, cim_ref, d_ref, gw_ref, gb_ref, o_ref,
               hbuf, hbf, carry):
    R = S5_ROWS

    @pl.when(pl.program_id(1) == 0)
    def _():
        carry[...] = jnp.zeros_like(carry)

    u = p_ref[:, 0:512]
    hbuf[...] = _dot(u.astype(BF16), bb_ref[...])

    for lc in range(S5_NS // 128):
        re = slice(128 * lc, 128 * lc + 128)
        im = slice(S5_NS + 128 * lc, S5_NS + 128 * lc + 128)
        steps = [(s, tab_ref[2 * i, :, re], tab_ref[2 * i + 1, :, re]) for i, s in enumerate((1, 2, 4))]
        pr = tab_ref[6, :, re]
        pi = tab_ref[7, :, re]

        def body(r, c, re=re, im=im, steps=steps, pr=pr, pi=pi):
            cr, ci = c
            base = pl.multiple_of(r * 16, 16)
            out_r, out_i = [], []
            for sub in range(2):
                rows = pl.ds(base + 8 * sub, 8)
                hr = hbuf[rows, re]
                hi = hbuf[rows, im]
                for s, tr, ti in steps:
                    rr = pltpu.roll(hr, s, 0)
                    ri = pltpu.roll(hi, s, 0)
                    hr, hi = hr + (tr * rr - ti * ri), hi + (tr * ri + ti * rr)
                hr = hr + (pr * cr - pi * ci)
                hi = hi + (pr * ci + pi * cr)
                out_r.append(hr)
                out_i.append(hi)
                cr = jnp.broadcast_to(hr[7:8, :], (8, 128))
                ci = jnp.broadcast_to(hi[7:8, :], (8, 128))
            hbf[pl.ds(base, 16), re] = jnp.concatenate(out_r, axis=0).astype(BF16)
            hbf[pl.ds(base, 16), im] = jnp.concatenate(out_i, axis=0).astype(BF16)
            return cr, ci

        cr, ci = lax.fori_loop(0, R // 16, body, (carry[:, re], carry[:, im]))
        carry[:, re] = cr
        carry[:, im] = ci

    y = _dot(hbf[:, 0:S5_NS], cre_ref[...]) - _dot(hbf[:, S5_NS:2 * S5_NS], cim_ref[...])
    y = y + d_ref[...] * u
    y = 0.5 * y * (1.0 + jnp.tanh(math.sqrt(2.0 / math.pi) * (y + 0.044715 * (y * y * y))))
    glu = _dot(y.astype(BF16), gw_ref[...]) + gb_ref[...]
    y = y / (1.0 + jnp.exp(-glu))
    o_ref[...] = (y * _silu(p_ref[:, 512:1024])).astype(BF16)


def _s5_branch(p, bb, tab, cre, cim, dsk, gw, gb, bsz, s):
    nt = s // S5_ROWS
    const = lambda shape: pl.BlockSpec(shape, lambda b, t: tuple(0 for _ in shape))
    return pl.pallas_call(
        _s5_kernel,
        grid=(bsz, nt),
        in_specs=[pl.BlockSpec((S5_ROWS, W_S5), lambda b, t: (b * nt + t, 0)),
                  const((GROUP_W, 2 * S5_NS)), const((8, 8, S5_NS)),
                  const((S5_NS, GROUP_W)), const((S5_NS, GROUP_W)),
                  const((1, GROUP_W)), const((GROUP_W, GROUP_W)), const((1, GROUP_W))],
        out_specs=pl.BlockSpec((S5_ROWS, GROUP_W), lambda b, t: (b * nt + t, 0)),
        out_shape=jax.ShapeDtypeStruct((bsz * s, GROUP_W), BF16),
        scratch_shapes=[pltpu.VMEM((S5_ROWS, 2 * S5_NS), F32),
                        pltpu.VMEM((S5_ROWS, 2 * S5_NS), BF16),
                        pltpu.VMEM((8, 2 * S5_NS), F32)],
        compiler_params=_params("arbitrary", "arbitrary"),
        name="s5_branch",
    )(p, bb, tab, cre, cim, dsk, gw, gb)


def _gla_kernel(p_ref, w2_ref, gb_ref, nw_ref, o_ref, st_ref):
    T = GLA_TILE
    NCH = T // GLA_CHUNK

    @pl.when(pl.program_id(1) == 0)
    def _():
        st_ref[...] = jnp.zeros_like(st_ref)

    x = _dot(p_ref[:, 1536:1664].astype(BF16), w2_ref[...]) + gb_ref[...]
    g = -_softplus(-x) * (1.0 / GLA_GATE_NORM)
    row = lax.broadcasted_iota(jnp.int32, (T, T), 0)
    col = lax.broadcasted_iota(jnp.int32, (T, T), 1)
    same = (row // GLA_CHUNK) == (col // GLA_CHUNK)
    intra = jnp.logical_and(same, col <= row)
    b = jnp.dot(intra.astype(F32), g, preferred_element_type=F32, precision=HI)
    ones_blk = same.astype(F32)
    blast = jnp.dot(ones_blk, g, preferred_element_type=F32, precision=HI)
    blast_t = lax.dot_general(g, ones_blk, (((0,), (0,)), ((), ())),
                              preferred_element_type=F32, precision=HI)
    scale = GLA_DK ** -0.5
    q = p_ref[:, 0:256] * scale
    k = p_ref[:, 256:512]
    half = 0.5 * blast
    q_mid = (q * jnp.exp(b - half)).astype(BF16)
    k_mid = (k * jnp.exp(half - b)).astype(BF16)
    q_dec = (q * jnp.exp(b)).astype(BF16)
    k_dec = (k * jnp.exp(blast - b)).astype(BF16)
    rsel = lax.broadcasted_iota(jnp.int32, (T, NCH * GLA_DK), 0) // GLA_CHUNK
    csel = lax.broadcasted_iota(jnp.int32, (T, NCH * GLA_DK), 1) // GLA_DK
    blockmask = rsel == csel

    for h in range(GLA_HEADS):
        lo = GLA_DK * h
        v = p_ref[:, 512 + GLA_DV * h:512 + GLA_DV * (h + 1)].astype(BF16)
        attn = jnp.where(intra, _dot_nt(q_mid[:, lo:lo + GLA_DK], k_mid[:, lo:lo + GLA_DK]), 0.0)
        o = _dot(attn.astype(BF16), v)
        zero = jnp.zeros((), BF16)
        qcat = jnp.where(blockmask, jnp.tile(q_dec[:, lo:lo + GLA_DK], (1, NCH)), zero)
        kcat = jnp.where(blockmask, jnp.tile(k_dec[:, lo:lo + GLA_DK], (1, NCH)), zero)
        ds = _dot_tn(kcat, v)
        st = st_ref[h]
        sts = []
        for c in range(NCH):
            sts.append(st)
            dec = jnp.exp(blast_t[lo:lo + GLA_DK, GLA_CHUNK * c:GLA_CHUNK * c + 1])
            st = dec * st + ds[GLA_DK * c:GLA_DK * (c + 1), :]
        st_ref[h] = st
        o = o + _dot(qcat, jnp.concatenate(sts, axis=0).astype(BF16))
        ms = jnp.mean(o * o, axis=-1, keepdims=True)
        o = o * lax.rsqrt(ms + NORM_EPS) * nw_ref[...]
        gate = p_ref[:, 1024 + GLA_DV * h:1024 + GLA_DV * (h + 1)]
        o_ref[:, GLA_DV * h:GLA_DV * (h + 1)] = (o * _silu(gate)).astype(BF16)


def _gla_branch(p, w2, gb, nw, bsz, s):
    nt = s // GLA_TILE
    return pl.pallas_call(
        _gla_kernel,
        grid=(bsz, nt),
        in_specs=[pl.BlockSpec((GLA_TILE, W_GLA), lambda b, t: (b * nt + t, 0)),
                  pl.BlockSpec((128, 256), lambda b, t: (0, 0)),
                  pl.BlockSpec((1, 256), lambda b, t: (0, 0)),
                  pl.BlockSpec((1, GLA_DV), lambda b, t: (0, 0))],
        out_specs=pl.BlockSpec((GLA_TILE, GROUP_W), lambda b, t: (b * nt + t, 0)),
        out_shape=jax.ShapeDtypeStruct((bsz * s, GROUP_W), BF16),
        scratch_shapes=[pltpu.VMEM((GLA_HEADS, GLA_DK, GLA_DV), F32)],
        compiler_params=_params("arbitrary", "arbitrary"),
        name="gla_branch",
    )(p, w2, gb, nw)


def _rope_swap(t):
    lane = lax.broadcasted_iota(jnp.int32, t.shape, 1)
    return jnp.where(lane < MLA_ROPE // 2, pltpu.roll(t, 128 - MLA_ROPE // 2, 1), pltpu.roll(t, MLA_ROPE // 2, 1))


def _mla_prep_kernel(p_ref, pos_ref, invf_ref, sign_ref, qnw_ref, wuq_ref, kvnw_ref, wukv_ref,
                     qhw_ref, khw_ref, q_ref, k_ref, v_ref):
    cq = p_ref[:, 0:384]
    ckv = p_ref[:, 384:512]
    kpe = p_ref[:, 512:640]
    ms = jnp.mean(cq * cq, axis=-1, keepdims=True)
    qn = (cq * lax.rsqrt(ms + NORM_EPS) * qnw_ref[...]).astype(BF16)
    ms = jnp.mean(ckv * ckv, axis=-1, keepdims=True)
    kvn = (ckv * lax.rsqrt(ms + NORM_EPS) * kvnw_ref[...]).astype(BF16)
    q = _dot(qn, wuq_ref[...])
    kv = _dot(kvn, wukv_ref[...])
    ang = pos_ref[...].astype(F32) * invf_ref[...]
    cos_t = jnp.cos(ang)
    sin_t = jnp.sin(ang) * sign_ref[...]
    inv_d = 1.0 / (MLA_NOPE + MLA_ROPE)
    scale = (MLA_NOPE + MLA_ROPE) ** -0.5
    kpe_ss = jnp.sum(kpe * kpe, axis=-1, keepdims=True)
    qhw = qhw_ref[...]
    khw = khw_ref[...]
    for h in range(MLA_HEADS):
        lo = MLA_HEAD_PAD * h
        qh = q[:, lo:lo + MLA_HEAD_PAD]
        r = lax.rsqrt(jnp.sum(qh * qh, axis=-1, keepdims=True) * inv_d + NORM_EPS) * scale
        qh = qh * r * qhw
        t = qh[:, 128:256]
        q_ref[:, lo:lo + 128] = qh[:, 0:128].astype(BF16)
        q_ref[:, lo + 128:lo + 256] = (t * cos_t + _rope_swap(t) * sin_t).astype(BF16)
        kn = kv[:, lo:lo + 128]
        r = lax.rsqrt((jnp.sum(kn * kn, axis=-1, keepdims=True) + kpe_ss) * inv_d + NORM_EPS)
        k_ref[:, lo:lo + 128] = (kn * r * khw[:, 0:128]).astype(BF16)
        t = kpe * r * khw[:, 128:256]
        k_ref[:, lo + 128:lo + 256] = (t * cos_t + _rope_swap(t) * sin_t).astype(BF16)
        v_ref[:, MLA_V * h:MLA_V * (h + 1)] = kv[:, lo + 128:lo + 256].astype(BF16)


def _mla_prep(p, pos, invf, sign, qnw, wuq, kvnw, wukv, qhw, khw, tm=256):
    t = p.shape[0]
    const = lambda shape: pl.BlockSpec(shape, lambda i: tuple(0 for _ in shape))
    hp = MLA_HEADS * MLA_HEAD_PAD
    return pl.pallas_call(
        _mla_prep_kernel,
        grid=(t // tm,),
        in_specs=[pl.BlockSpec((tm, W_MLA), lambda i: (i, 0)),
                  pl.BlockSpec((tm, 1), lambda i: (i, 0)),
                  const((1, 128)), const((1, 128)),
                  const((1, MLA_Q_RANK)), const((MLA_Q_RANK, hp)),
                  const((1, MLA_KV_RANK)), const((MLA_KV_RANK, hp)),
                  const((1, MLA_HEAD_PAD)), const((1, MLA_HEAD_PAD))],
        out_specs=[pl.BlockSpec((tm, hp), lambda i: (i, 0)),
                   pl.BlockSpec((tm, hp), lambda i: (i, 0)),
                   pl.BlockSpec((tm, MLA_HEADS * MLA_V), lambda i: (i, 0))],
        out_shape=[jax.ShapeDtypeStruct((t, hp), BF16),
                   jax.ShapeDtypeStruct((t, hp), BF16),
                   jax.ShapeDtypeStruct((t, MLA_HEADS * MLA_V), BF16)],
        compiler_params=_params("arbitrary"),
        name="mla_prep",
    )(p, pos, invf, sign, qnw, wuq, kvnw, wukv, qhw, khw)


def _flash_kernel(q_ref, k_ref, v_ref, g_ref, o_ref):
    tq = FLASH_TQ
    i = pl.program_id(2)
    q = q_ref[...]

    def block(j, carry, masked):
        m, l, acc = carry
        rows = pl.ds(pl.multiple_of(j * tq, tq), tq)
        s = _dot_nt(q, k_ref[rows, :])
        if masked:
            row = lax.broadcasted_iota(jnp.int32, (tq, tq), 0)
            col = lax.broadcasted_iota(jnp.int32, (tq, tq), 1)
            s = jnp.where(col <= row, s, -jnp.inf)
        mn = jnp.maximum(m, jnp.max(s, axis=-1, keepdims=True))
        p = jnp.exp(s - mn)
        alpha = jnp.exp(m - mn)
        l = alpha * l + jnp.sum(p, axis=-1, keepdims=True)
        acc = alpha * acc + _dot(p.astype(BF16), v_ref[rows, :])
        return mn, l, acc

    init = (jnp.full((tq, 1), -jnp.inf, F32), jnp.zeros((tq, 1), F32), jnp.zeros((tq, MLA_V), F32))
    carry = lax.fori_loop(0, i, lambda j, c: block(j, c, False), init)
    _, l, acc = block(i, carry, True)
    o_ref[...] = (acc / l * _silu(g_ref[...])).astype(BF16)


def _flash(q, k, v, p, bsz, s):
    nq = s // FLASH_TQ
    gate_blk = (W_MLA - GROUP_W) // MLA_V
    return pl.pallas_call(
        _flash_kernel,
        grid=(bsz, MLA_HEADS, nq),
        in_specs=[pl.BlockSpec((FLASH_TQ, MLA_HEAD_PAD), lambda b, h, i: (b * nq + i, h)),
                  pl.BlockSpec((s, MLA_HEAD_PAD), lambda b, h, i: (b, h)),
                  pl.BlockSpec((s, MLA_V), lambda b, h, i: (b, h)),
                  pl.BlockSpec((FLASH_TQ, MLA_V), lambda b, h, i: (b * nq + i, gate_blk + h))],
        out_specs=pl.BlockSpec((FLASH_TQ, MLA_V), lambda b, h, i: (b * nq + i, h)),
        out_shape=jax.ShapeDtypeStruct((bsz * s, MLA_HEADS * MLA_V), BF16),
        compiler_params=_params("arbitrary", "arbitrary", "arbitrary"),
        name="mla_flash",
    )(q, k, v, p)


def _block_diag(t):
    g, r, c = t.shape
    eye = jnp.eye(g, dtype=t.dtype)
    return (t[:, :, None, :] * eye[:, None, :, None]).reshape(g * r, g * c)


def _pad_cols(w, n):
    return jnp.pad(w, ((0, 0), (0, n - w.shape[1])))


def kernel(x, positions, norm_w, w_in, w_out, ssd_conv_w, ssd_conv_b, ssd_dt_bias, ssd_a_log, ssd_d, ssd_norm_w, s5_a_re, s5_a_im, s5_log_dt, s5_b_re, s5_b_im, s5_c_re, s5_c_im, s5_d, s5_glu_w, s5_glu_b, gla_gate_w2, gla_gate_b, gla_norm_w, mla_q_norm_w, mla_w_uq, mla_kv_norm_w, mla_w_ukv, mla_q_head_norm_w, mla_k_head_norm_w):
    bsz, s, d = x.shape
    depth = w_in.shape[0]
    t = bsz * s
    h = x.reshape(t, d)
    pos = positions.reshape(t, 1)

    inv_freq = ROPE_THETA ** (-jnp.arange(0, MLA_ROPE, 2, dtype=F32) / MLA_ROPE)
    invf = jnp.concatenate([inv_freq, inv_freq, jnp.zeros((64,), F32)])[None, :]
    sign = jnp.concatenate([-jnp.ones((32,), F32), jnp.ones((32,), F32), jnp.zeros((64,), F32)])[None, :]

    u = _rmsnorm_bf16(h, norm_w[0][None, :])
    for l in range(depth):
        wl = w_in[l]
        o = 0
        cols = []
        for wd in (512, 768, 8, 512, 512, 256, 256, 512, 512, 16, 384, 128, 64, 512):
            cols.append(wl[:, o:o + wd])
            o += wd
        (w_z, w_xbc, w_dt, w_s5u, w_s5g, w_gq, w_gk, w_gv, w_gg, w_glr, w_cq, w_ckv, w_kpe, w_mg) = cols
        w_ssd = jnp.concatenate([w_z, w_xbc, jnp.repeat(w_dt, SSD_HEAD_DIM, axis=1)], axis=1).astype(BF16)
        w_s5 = jnp.concatenate([w_s5u, w_s5g], axis=1).astype(BF16)
        w_gla = jnp.concatenate([w_gq, w_gk, w_gv, w_gg, _pad_cols(w_glr, 128)], axis=1).astype(BF16)
        w_mla = jnp.concatenate([w_cq, w_ckv, _pad_cols(w_kpe, 128), w_mg], axis=1).astype(BF16)

        p_ssd = _in_proj(u, w_ssd, "in_proj_ssd")
        p_s5 = _in_proj(u, w_s5, "in_proj_s5")
        p_gla = _in_proj(u, w_gla, "in_proj_gla")
        p_mla = _in_proj(u, w_mla, "in_proj_mla")

        rep = lambda v: jnp.repeat(v, SSD_HEAD_DIM)[None, :]
        y_a = _ssd_branch(p_ssd, ssd_conv_w[l], ssd_conv_b[l][None, :], rep(ssd_dt_bias[l]),
                          rep(ssd_a_log[l]), rep(ssd_d[l]), ssd_norm_w[l][None, :], bsz, s)

        bb, tab = _s5_prep(s5_a_re[l].reshape(1, S5_NS), s5_a_im[l].reshape(1, S5_NS),
                           jnp.repeat(s5_log_dt[l], S5_STATE)[None, :],
                           _block_diag(s5_b_re[l].transpose(0, 2, 1)),
                           _block_diag(s5_b_im[l].transpose(0, 2, 1)))
        y_b = _s5_branch(p_s5, bb, tab,
                         _block_diag(s5_c_re[l].transpose(0, 2, 1)).astype(BF16),
                         _block_diag(s5_c_im[l].transpose(0, 2, 1)).astype(BF16),
                         s5_d[l][None, :], s5_glu_w[l].astype(BF16), s5_glu_b[l][None, :], bsz, s)

        y_c = _gla_branch(p_gla, jnp.pad(gla_gate_w2[l], ((0, 112), (0, 0))).astype(BF16),
                          gla_gate_b[l][None, :], gla_norm_w[l][None, :], bsz, s)

        wuq = jnp.pad(mla_w_uq[l].reshape(MLA_Q_RANK, MLA_HEADS, MLA_NOPE + MLA_ROPE),
                      ((0, 0), (0, 0), (0, 64))).reshape(MLA_Q_RANK, -1).astype(BF16)
        head_pad = lambda w: jnp.pad(w, (0, 64))[None, :]
        q, k, v = _mla_prep(p_mla, pos, invf, sign, mla_q_norm_w[l][None, :], wuq,
                            mla_kv_norm_w[l][None, :], mla_w_ukv[l].astype(BF16),
                            head_pad(mla_q_head_norm_w[l]), head_pad(mla_k_head_norm_w[l]))
        y_d = _flash(q, k, v, p_mla, bsz, s)

        nw_next = norm_w[(l + 1) % depth][None, :]
        h, u = _out_proj((y_a, y_b, y_c, y_d), w_out[l].astype(BF16), h, nw_next)
    return h.reshape(bsz, s, d)
```

```python
import functools
import math

import jax
import jax.numpy as jnp
from jax import lax
from jax.experimental import pallas as pl
from jax.experimental.pallas import tpu as pltpu

F32 = jnp.float32
BF16 = jnp.bfloat16
NORM_EPS = 1e-6
HI = lax.Precision.HIGHEST

D_MODEL = 2048
GROUP_W = 512
SSD_HEADS = 8
SSD_HEAD_DIM = 64
SSD_STATE = 64
SSD_CHUNK = 128
SSD_XBC = 768
SSD_CONV = 4
S5_GROUPS = 32
S5_CH = 16
S5_STATE = 64
S5_NS = S5_GROUPS * S5_STATE
S5_HALF = S5_NS // 2
S5_ROWS = 256
S5_SCAN_LANES = 512
GLA_HEADS = 4
GLA_DK = 64
GLA_DV = 128
GLA_CHUNK = 16
GLA_TILE = 128
GLA_GATE_NORM = 16.0
MLA_HEADS = 4
MLA_NOPE = 128
MLA_ROPE = 64
MLA_V = 128
MLA_Q_RANK = 384
MLA_KV_RANK = 128
MLA_HEAD_PAD = 256
ROPE_THETA = 10000.0
FLASH_TQ = 256

IN_WIDTHS = (512, 768, 8, 512, 512, 256, 256, 512, 512, 16, 384, 128, 64, 512)
W_SSD = 512 + 768 + 512
W_S5 = 1024
W_GLA = 256 + 256 + 512 + 512 + 128
W_MLA = 384 + 128 + 128 + 512

VMEM_LIMIT_BYTES = 56 * 1024 * 1024


def _params(*sem):
    return pltpu.CompilerParams(dimension_semantics=sem, vmem_limit_bytes=VMEM_LIMIT_BYTES)


def _layer(shape, l):
    return pl.BlockSpec((None,) + tuple(shape), lambda *_: (l,) + (0,) * len(shape))


def _silu(x):
    return x / (1.0 + jnp.exp(-x))


def _softplus(x):
    return jnp.maximum(x, 0.0) + jnp.log(1.0 + jnp.exp(-jnp.abs(x)))


def _dot(a, b):
    return jnp.dot(a, b, preferred_element_type=F32)


def _dot_nt(a, b):
    return lax.dot_general(a, b, (((1,), (1,)), ((), ())), preferred_element_type=F32)


def _dot_tn(a, b):
    return lax.dot_general(a, b, (((0,), (0,)), ((), ())), preferred_element_type=F32)


def _rms_kernel(x_ref, w_ref, o_ref):
    x = x_ref[...]
    ms = jnp.mean(x * x, axis=-1, keepdims=True)
    o_ref[...] = (x * lax.rsqrt(ms + NORM_EPS) * w_ref[...]).astype(BF16)


def _rmsnorm_bf16(x, w, l, tm=512):
    t, d = x.shape
    return pl.pallas_call(
        _rms_kernel,
        grid=(t // tm,),
        in_specs=[pl.BlockSpec((tm, d), lambda i: (i, 0)), _layer((1, d), l)],
        out_specs=pl.BlockSpec((tm, d), lambda i: (i, 0)),
        out_shape=jax.ShapeDtypeStruct((t, d), BF16),
        compiler_params=_params("arbitrary"),
        name="rmsnorm_in",
    )(x, w)


def _mm_kernel(x_ref, w_ref, o_ref):
    o_ref[...] = _dot(x_ref[...], w_ref[...])


def _in_proj(u, w, l, name, tm=512):
    t, k = u.shape
    n = w.shape[-1]
    return pl.pallas_call(
        _mm_kernel,
        grid=(t // tm,),
        in_specs=[pl.BlockSpec((tm, k), lambda i: (i, 0)), _layer((k, n), l)],
        out_specs=pl.BlockSpec((tm, n), lambda i: (i, 0)),
        out_shape=jax.ShapeDtypeStruct((t, n), F32),
        compiler_params=_params("arbitrary"),
        name=name,
    )(u, w)


def _out_proj_kernel(ya_ref, yb_ref, yc_ref, yd_ref, w_ref, h_ref, nw_ref, ho_ref, uo_ref):
    acc = h_ref[...]
    for i, y_ref in enumerate((ya_ref, yb_ref, yc_ref, yd_ref)):
        acc = acc + _dot(y_ref[...], w_ref[GROUP_W * i:GROUP_W * (i + 1), :])
    ho_ref[...] = acc
    ms = jnp.mean(acc * acc, axis=-1, keepdims=True)
    uo_ref[...] = (acc * lax.rsqrt(ms + NORM_EPS) * nw_ref[...]).astype(BF16)


def _out_proj(ys, w, h, nw, l, l_next, tm=256):
    t, d = h.shape
    yspec = pl.BlockSpec((tm, GROUP_W), lambda i: (i, 0))
    return pl.pallas_call(
        _out_proj_kernel,
        grid=(t // tm,),
        in_specs=[yspec, yspec, yspec, yspec,
                  _layer((d, d), l),
                  pl.BlockSpec((tm, d), lambda i: (i, 0)),
                  _layer((1, d), l_next)],
        out_specs=[pl.BlockSpec((tm, d), lambda i: (i, 0)), pl.BlockSpec((tm, d), lambda i: (i, 0))],
        out_shape=[jax.ShapeDtypeStruct((t, d), F32), jax.ShapeDtypeStruct((t, d), BF16)],
        compiler_params=_params("arbitrary"),
        name="out_proj",
    )(*ys, w, h, nw)


def _ssd_kernel(p_ref, cw_ref, cb_ref, dtb_ref, alog_ref, d_ref, nw_ref, o_ref, cbuf, st_ref):
    L = SSD_CHUNK

    @pl.when(pl.program_id(1) == 0)
    def _():
        cbuf[0:8, :] = jnp.zeros((8, SSD_XBC), F32)
        st_ref[...] = jnp.zeros_like(st_ref)

    z = p_ref[:, 0:512]
    cbuf[8:8 + L, :] = p_ref[:, 512:512 + SSD_XBC]
    acc = cb_ref[...] + cbuf[pl.ds(8 - (SSD_CONV - 1), L), :] * cw_ref[0:1, :]
    for k in range(1, SSD_CONV):
        acc = acc + cbuf[pl.ds(8 - (SSD_CONV - 1) + k, L), :] * cw_ref[k:k + 1, :]
    cbuf[0:8, :] = cbuf[L:L + 8, :]
    xbc = _silu(acc)
    xs = xbc[:, 0:512]
    bm = xbc[:, 512:640]
    cm = xbc[:, 640:768]

    dt = _softplus(p_ref[:, 1280:1792] + dtb_ref[...])
    a = -jnp.exp(alog_ref[...])
    adt = dt * a
    row = lax.broadcasted_iota(jnp.int32, (L, L), 0)
    col = lax.broadcasted_iota(jnp.int32, (L, L), 1)
    causal = col <= row
    tri = causal.astype(F32)
    cs = jnp.dot(tri, adt, preferred_element_type=F32, precision=HI)
    cs_t = jnp.transpose(cs)
    xdt = (xs * dt).astype(BF16)

    ys = []
    for g in range(2):
        bg = bm[:, 64 * g:64 * g + 64]
        cg = cm[:, 64 * g:64 * g + 64]
        gmat = _dot_nt(cg.astype(BF16), bg.astype(BF16))
        for hh in range(4):
            h = 4 * g + hh
            lo = 64 * h
            cs_col = cs[:, lo:lo + 1]
            cs_row = cs_t[lo:lo + 1, :]
            lmat = jnp.exp(jnp.where(causal, cs_col - cs_row, -jnp.inf))
            xh = xdt[:, lo:lo + 64]
            y = _dot((gmat * lmat).astype(BF16), xh)
            st = st_ref[h]
            y = y + _dot((cg * jnp.exp(cs_col)).astype(BF16), st.astype(BF16))
            cs_last = cs[L - 1:L, lo:lo + 1]
            bdec = bg * jnp.exp(cs_last - cs_col)
            st_ref[h] = jnp.exp(cs_last) * st + _dot_tn(bdec.astype(BF16), xh)
            ys.append(y)
    y = jnp.concatenate(ys, axis=1) + d_ref[...] * xs
    y = y * _silu(z)
    ms = jnp.mean(y * y, axis=-1, keepdims=True)
    o_ref[...] = (y * lax.rsqrt(ms + NORM_EPS) * nw_ref[...]).astype(BF16)


def _ssd_branch(l, p, cw, cb, dtb, alog, dsk, nw, bsz, s):
    nc = s // SSD_CHUNK
    return pl.pallas_call(
        _ssd_kernel,
        grid=(bsz, nc),
        in_specs=[pl.BlockSpec((SSD_CHUNK, W_SSD), lambda b, c: (b * nc + c, 0)),
                  _layer((SSD_CONV, SSD_XBC), l), _layer((1, SSD_XBC), l),
                  _layer((1, 512), l), _layer((1, 512), l), _layer((1, 512), l), _layer((1, 512), l)],
        out_specs=pl.BlockSpec((SSD_CHUNK, GROUP_W), lambda b, c: (b * nc + c, 0)),
        out_shape=jax.ShapeDtypeStruct((bsz * s, GROUP_W), BF16),
        scratch_shapes=[pltpu.VMEM((SSD_CHUNK + 8, SSD_XBC), F32),
                        pltpu.VMEM((SSD_HEADS, SSD_STATE, SSD_HEAD_DIM), F32)],
        compiler_params=_params("arbitrary", "arbitrary"),
        name="ssd_branch",
    )(p, cw, cb, dtb, alog, dsk, nw)


def _cmul(ar, ai, br, bi):
    return ar * br - ai * bi, ar * bi + ai * br


def _s5_prep_kernel(are_ref, aim_ref, ldt_ref, bre_ref, bim_ref, bb_ref, tab_ref):
    are = are_ref[...]
    aim = aim_ref[...]
    delta = jnp.exp(ldt_ref[...])
    mag = jnp.exp(are * delta)
    ar = mag * jnp.cos(aim * delta)
    ai = mag * jnp.sin(aim * delta)
    den = are * are + aim * aim
    coef_re = ((ar - 1.0) * are + ai * aim) / den
    coef_im = (ai * are - (ar - 1.0) * aim) / den
    for kb in range(2):
        st = slice(S5_HALF * kb, S5_HALF * (kb + 1))
        bre = bre_ref[kb]
        bim = bim_ref[kb]
        bb_ref[kb, :, 0:S5_HALF] = (coef_re[:, st] * bre - coef_im[:, st] * bim).astype(BF16)
        bb_ref[kb, :, S5_HALF:2 * S5_HALF] = (coef_re[:, st] * bim + coef_im[:, st] * bre).astype(BF16)

    pw = [(ar, ai)]
    for k in range(2, 9):
        pw.append(_cmul(*pw[k // 2 - 1], *pw[k - k // 2 - 1]))
    row = lax.broadcasted_iota(jnp.int32, (8, S5_NS), 0)
    for i, s in enumerate((1, 2, 4)):
        tab_ref[2 * i] = jnp.where(row >= s, pw[s - 1][0], 0.0)
        tab_ref[2 * i + 1] = jnp.where(row >= s, pw[s - 1][1], 0.0)
    cr = jnp.zeros((8, S5_NS), F32)
    ci = jnp.zeros((8, S5_NS), F32)
    for r in range(8):
        cr = jnp.where(row == r, pw[r][0], cr)
        ci = jnp.where(row == r, pw[r][1], ci)
    tab_ref[6] = cr
    tab_ref[7] = ci


def _s5_prep(l, are, aim, ldt, bre, bim):
    return pl.pallas_call(
        _s5_prep_kernel,
        grid=(1,),
        in_specs=[_layer((1, S5_NS), l), _layer((1, S5_NS), l), _layer((1, S5_NS), l),
                  _layer((2, 256, S5_HALF), l), _layer((2, 256, S5_HALF), l)],
        out_specs=[pl.BlockSpec((2, 256, 2 * S5_HALF), lambda i: (0, 0, 0)),
                   pl.BlockSpec((8, 8, S5_NS), lambda i: (0, 0, 0))],
        out_shape=[jax.ShapeDtypeStruct((2, 256, 2 * S5_HALF), BF16),
                   jax.ShapeDtypeStruct((8, 8, S5_NS), F32)],
        compiler_params=_params("arbitrary"),
        name="s5_prep",
    )(are, aim, ldt, bre, bim)


def _s5_kernel(p_ref, bb_ref, tab_ref, cc_ref, d_ref, gw_ref, gb_ref, o_ref, hbuf, hbf, carry):
    R = S5_ROWS
    W = S5_SCAN_LANES

    @pl.when(pl.program_id(1) == 0)
    def _():
        carry[...] = jnp.zeros_like(carry)

    u = p_ref[:, 0:512]
    ub = u.astype(BF16)
    ys = []
    for kb in range(2):
        blk = slice(2 * S5_HALF * kb, 2 * S5_HALF * (kb + 1))
        hbuf[:, blk] = _dot(ub[:, 256 * kb:256 * (kb + 1)], bb_ref[kb])
        for lc in range(S5_HALF // W):
            re = slice(2 * S5_HALF * kb + W * lc, 2 * S5_HALF * kb + W * (lc + 1))
            im = slice(re.start + S5_HALF, re.stop + S5_HALF)
            ts = slice(S5_HALF * kb + W * lc, S5_HALF * kb + W * (lc + 1))

            def body(r, c, re=re, im=im, ts=ts):
                cr, ci = c
                base = pl.multiple_of(r * 16, 16)
                out_r, out_i = [], []
                for sub in range(2):
                    rows = pl.ds(base + 8 * sub, 8)
                    hr = hbuf[rows, re]
                    hi = hbuf[rows, im]
                    for i, s in enumerate((1, 2, 4)):
                        tr = tab_ref[2 * i, :, ts]
                        ti = tab_ref[2 * i + 1, :, ts]
                        rr = pltpu.roll(hr, s, 0)
                        ri = pltpu.roll(hi, s, 0)
                        hr, hi = hr + (tr * rr - ti * ri), hi + (tr * ri + ti * rr)
                    pr = tab_ref[6, :, ts]
                    pi = tab_ref[7, :, ts]
                    hr = hr + (pr * cr - pi * ci)
                    hi = hi + (pr * ci + pi * cr)
                    out_r.append(hr)
                    out_i.append(hi)
                    cr = jnp.broadcast_to(hr[7:8, :], (8, W))
                    ci = jnp.broadcast_to(hi[7:8, :], (8, W))
                hbf[pl.ds(base, 16), re] = jnp.concatenate(out_r, axis=0).astype(BF16)
                hbf[pl.ds(base, 16), im] = jnp.concatenate(out_i, axis=0).astype(BF16)
                return cr, ci

            cr, ci = lax.fori_loop(0, R // 16, body, (carry[:, re], carry[:, im]))
            carry[:, re] = cr
            carry[:, im] = ci
        ys.append(_dot(hbf[:, blk], cc_ref[kb]))
    y = jnp.concatenate(ys, axis=1)
    y = y + d_ref[...] * u
    y = 0.5 * y * (1.0 + jnp.tanh(math.sqrt(2.0 / math.pi) * (y + 0.044715 * (y * y * y))))
    glu = _dot(y.astype(BF16), gw_ref[...]) + gb_ref[...]
    y = y / (1.0 + jnp.exp(-glu))
    o_ref[...] = (y * _silu(p_ref[:, 512:1024])).astype(BF16)


def _s5_branch(l, p, bb, tab, cc, dsk, gw, gb, bsz, s):
    nt = s // S5_ROWS
    const = lambda shape: pl.BlockSpec(shape, lambda b, t: tuple(0 for _ in shape))
    return pl.pallas_call(
        _s5_kernel,
        grid=(bsz, nt),
        in_specs=[pl.BlockSpec((S5_ROWS, W_S5), lambda b, t: (b * nt + t, 0)),
                  const((2, 256, 2 * S5_HALF)), const((8, 8, S5_NS)),
                  _layer((2, 2 * S5_HALF, 256), l),
                  _layer((1, GROUP_W), l), _layer((GROUP_W, GROUP_W), l), _layer((1, GROUP_W), l)],
        out_specs=pl.BlockSpec((S5_ROWS, GROUP_W), lambda b, t: (b * nt + t, 0)),
        out_shape=jax.ShapeDtypeStruct((bsz * s, GROUP_W), BF16),
        scratch_shapes=[pltpu.VMEM((S5_ROWS, 2 * S5_NS), F32),
                        pltpu.VMEM((S5_ROWS, 2 * S5_NS), BF16),
                        pltpu.VMEM((8, 2 * S5_NS), F32)],
        compiler_params=_params("arbitrary", "arbitrary"),
        name="s5_branch",
    )(p, bb, tab, cc, dsk, gw, gb)


def _gla_kernel(p_ref, w2_ref, gb_ref, nw_ref, o_ref, st_ref):
    T = GLA_TILE
    NCH = T // GLA_CHUNK

    @pl.when(pl.program_id(1) == 0)
    def _():
        st_ref[...] = jnp.zeros_like(st_ref)

    x = _dot(p_ref[:, 1536:1664].astype(BF16), w2_ref[...]) + gb_ref[...]
    g = -_softplus(-x) * (1.0 / GLA_GATE_NORM)
    row = lax.broadcasted_iota(jnp.int32, (T, T), 0)
    col = lax.broadcasted_iota(jnp.int32, (T, T), 1)
    same = (row // GLA_CHUNK) == (col // GLA_CHUNK)
    intra = jnp.logical_and(same, col <= row)
    b = jnp.dot(intra.astype(F32), g, preferred_element_type=F32, precision=HI)
    ones_blk = same.astype(F32)
    blast = jnp.dot(ones_blk, g, preferred_element_type=F32, precision=HI)
    blast_t = lax.dot_general(g, ones_blk, (((0,), (0,)), ((), ())),
                              preferred_element_type=F32, precision=HI)
    scale = GLA_DK ** -0.5
    q = p_ref[:, 0:256] * scale
    k = p_ref[:, 256:512]
    half = 0.5 * blast
    q_mid = (q * jnp.exp(b - half)).astype(BF16)
    k_mid = (k * jnp.exp(half - b)).astype(BF16)
    q_dec = (q * jnp.exp(b)).astype(BF16)
    k_dec = (k * jnp.exp(blast - b)).astype(BF16)
    rsel = lax.broadcasted_iota(jnp.int32, (T, NCH * GLA_DK), 0) // GLA_CHUNK
    csel = lax.broadcasted_iota(jnp.int32, (T, NCH * GLA_DK), 1) // GLA_DK
    blockmask = rsel == csel

    for h in range(GLA_HEADS):
        lo = GLA_DK * h
        v = p_ref[:, 512 + GLA_DV * h:512 + GLA_DV * (h + 1)].astype(BF16)
        attn = jnp.where(intra, _dot_nt(q_mid[:, lo:lo + GLA_DK], k_mid[:, lo:lo + GLA_DK]), 0.0)
        o = _dot(attn.astype(BF16), v)
        zero = jnp.zeros((), BF16)
        qcat = jnp.where(blockmask, jnp.tile(q_dec[:, lo:lo + GLA_DK], (1, NCH)), zero)
        kcat = jnp.where(blockmask, jnp.tile(k_dec[:, lo:lo + GLA_DK], (1, NCH)), zero)
        ds = _dot_tn(kcat, v)
        st = st_ref[h]
        sts = []
        for c in range(NCH):
            sts.append(st)
            dec = jnp.exp(blast_t[lo:lo + GLA_DK, GLA_CHUNK * c:GLA_CHUNK * c + 1])
            st = dec * st + ds[GLA_DK * c:GLA_DK * (c + 1), :]
        st_ref[h] = st
        o = o + _dot(qcat, jnp.concatenate(sts, axis=0).astype(BF16))
        ms = jnp.mean(o * o, axis=-1, keepdims=True)
        o = o * lax.rsqrt(ms + NORM_EPS) * nw_ref[...]
        gate = p_ref[:, 1024 + GLA_DV * h:1024 + GLA_DV * (h + 1)]
        o_ref[:, GLA_DV * h:GLA_DV * (h + 1)] = (o * _silu(gate)).astype(BF16)


def _gla_branch(l, p, w2, gb, nw, bsz, s):
    nt = s // GLA_TILE
    return pl.pallas_call(
        _gla_kernel,
        grid=(bsz, nt),
        in_specs=[pl.BlockSpec((GLA_TILE, W_GLA), lambda b, t: (b * nt + t, 0)),
                  _layer((128, 256), l), _layer((1, 256), l), _layer((1, GLA_DV), l)],
        out_specs=pl.BlockSpec((GLA_TILE, GROUP_W), lambda b, t: (b * nt + t, 0)),
        out_shape=jax.ShapeDtypeStruct((bsz * s, GROUP_W), BF16),
        scratch_shapes=[pltpu.VMEM((GLA_HEADS, GLA_DK, GLA_DV), F32)],
        compiler_params=_params("arbitrary", "arbitrary"),
        name="gla_branch",
    )(p, w2, gb, nw)


def _rope_swap(t):
    lane = lax.broadcasted_iota(jnp.int32, t.shape, 1)
    return jnp.where(lane < MLA_ROPE // 2, pltpu.roll(t, 128 - MLA_ROPE // 2, 1), pltpu.roll(t, MLA_ROPE // 2, 1))


def _mla_prep_kernel(p_ref, pos_ref, invf_ref, sign_ref, qnw_ref, wuq_ref, kvnw_ref, wukv_ref,
                     qhw_ref, khw_ref, q_ref, k_ref, v_ref):
    cq = p_ref[:, 0:384]
    ckv = p_ref[:, 384:512]
    kpe = p_ref[:, 512:640]
    ms = jnp.mean(cq * cq, axis=-1, keepdims=True)
    qn = (cq * lax.rsqrt(ms + NORM_EPS) * qnw_ref[...]).astype(BF16)
    ms = jnp.mean(ckv * ckv, axis=-1, keepdims=True)
    kvn = (ckv * lax.rsqrt(ms + NORM_EPS) * kvnw_ref[...]).astype(BF16)
    q = _dot(qn, wuq_ref[...])
    kv = _dot(kvn, wukv_ref[...])
    ang = pos_ref[...].astype(F32) * invf_ref[...]
    cos_t = jnp.cos(ang)
    sin_t = jnp.sin(ang) * sign_ref[...]
    inv_d = 1.0 / (MLA_NOPE + MLA_ROPE)
    scale = (MLA_NOPE + MLA_ROPE) ** -0.5
    kpe_ss = jnp.sum(kpe * kpe, axis=-1, keepdims=True)
    qhw = qhw_ref[...]
    khw = khw_ref[...]
    for h in range(MLA_HEADS):
        lo = MLA_HEAD_PAD * h
        qh = q[:, lo:lo + MLA_HEAD_PAD]
        r = lax.rsqrt(jnp.sum(qh * qh, axis=-1, keepdims=True) * inv_d + NORM_EPS) * scale
        qh = qh * r * qhw
        t = qh[:, 128:256]
        q_ref[:, lo:lo + 128] = qh[:, 0:128].astype(BF16)
        q_ref[:, lo + 128:lo + 256] = (t * cos_t + _rope_swap(t) * sin_t).astype(BF16)
        kn = kv[:, lo:lo + 128]
        r = lax.rsqrt((jnp.sum(kn * kn, axis=-1, keepdims=True) + kpe_ss) * inv_d + NORM_EPS)
        k_ref[:, lo:lo + 128] = (kn * r * khw[:, 0:128]).astype(BF16)
        t = kpe * r * khw[:, 128:256]
        k_ref[:, lo + 128:lo + 256] = (t * cos_t + _rope_swap(t) * sin_t).astype(BF16)
        v_ref[:, MLA_V * h:MLA_V * (h + 1)] = kv[:, lo + 128:lo + 256].astype(BF16)


def _mla_prep(l, p, pos, invf, sign, qnw, wuq, kvnw, wukv, qhw, khw, tm=256):
    t = p.shape[0]
    const = lambda shape: pl.BlockSpec(shape, lambda i: tuple(0 for _ in shape))
    hp = MLA_HEADS * MLA_HEAD_PAD
    return pl.pallas_call(
        _mla_prep_kernel,
        grid=(t // tm,),
        in_specs=[pl.BlockSpec((tm, W_MLA), lambda i: (i, 0)),
                  pl.BlockSpec((tm, 1), lambda i: (i, 0)),
                  const((1, 128)), const((1, 128)),
                  _layer((1, MLA_Q_RANK), l), _layer((MLA_Q_RANK, hp), l),
                  _layer((1, MLA_KV_RANK), l), _layer((MLA_KV_RANK, hp), l),
                  _layer((1, MLA_HEAD_PAD), l), _layer((1, MLA_HEAD_PAD), l)],
        out_specs=[pl.BlockSpec((tm, hp), lambda i: (i, 0)),
                   pl.BlockSpec((tm, hp), lambda i: (i, 0)),
                   pl.BlockSpec((tm, MLA_HEADS * MLA_V), lambda i: (i, 0))],
        out_shape=[jax.ShapeDtypeStruct((t, hp), BF16),
                   jax.ShapeDtypeStruct((t, hp), BF16),
                   jax.ShapeDtypeStruct((t, MLA_HEADS * MLA_V), BF16)],
        compiler_params=_params("arbitrary"),
        name="mla_prep",
    )(p, pos, invf, sign, qnw, wuq, kvnw, wukv, qhw, khw)


def _flash_kernel(q_ref, k_ref, v_ref, p_ref, o_ref, s_ref):
    tq = FLASH_TQ
    hp = MLA_HEAD_PAD
    row = lax.broadcasted_iota(jnp.int32, (tq, tq), 0)
    col = lax.broadcasted_iota(jnp.int32, (tq, tq), 1)
    heads = range(MLA_HEADS)

    def attend(nb):
        q = [q_ref[:, hp * h:hp * (h + 1)] for h in heads]
        m = [None] * MLA_HEADS
        for j in range(nb):
            for h in heads:
                s = _dot_nt(q[h], k_ref[tq * j:tq * (j + 1), hp * h:hp * (h + 1)])
                if j == nb - 1:
                    s = jnp.where(col <= row, s, -jnp.inf)
                s_ref[h, :, tq * j:tq * (j + 1)] = s
                sm = jnp.maximum(s[:, 0:128], s[:, 128:256])
                m[h] = sm if m[h] is None else jnp.maximum(m[h], sm)
        m = [jnp.max(mh, axis=-1, keepdims=True) for mh in m]
        l = [jnp.zeros((tq, 128), F32) for _ in heads]
        acc = [jnp.zeros((tq, MLA_V), F32) for _ in heads]
        for j in range(nb):
            for h in heads:
                p = jnp.exp(s_ref[h, :, tq * j:tq * (j + 1)] - m[h])
                l[h] = l[h] + (p[:, 0:128] + p[:, 128:256])
                acc[h] = acc[h] + _dot(p.astype(BF16), v_ref[tq * j:tq * (j + 1), MLA_V * h:MLA_V * (h + 1)])
        for h in heads:
            lh = jnp.sum(l[h], axis=-1, keepdims=True)
            gate = p_ref[:, W_MLA - GROUP_W + MLA_V * h:W_MLA - GROUP_W + MLA_V * (h + 1)]
            o_ref[:, MLA_V * h:MLA_V * (h + 1)] = (acc[h] / lh * _silu(gate)).astype(BF16)

    for nb in range(1, s_ref.shape[2] // tq + 1):
        pl.when(pl.program_id(1) == nb - 1)(functools.partial(attend, nb))


def _flash(q, k, v, p, bsz, s):
    nq = s // FLASH_TQ
    hp = MLA_HEADS * MLA_HEAD_PAD
    return pl.pallas_call(
        _flash_kernel,
        grid=(bsz, nq),
        in_specs=[pl.BlockSpec((FLASH_TQ, hp), lambda b, i: (b * nq + i, 0)),
                  pl.BlockSpec((s, hp), lambda b, i: (b, 0)),
                  pl.BlockSpec((s, MLA_HEADS * MLA_V), lambda b, i: (b, 0)),
                  pl.BlockSpec((FLASH_TQ, W_MLA), lambda b, i: (b * nq + i, 0))],
        out_specs=pl.BlockSpec((FLASH_TQ, MLA_HEADS * MLA_V), lambda b, i: (b * nq + i, 0)),
        out_shape=jax.ShapeDtypeStruct((bsz * s, MLA_HEADS * MLA_V), BF16),
        scratch_shapes=[pltpu.VMEM((MLA_HEADS, FLASH_TQ, s), F32)],
        compiler_params=_params("arbitrary", "arbitrary"),
        name="mla_flash",
    )(q, k, v, p)


def _s5_block_diag(t):
    depth, _, r, c = t.shape
    t = t.reshape(depth, 2, 16, r, 1, c)
    eye = jnp.eye(16, dtype=t.dtype)[:, None, :, None]
    return (t * eye).reshape(depth, 2, 16 * r, 16 * c)


def _pad_last(w, n):
    return jnp.pad(w, [(0, 0)] * (w.ndim - 1) + [(0, n - w.shape[-1])])


def kernel(x, positions, norm_w, w_in, w_out, ssd_conv_w, ssd_conv_b, ssd_dt_bias, ssd_a_log, ssd_d, ssd_norm_w, s5_a_re, s5_a_im, s5_log_dt, s5_b_re, s5_b_im, s5_c_re, s5_c_im, s5_d, s5_glu_w, s5_glu_b, gla_gate_w2, gla_gate_b, gla_norm_w, mla_q_norm_w, mla_w_uq, mla_kv_norm_w, mla_w_ukv, mla_q_head_norm_w, mla_k_head_norm_w):
    bsz, s, d = x.shape
    depth = w_in.shape[0]
    t = bsz * s
    h = x.reshape(t, d)
    pos = positions.reshape(t, 1)
    row = lambda v: v[:, None, :]

    inv_freq = ROPE_THETA ** (-jnp.arange(0, MLA_ROPE, 2, dtype=F32) / MLA_ROPE)
    invf = jnp.concatenate([inv_freq, inv_freq, jnp.zeros((64,), F32)])[None, :]
    sign = jnp.concatenate([-jnp.ones((32,), F32), jnp.ones((32,), F32), jnp.zeros((64,), F32)])[None, :]

    offs = [0]
    for wd in IN_WIDTHS:
        offs.append(offs[-1] + wd)
    (w_z, w_xbc, w_dt, w_s5u, w_s5g, w_gq, w_gk, w_gv, w_gg, w_glr, w_cq, w_ckv, w_kpe, w_mg) = [
        w_in[:, :, a:b] for a, b in zip(offs[:-1], offs[1:])]
    w_ssd = jnp.concatenate([w_z, w_xbc, jnp.repeat(w_dt, SSD_HEAD_DIM, axis=2)], axis=2).astype(BF16)
    w_s5 = jnp.concatenate([w_s5u, w_s5g], axis=2).astype(BF16)
    w_gla = jnp.concatenate([w_gq, w_gk, w_gv, w_gg, _pad_last(w_glr, 128)], axis=2).astype(BF16)
    w_mla = jnp.concatenate([w_cq, w_ckv, _pad_last(w_kpe, 128), w_mg], axis=2).astype(BF16)
    w_out_b = w_out.astype(BF16)
    norm_w3 = row(norm_w)

    rep = lambda v: row(jnp.repeat(v, SSD_HEAD_DIM, axis=1))
    ssd_cb, ssd_dtb, ssd_alog, ssd_dsk, ssd_nw = (row(ssd_conv_b), rep(ssd_dt_bias), rep(ssd_a_log),
                                                   rep(ssd_d), row(ssd_norm_w))

    s5_are = s5_a_re.reshape(depth, 1, S5_NS)
    s5_aim = s5_a_im.reshape(depth, 1, S5_NS)
    s5_ldt = row(jnp.repeat(s5_log_dt, S5_STATE, axis=1))
    s5_bre = _s5_block_diag(s5_b_re.transpose(0, 1, 3, 2))
    s5_bim = _s5_block_diag(s5_b_im.transpose(0, 1, 3, 2))
    s5_cc = jnp.concatenate([_s5_block_diag(s5_c_re.transpose(0, 1, 3, 2)),
                             -_s5_block_diag(s5_c_im.transpose(0, 1, 3, 2))], axis=2).astype(BF16)
    s5_dsk, s5_gw, s5_gb = row(s5_d), s5_glu_w.astype(BF16), row(s5_glu_b)

    gla_w2 = jnp.pad(gla_gate_w2, ((0, 0), (0, 112), (0, 0))).astype(BF16)
    gla_gb, gla_nw = row(gla_gate_b), row(gla_norm_w)

    wuq = _pad_last(mla_w_uq.reshape(depth, MLA_Q_RANK, MLA_HEADS, MLA_NOPE + MLA_ROPE), MLA_HEAD_PAD)
    wuq = wuq.reshape(depth, MLA_Q_RANK, MLA_HEADS * MLA_HEAD_PAD).astype(BF16)
    wukv = mla_w_ukv.astype(BF16)
    qnw, kvnw = row(mla_q_norm_w), row(mla_kv_norm_w)
    qhw, khw = row(_pad_last(mla_q_head_norm_w, MLA_HEAD_PAD)), row(_pad_last(mla_k_head_norm_w, MLA_HEAD_PAD))

    u = _rmsnorm_bf16(h, norm_w3, 0)
    for l in range(depth):
        p_ssd = _in_proj(u, w_ssd, l, "in_proj_ssd")
        p_s5 = _in_proj(u, w_s5, l, "in_proj_s5")
        p_gla = _in_proj(u, w_gla, l, "in_proj_gla")
        p_mla = _in_proj(u, w_mla, l, "in_proj_mla")

        y_a = _ssd_branch(l, p_ssd, ssd_conv_w, ssd_cb, ssd_dtb, ssd_alog, ssd_dsk, ssd_nw, bsz, s)

        bb, tab = _s5_prep(l, s5_are, s5_aim, s5_ldt, s5_bre, s5_bim)
        y_b = _s5_branch(l, p_s5, bb, tab, s5_cc, s5_dsk, s5_gw, s5_gb, bsz, s)

        y_c = _gla_branch(l, p_gla, gla_w2, gla_gb, gla_nw, bsz, s)

        q, k, v = _mla_prep(l, p_mla, pos, invf, sign, qnw, wuq, kvnw, wukv, qhw, khw)
        y_d = _flash(q, k, v, p_mla, bsz, s)

        h, u = _out_proj((y_a, y_b, y_c, y_d), w_out_b, h, norm_w3, l, (l + 1) % depth)
    return h.reshape(bsz, s, d)
```

```python
import functools
import math

import jax
import jax.numpy as jnp
from jax import lax
from jax.experimental import pallas as pl
from jax.experimental.pallas import tpu as pltpu

F32 = jnp.float32
BF16 = jnp.bfloat16
NORM_EPS = 1e-6
HI = lax.Precision.HIGHEST

D_MODEL = 2048
GROUP_W = 512
SSD_HEADS = 8
SSD_HEAD_DIM = 64
SSD_STATE = 64
SSD_CHUNK = 128
SSD_XBC = 768
SSD_CONV = 4
S5_GROUPS = 32
S5_CH = 16
S5_STATE = 64
S5_NS = S5_GROUPS * S5_STATE
S5_HALF = S5_NS // 2
S5_TSTEP = 64
S5_SCAN_LANES = 512
GLA_HEADS = 4
GLA_DK = 64
GLA_DV = 128
GLA_CHUNK = 16
GLA_TILE = 128
GLA_GATE_NORM = 16.0
MLA_HEADS = 4
MLA_NOPE = 128
MLA_ROPE = 64
MLA_V = 128
MLA_Q_RANK = 384
MLA_KV_RANK = 128
MLA_HEAD_PAD = 256
ROPE_THETA = 10000.0
FLASH_TQ = 256

IN_WIDTHS = (512, 768, 8, 512, 512, 256, 256, 512, 512, 16, 384, 128, 64, 512)
W_SSD = 512 + 768 + 512
W_S5 = 1024
W_GLA = 256 + 256 + 512 + 512 + 128
W_MLA = 384 + 128 + 128 + 512

VMEM_LIMIT_BYTES = 56 * 1024 * 1024


def _params(*sem):
    return pltpu.CompilerParams(dimension_semantics=sem, vmem_limit_bytes=VMEM_LIMIT_BYTES)


def _layer(shape, l):
    return pl.BlockSpec((None,) + tuple(shape), lambda *_: (l,) + (0,) * len(shape))


def _silu(x):
    return x / (1.0 + jnp.exp(-x))


def _softplus(x):
    return jnp.maximum(x, 0.0) + jnp.log(1.0 + jnp.exp(-jnp.abs(x)))


def _dot(a, b):
    return jnp.dot(a, b, preferred_element_type=F32)


def _dot_nt(a, b):
    return lax.dot_general(a, b, (((1,), (1,)), ((), ())), preferred_element_type=F32)


def _dot_tn(a, b):
    return lax.dot_general(a, b, (((0,), (0,)), ((), ())), preferred_element_type=F32)


def _rms_kernel(x_ref, w_ref, o_ref):
    x = x_ref[...]
    ms = jnp.mean(x * x, axis=-1, keepdims=True)
    o_ref[...] = (x * lax.rsqrt(ms + NORM_EPS) * w_ref[...]).astype(BF16)


def _rmsnorm_bf16(x, w, l, tm=512):
    t, d = x.shape
    return pl.pallas_call(
        _rms_kernel,
        grid=(t // tm,),
        in_specs=[pl.BlockSpec((tm, d), lambda i: (i, 0)), _layer((1, d), l)],
        out_specs=pl.BlockSpec((tm, d), lambda i: (i, 0)),
        out_shape=jax.ShapeDtypeStruct((t, d), BF16),
        compiler_params=_params("arbitrary"),
        name="rmsnorm_in",
    )(x, w)


def _mm_kernel(x_ref, w_ref, o_ref):
    o_ref[...] = _dot(x_ref[...], w_ref[...])


def _in_proj(u, w, l, name, tm=512):
    t, k = u.shape
    n = w.shape[-1]
    return pl.pallas_call(
        _mm_kernel,
        grid=(t // tm,),
        in_specs=[pl.BlockSpec((tm, k), lambda i: (i, 0)), _layer((k, n), l)],
        out_specs=pl.BlockSpec((tm, n), lambda i: (i, 0)),
        out_shape=jax.ShapeDtypeStruct((t, n), F32),
        compiler_params=_params("arbitrary"),
        name=name,
    )(u, w)


def _out_proj_kernel(ya_ref, yb_ref, yc_ref, yd_ref, w_ref, h_ref, nw_ref, ho_ref, uo_ref):
    acc = h_ref[...]
    for i, y_ref in enumerate((ya_ref, yb_ref, yc_ref, yd_ref)):
        acc = acc + _dot(y_ref[...], w_ref[GROUP_W * i:GROUP_W * (i + 1), :])
    ho_ref[...] = acc
    ms = jnp.mean(acc * acc, axis=-1, keepdims=True)
    uo_ref[...] = (acc * lax.rsqrt(ms + NORM_EPS) * nw_ref[...]).astype(BF16)


def _out_proj(ys, w, h, nw, l, l_next, tm=256):
    t, d = h.shape
    yspec = pl.BlockSpec((tm, GROUP_W), lambda i: (i, 0))
    return pl.pallas_call(
        _out_proj_kernel,
        grid=(t // tm,),
        in_specs=[yspec, yspec, yspec, yspec,
                  _layer((d, d), l),
                  pl.BlockSpec((tm, d), lambda i: (i, 0)),
                  _layer((1, d), l_next)],
        out_specs=[pl.BlockSpec((tm, d), lambda i: (i, 0)), pl.BlockSpec((tm, d), lambda i: (i, 0))],
        out_shape=[jax.ShapeDtypeStruct((t, d), F32), jax.ShapeDtypeStruct((t, d), BF16)],
        compiler_params=_params("arbitrary"),
        name="out_proj",
    )(*ys, w, h, nw)


def _ssd_kernel(p_ref, cw_ref, cb_ref, dtb_ref, alog_ref, d_ref, nw_ref, o_ref, cbuf, st_ref):
    L = SSD_CHUNK

    @pl.when(pl.program_id(1) == 0)
    def _():
        cbuf[0:8, :] = jnp.zeros((8, SSD_XBC), F32)
        st_ref[...] = jnp.zeros_like(st_ref)

    z = p_ref[:, 0:512]
    cbuf[8:8 + L, :] = p_ref[:, 512:512 + SSD_XBC]
    acc = cb_ref[...] + cbuf[pl.ds(8 - (SSD_CONV - 1), L), :] * cw_ref[0:1, :]
    for k in range(1, SSD_CONV):
        acc = acc + cbuf[pl.ds(8 - (SSD_CONV - 1) + k, L), :] * cw_ref[k:k + 1, :]
    cbuf[0:8, :] = cbuf[L:L + 8, :]
    xbc = _silu(acc)
    xs = xbc[:, 0:512]
    bm = xbc[:, 512:640]
    cm = xbc[:, 640:768]

    dt = _softplus(p_ref[:, 1280:1792] + dtb_ref[...])
    a = -jnp.exp(alog_ref[...])
    adt = dt * a
    row = lax.broadcasted_iota(jnp.int32, (L, L), 0)
    col = lax.broadcasted_iota(jnp.int32, (L, L), 1)
    causal = col <= row
    tri = causal.astype(F32)
    cs = jnp.dot(tri, adt, preferred_element_type=F32, precision=HI)
    cs_t = jnp.transpose(cs)
    xdt = (xs * dt).astype(BF16)

    ys = []
    for g in range(2):
        bg = bm[:, 64 * g:64 * g + 64]
        cg = cm[:, 64 * g:64 * g + 64]
        gmat = _dot_nt(cg.astype(BF16), bg.astype(BF16))
        for hh in range(4):
            h = 4 * g + hh
            lo = 64 * h
            cs_col = cs[:, lo:lo + 1]
            cs_row = cs_t[lo:lo + 1, :]
            lmat = jnp.exp(jnp.where(causal, cs_col - cs_row, -jnp.inf))
            xh = xdt[:, lo:lo + 64]
            y = _dot((gmat * lmat).astype(BF16), xh)
            st = st_ref[h]
            y = y + _dot((cg * jnp.exp(cs_col)).astype(BF16), st.astype(BF16))
            cs_last = cs[L - 1:L, lo:lo + 1]
            bdec = bg * jnp.exp(cs_last - cs_col)
            st_ref[h] = jnp.exp(cs_last) * st + _dot_tn(bdec.astype(BF16), xh)
            ys.append(y)
    y = jnp.concatenate(ys, axis=1) + d_ref[...] * xs
    y = y * _silu(z)
    ms = jnp.mean(y * y, axis=-1, keepdims=True)
    o_ref[...] = (y * lax.rsqrt(ms + NORM_EPS) * nw_ref[...]).astype(BF16)


def _ssd_branch(l, p, cw, cb, dtb, alog, dsk, nw, bsz, s):
    nc = s // SSD_CHUNK
    return pl.pallas_call(
        _ssd_kernel,
        grid=(bsz, nc),
        in_specs=[pl.BlockSpec((SSD_CHUNK, W_SSD), lambda b, c: (b * nc + c, 0)),
                  _layer((SSD_CONV, SSD_XBC), l), _layer((1, SSD_XBC), l),
                  _layer((1, 512), l), _layer((1, 512), l), _layer((1, 512), l), _layer((1, 512), l)],
        out_specs=pl.BlockSpec((SSD_CHUNK, GROUP_W), lambda b, c: (b * nc + c, 0)),
        out_shape=jax.ShapeDtypeStruct((bsz * s, GROUP_W), BF16),
        scratch_shapes=[pltpu.VMEM((SSD_CHUNK + 8, SSD_XBC), F32),
                        pltpu.VMEM((SSD_HEADS, SSD_STATE, SSD_HEAD_DIM), F32)],
        compiler_params=_params("arbitrary", "arbitrary"),
        name="ssd_branch",
    )(p, cw, cb, dtb, alog, dsk, nw)


def _s5_prep_kernel(are_ref, aim_ref, ldt_ref, bre_ref, bim_ref, cre_ref, cim_ref, bb_ref, cc_ref, tab_ref):
    are = are_ref[...]
    aim = aim_ref[...]
    delta = jnp.exp(ldt_ref[...])
    mag = jnp.exp(are * delta)
    ar = mag * jnp.cos(aim * delta)
    ai = mag * jnp.sin(aim * delta)
    den = are * are + aim * aim
    coef_re = ((ar - 1.0) * are + ai * aim) / den
    coef_im = (ai * are - (ar - 1.0) * aim) / den

    bb_ref[...] = jnp.zeros_like(bb_ref)
    cc_ref[...] = jnp.zeros_like(cc_ref)
    for g in range(S5_GROUPS):
        kb, gl = divmod(g, 16)
        st = slice(S5_STATE * g, S5_STATE * (g + 1))
        rows = slice(S5_CH * gl, S5_CH * (gl + 1))
        re = slice(S5_STATE * gl, S5_STATE * (gl + 1))
        im = slice(S5_HALF + S5_STATE * gl, S5_HALF + S5_STATE * (gl + 1))
        bre = bre_ref[g]
        bim = bim_ref[g]
        bb_ref[kb, rows, re] = (coef_re[:, st] * bre - coef_im[:, st] * bim).astype(BF16)
        bb_ref[kb, rows, im] = (coef_re[:, st] * bim + coef_im[:, st] * bre).astype(BF16)
        cc_ref[kb, re, rows] = cre_ref[g].astype(BF16)
        cc_ref[kb, im, rows] = (-cim_ref[g]).astype(BF16)

    odd = lax.broadcasted_iota(jnp.int32, (8, S5_HALF), 0) % 2 == 1
    tab_ref[0] = jnp.where(odd, ar[:, S5_HALF:], ar[:, :S5_HALF])
    tab_ref[1] = jnp.where(odd, ai[:, S5_HALF:], ai[:, :S5_HALF])


def _s5_prep(l, are, aim, ldt, bre, bim, cre, cim):
    out3 = lambda shape: pl.BlockSpec(shape, lambda i: (0, 0, 0))
    return pl.pallas_call(
        _s5_prep_kernel,
        grid=(1,),
        in_specs=[_layer((1, S5_NS), l), _layer((1, S5_NS), l), _layer((1, S5_NS), l),
                  _layer((S5_GROUPS, S5_CH, S5_STATE), l), _layer((S5_GROUPS, S5_CH, S5_STATE), l),
                  _layer((S5_GROUPS, S5_STATE, S5_CH), l), _layer((S5_GROUPS, S5_STATE, S5_CH), l)],
        out_specs=[out3((2, 256, 2 * S5_HALF)), out3((2, 2 * S5_HALF, 256)), out3((2, 8, S5_HALF))],
        out_shape=[jax.ShapeDtypeStruct((2, 256, 2 * S5_HALF), BF16),
                   jax.ShapeDtypeStruct((2, 2 * S5_HALF, 256), BF16),
                   jax.ShapeDtypeStruct((2, 8, S5_HALF), F32)],
        compiler_params=_params("arbitrary"),
        name="s5_prep",
    )(are, aim, ldt, bre, bim, cre, cim)


def _s5_kernel(p_ref, bb_ref, cc_ref, tab_ref, d_ref, gw_ref, gb_ref, o_ref, pbuf, hbuf, ybuf, carry):
    nb = p_ref.shape[0]
    rt = S5_TSTEP
    n = nb * rt

    @pl.when(pl.program_id(0) == 0)
    def _():
        carry[...] = jnp.zeros_like(carry)

    def cat(ref, blocks, rows):
        return jnp.concatenate([ref[c, rows, :] for c in blocks], axis=1)

    for b in range(nb):
        for c in range(W_S5 // 128):
            pbuf[c, pl.ds(b, rt, stride=nb), :] = p_ref[b, :, 128 * c:128 * (c + 1)]
    u = cat(pbuf, range(0, 4), slice(None))
    ub = u.astype(BF16)
    nblk = 2 * S5_HALF // 128
    for j in range(2):
        hj = _dot(ub[:, 256 * j:256 * (j + 1)], bb_ref[j])
        for c in range(nblk):
            hbuf[c, pl.ds(j, n, stride=2), :] = hj[:, 128 * c:128 * (c + 1)]

    wb = S5_SCAN_LANES // 128
    for lc in range(S5_HALF // S5_SCAN_LANES):
        re_blocks = list(range(wb * lc, wb * (lc + 1)))
        im_blocks = [c + nblk // 2 for c in re_blocks]
        lanes = slice(S5_SCAN_LANES * lc, S5_SCAN_LANES * (lc + 1))
        ar = tab_ref[0, :, lanes]
        ai = tab_ref[1, :, lanes]

        def body(t, c, re_blocks=re_blocks, im_blocks=im_blocks, ar=ar, ai=ai):
            hr, hi = c
            rows = pl.ds(pl.multiple_of(t * 8, 8), 8)
            hr, hi = (ar * hr - ai * hi + cat(hbuf, re_blocks, rows),
                      ar * hi + ai * hr + cat(hbuf, im_blocks, rows))
            for k in range(wb):
                hbuf[re_blocks[k], rows, :] = hr[:, 128 * k:128 * (k + 1)]
                hbuf[im_blocks[k], rows, :] = hi[:, 128 * k:128 * (k + 1)]
            return hr, hi

        re = slice(S5_SCAN_LANES * lc, S5_SCAN_LANES * (lc + 1))
        im = slice(S5_HALF + re.start, S5_HALF + re.stop)
        hr, hi = lax.fori_loop(0, rt, body, (carry[:, re], carry[:, im]), unroll=4)
        carry[:, re] = hr
        carry[:, im] = hi

    ys = [_dot(cat(hbuf, range(nblk), pl.ds(j, n, stride=2)).astype(BF16), cc_ref[j]) for j in range(2)]
    y = jnp.concatenate(ys, axis=1) + d_ref[...] * u
    y = 0.5 * y * (1.0 + jnp.tanh(math.sqrt(2.0 / math.pi) * (y + 0.044715 * (y * y * y))))
    glu = _dot(y.astype(BF16), gw_ref[...]) + gb_ref[...]
    y = y / (1.0 + jnp.exp(-glu))
    y = y * _silu(cat(pbuf, range(4, 8), slice(None)))
    for c in range(GROUP_W // 128):
        ybuf[c] = y[:, 128 * c:128 * (c + 1)]
    for b in range(nb):
        o_ref[b] = cat(ybuf, range(GROUP_W // 128), pl.ds(b, rt, stride=nb)).astype(BF16)


def _s5_branch(l, p, bb, cc, tab, dsk, gw, gb, bsz, s):
    assert 2 * bsz == 8, "scan rows (batch, state block) must fill the 8 sublanes"
    const = lambda shape: pl.BlockSpec(shape, lambda t: tuple(0 for _ in shape))
    n = bsz * S5_TSTEP
    out = pl.pallas_call(
        _s5_kernel,
        grid=(s // S5_TSTEP,),
        in_specs=[pl.BlockSpec((bsz, S5_TSTEP, W_S5), lambda t: (0, t, 0)),
                  const((2, 256, 2 * S5_HALF)), const((2, 2 * S5_HALF, 256)), const((2, 8, S5_HALF)),
                  _layer((1, GROUP_W), l), _layer((GROUP_W, GROUP_W), l), _layer((1, GROUP_W), l)],
        out_specs=pl.BlockSpec((bsz, S5_TSTEP, GROUP_W), lambda t: (0, t, 0)),
        out_shape=jax.ShapeDtypeStruct((bsz, s, GROUP_W), BF16),
        scratch_shapes=[pltpu.VMEM((W_S5 // 128, n, 128), F32),
                        pltpu.VMEM((2 * S5_HALF // 128, 2 * n, 128), F32),
                        pltpu.VMEM((GROUP_W // 128, n, 128), F32),
                        pltpu.VMEM((8, 2 * S5_HALF), F32)],
        compiler_params=_params("arbitrary"),
        name="s5_branch",
    )(p.reshape(bsz, s, W_S5), bb, cc, tab, dsk, gw, gb)
    return out.reshape(bsz * s, GROUP_W)


def _gla_kernel(p_ref, w2_ref, gb_ref, nw_ref, o_ref, st_ref):
    T = GLA_TILE
    NCH = T // GLA_CHUNK

    @pl.when(pl.program_id(1) == 0)
    def _():
        st_ref[...] = jnp.zeros_like(st_ref)

    x = _dot(p_ref[:, 1536:1664].astype(BF16), w2_ref[...]) + gb_ref[...]
    g = -_softplus(-x) * (1.0 / GLA_GATE_NORM)
    row = lax.broadcasted_iota(jnp.int32, (T, T), 0)
    col = lax.broadcasted_iota(jnp.int32, (T, T), 1)
    same = (row // GLA_CHUNK) == (col // GLA_CHUNK)
    intra = jnp.logical_and(same, col <= row)
    b = jnp.dot(intra.astype(F32), g, preferred_element_type=F32, precision=HI)
    ones_blk = same.astype(F32)
    blast = jnp.dot(ones_blk, g, preferred_element_type=F32, precision=HI)
    blast_t = lax.dot_general(g, ones_blk, (((0,), (0,)), ((), ())),
                              preferred_element_type=F32, precision=HI)
    scale = GLA_DK ** -0.5
    q = p_ref[:, 0:256] * scale
    k = p_ref[:, 256:512]
    half = 0.5 * blast
    q_mid = (q * jnp.exp(b - half)).astype(BF16)
    k_mid = (k * jnp.exp(half - b)).astype(BF16)
    q_dec = (q * jnp.exp(b)).astype(BF16)
    k_dec = (k * jnp.exp(blast - b)).astype(BF16)
    rsel = lax.broadcasted_iota(jnp.int32, (T, NCH * GLA_DK), 0) // GLA_CHUNK
    csel = lax.broadcasted_iota(jnp.int32, (T, NCH * GLA_DK), 1) // GLA_DK
    blockmask = rsel == csel

    for h in range(GLA_HEADS):
        lo = GLA_DK * h
        v = p_ref[:, 512 + GLA_DV * h:512 + GLA_DV * (h + 1)].astype(BF16)
        attn = jnp.where(intra, _dot_nt(q_mid[:, lo:lo + GLA_DK], k_mid[:, lo:lo + GLA_DK]), 0.0)
        o = _dot(attn.astype(BF16), v)
        zero = jnp.zeros((), BF16)
        qcat = jnp.where(blockmask, jnp.tile(q_dec[:, lo:lo + GLA_DK], (1, NCH)), zero)
        kcat = jnp.where(blockmask, jnp.tile(k_dec[:, lo:lo + GLA_DK], (1, NCH)), zero)
        ds = _dot_tn(kcat, v)
        st = st_ref[h]
        sts = []
        for c in range(NCH):
            sts.append(st)
            dec = jnp.exp(blast_t[lo:lo + GLA_DK, GLA_CHUNK * c:GLA_CHUNK * c + 1])
            st = dec * st + ds[GLA_DK * c:GLA_DK * (c + 1), :]
        st_ref[h] = st
        o = o + _dot(qcat, jnp.concatenate(sts, axis=0).astype(BF16))
        ms = jnp.mean(o * o, axis=-1, keepdims=True)
        o = o * lax.rsqrt(ms + NORM_EPS) * nw_ref[...]
        gate = p_ref[:, 1024 + GLA_DV * h:1024 + GLA_DV * (h + 1)]
        o_ref[:, GLA_DV * h:GLA_DV * (h + 1)] = (o * _silu(gate)).astype(BF16)


def _gla_branch(l, p, w2, gb, nw, bsz, s):
    nt = s // GLA_TILE
    return pl.pallas_call(
        _gla_kernel,
        grid=(bsz, nt),
        in_specs=[pl.BlockSpec((GLA_TILE, W_GLA), lambda b, t: (b * nt + t, 0)),
                  _layer((128, 256), l), _layer((1, 256), l), _layer((1, GLA_DV), l)],
        out_specs=pl.BlockSpec((GLA_TILE, GROUP_W), lambda b, t: (b * nt + t, 0)),
        out_shape=jax.ShapeDtypeStruct((bsz * s, GROUP_W), BF16),
        scratch_shapes=[pltpu.VMEM((GLA_HEADS, GLA_DK, GLA_DV), F32)],
        compiler_params=_params("arbitrary", "arbitrary"),
        name="gla_branch",
    )(p, w2, gb, nw)


def _rope_swap(t):
    lane = lax.broadcasted_iota(jnp.int32, t.shape, 1)
    return jnp.where(lane < MLA_ROPE // 2, pltpu.roll(t, 128 - MLA_ROPE // 2, 1), pltpu.roll(t, MLA_ROPE // 2, 1))


def _mla_prep_kernel(p_ref, pos_ref, invf_ref, sign_ref, qnw_ref, wuq_ref, kvnw_ref, wukv_ref,
                     qhw_ref, khw_ref, q_ref, k_ref, v_ref):
    cq = p_ref[:, 0:384]
    ckv = p_ref[:, 384:512]
    kpe = p_ref[:, 512:640]
    ms = jnp.mean(cq * cq, axis=-1, keepdims=True)
    qn = (cq * lax.rsqrt(ms + NORM_EPS) * qnw_ref[...]).astype(BF16)
    ms = jnp.mean(ckv * ckv, axis=-1, keepdims=True)
    kvn = (ckv * lax.rsqrt(ms + NORM_EPS) * kvnw_ref[...]).astype(BF16)
    q = _dot(qn, wuq_ref[...])
    kv = _dot(kvn, wukv_ref[...])
    ang = pos_ref[...].astype(F32) * invf_ref[...]
    cos_t = jnp.cos(ang)
    sin_t = jnp.sin(ang) * sign_ref[...]
    inv_d = 1.0 / (MLA_NOPE + MLA_ROPE)
    scale = (MLA_NOPE + MLA_ROPE) ** -0.5
    kpe_ss = jnp.sum(kpe * kpe, axis=-1, keepdims=True)
    qhw = qhw_ref[...]
    khw = khw_ref[...]
    for h in range(MLA_HEADS):
        lo = MLA_HEAD_PAD * h
        qh = q[:, lo:lo + MLA_HEAD_PAD]
        r = lax.rsqrt(jnp.sum(qh * qh, axis=-1, keepdims=True) * inv_d + NORM_EPS) * scale
        qh = qh * r * qhw
        t = qh[:, 128:256]
        q_ref[:, lo:lo + 128] = qh[:, 0:128].astype(BF16)
        q_ref[:, lo + 128:lo + 256] = (t * cos_t + _rope_swap(t) * sin_t).astype(BF16)
        kn = kv[:, lo:lo + 128]
        r = lax.rsqrt((jnp.sum(kn * kn, axis=-1, keepdims=True) + kpe_ss) * inv_d + NORM_EPS)
        k_ref[:, lo:lo + 128] = (kn * r * khw[:, 0:128]).astype(BF16)
        t = kpe * r * khw[:, 128:256]
        k_ref[:, lo + 128:lo + 256] = (t * cos_t + _rope_swap(t) * sin_t).astype(BF16)
        v_ref[:, MLA_V * h:MLA_V * (h + 1)] = kv[:, lo + 128:lo + 256].astype(BF16)


def _mla_prep(l, p, pos, invf, sign, qnw, wuq, kvnw, wukv, qhw, khw, tm=256):
    t = p.shape[0]
    const = lambda shape: pl.BlockSpec(shape, lambda i: tuple(0 for _ in shape))
    hp = MLA_HEADS * MLA_HEAD_PAD
    return pl.pallas_call(
        _mla_prep_kernel,
        grid=(t // tm,),
        in_specs=[pl.BlockSpec((tm, W_MLA), lambda i: (i, 0)),
                  pl.BlockSpec((tm, 1), lambda i: (i, 0)),
                  const((1, 128)), const((1, 128)),
                  _layer((1, MLA_Q_RANK), l), _layer((MLA_Q_RANK, hp), l),
                  _layer((1, MLA_KV_RANK), l), _layer((MLA_KV_RANK, hp), l),
                  _layer((1, MLA_HEAD_PAD), l), _layer((1, MLA_HEAD_PAD), l)],
        out_specs=[pl.BlockSpec((tm, hp), lambda i: (i, 0)),
                   pl.BlockSpec((tm, hp), lambda i: (i, 0)),
                   pl.BlockSpec((tm, MLA_HEADS * MLA_V), lambda i: (i, 0))],
        out_shape=[jax.ShapeDtypeStruct((t, hp), BF16),
                   jax.ShapeDtypeStruct((t, hp), BF16),
                   jax.ShapeDtypeStruct((t, MLA_HEADS * MLA_V), BF16)],
        compiler_params=_params("arbitrary"),
        name="mla_prep",
    )(p, pos, invf, sign, qnw, wuq, kvnw, wukv, qhw, khw)


def _flash_kernel(q_ref, k_ref, v_ref, p_ref, o_ref, s_ref):
    tq = FLASH_TQ
    hp = MLA_HEAD_PAD
    row = lax.broadcasted_iota(jnp.int32, (tq, tq), 0)
    col = lax.broadcasted_iota(jnp.int32, (tq, tq), 1)
    heads = range(MLA_HEADS)

    def attend(nb):
        q = [q_ref[:, hp * h:hp * (h + 1)] for h in heads]
        m = [None] * MLA_HEADS
        for j in range(nb):
            for h in heads:
                s = _dot_nt(q[h], k_ref[tq * j:tq * (j + 1), hp * h:hp * (h + 1)])
                if j == nb - 1:
                    s = jnp.where(col <= row, s, -jnp.inf)
                s_ref[h, :, tq * j:tq * (j + 1)] = s
                sm = jnp.maximum(s[:, 0:128], s[:, 128:256])
                m[h] = sm if m[h] is None else jnp.maximum(m[h], sm)
        m = [jnp.max(mh, axis=-1, keepdims=True) for mh in m]
        l = [jnp.zeros((tq, 128), F32) for _ in heads]
        acc = [jnp.zeros((tq, MLA_V), F32) for _ in heads]
        for j in range(nb):
            for h in heads:
                p = jnp.exp(s_ref[h, :, tq * j:tq * (j + 1)] - m[h])
                l[h] = l[h] + (p[:, 0:128] + p[:, 128:256])
                acc[h] = acc[h] + _dot(p.astype(BF16), v_ref[tq * j:tq * (j + 1), MLA_V * h:MLA_V * (h + 1)])
        for h in heads:
            lh = jnp.sum(l[h], axis=-1, keepdims=True)
            gate = p_ref[:, W_MLA - GROUP_W + MLA_V * h:W_MLA - GROUP_W + MLA_V * (h + 1)]
            o_ref[:, MLA_V * h:MLA_V * (h + 1)] = (acc[h] / lh * _silu(gate)).astype(BF16)

    for nb in range(1, s_ref.shape[2] // tq + 1):
        pl.when(pl.program_id(1) == nb - 1)(functools.partial(attend, nb))


def _flash(q, k, v, p, bsz, s):
    nq = s // FLASH_TQ
    hp = MLA_HEADS * MLA_HEAD_PAD
    return pl.pallas_call(
        _flash_kernel,
        grid=(bsz, nq),
        in_specs=[pl.BlockSpec((FLASH_TQ, hp), lambda b, i: (b * nq + i, 0)),
                  pl.BlockSpec((s, hp), lambda b, i: (b, 0)),
                  pl.BlockSpec((s, MLA_HEADS * MLA_V), lambda b, i: (b, 0)),
                  pl.BlockSpec((FLASH_TQ, W_MLA), lambda b, i: (b * nq + i, 0))],
        out_specs=pl.BlockSpec((FLASH_TQ, MLA_HEADS * MLA_V), lambda b, i: (b * nq + i, 0)),
        out_shape=jax.ShapeDtypeStruct((bsz * s, MLA_HEADS * MLA_V), BF16),
        scratch_shapes=[pltpu.VMEM((MLA_HEADS, FLASH_TQ, s), F32)],
        compiler_params=_params("arbitrary", "arbitrary"),
        name="mla_flash",
    )(q, k, v, p)


def _regroup_kernel(w_ref, ssd_ref, s5_ref, gla_ref, mla_ref):
    offs = [0]
    for wd in IN_WIDTHS:
        offs.append(offs[-1] + wd)
    (z, xbc, dt, s5u, s5g, gq, gk, gv, gg, glr, cq, ckv, kpe, mg) = [
        w_ref[:, a:b] for a, b in zip(offs[:-1], offs[1:])]
    rows = z.shape[0]
    ssd_ref[:, 0:512] = z.astype(BF16)
    ssd_ref[:, 512:1280] = xbc.astype(BF16)
    for h in range(SSD_HEADS):
        ssd_ref[:, 1280 + 64 * h:1280 + 64 * (h + 1)] = jnp.broadcast_to(dt[:, h:h + 1], (rows, 64)).astype(BF16)
    s5_ref[:, 0:512] = s5u.astype(BF16)
    s5_ref[:, 512:1024] = s5g.astype(BF16)
    gla_ref[:, 0:256] = gq.astype(BF16)
    gla_ref[:, 256:512] = gk.astype(BF16)
    gla_ref[:, 512:1024] = gv.astype(BF16)
    gla_ref[:, 1024:1536] = gg.astype(BF16)
    gla_ref[:, 1536:1664] = jnp.zeros((rows, 128), BF16)
    gla_ref[:, 1536:1552] = glr.astype(BF16)
    mla_ref[:, 0:384] = cq.astype(BF16)
    mla_ref[:, 384:512] = ckv.astype(BF16)
    mla_ref[:, 512:640] = jnp.zeros((rows, 128), BF16)
    mla_ref[:, 512:576] = kpe.astype(BF16)
    mla_ref[:, 640:1152] = mg.astype(BF16)


def _regroup_w_in(w_in, tr=256):
    depth, k, n = w_in.shape
    widths = (W_SSD, W_S5, W_GLA, W_MLA)
    return pl.pallas_call(
        _regroup_kernel,
        grid=(depth, k // tr),
        in_specs=[pl.BlockSpec((None, tr, n), lambda l, i: (l, i, 0))],
        out_specs=[pl.BlockSpec((None, tr, w), lambda l, i: (l, i, 0)) for w in widths],
        out_shape=[jax.ShapeDtypeStruct((depth, k, w), BF16) for w in widths],
        compiler_params=_params("arbitrary", "arbitrary"),
        name="regroup_w_in",
    )(w_in)


def _pad_last(w, n):
    return jnp.pad(w, [(0, 0)] * (w.ndim - 1) + [(0, n - w.shape[-1])])


def kernel(x, positions, norm_w, w_in, w_out, ssd_conv_w, ssd_conv_b, ssd_dt_bias, ssd_a_log, ssd_d, ssd_norm_w, s5_a_re, s5_a_im, s5_log_dt, s5_b_re, s5_b_im, s5_c_re, s5_c_im, s5_d, s5_glu_w, s5_glu_b, gla_gate_w2, gla_gate_b, gla_norm_w, mla_q_norm_w, mla_w_uq, mla_kv_norm_w, mla_w_ukv, mla_q_head_norm_w, mla_k_head_norm_w):
    bsz, s, d = x.shape
    depth = w_in.shape[0]
    t = bsz * s
    h = x.reshape(t, d)
    pos = positions.reshape(t, 1)
    row = lambda v: v[:, None, :]

    inv_freq = ROPE_THETA ** (-jnp.arange(0, MLA_ROPE, 2, dtype=F32) / MLA_ROPE)
    invf = jnp.concatenate([inv_freq, inv_freq, jnp.zeros((64,), F32)])[None, :]
    sign = jnp.concatenate([-jnp.ones((32,), F32), jnp.ones((32,), F32), jnp.zeros((64,), F32)])[None, :]

    w_ssd, w_s5, w_gla, w_mla = _regroup_w_in(w_in)
    w_out_b = w_out.astype(BF16)
    norm_w3 = row(norm_w)

    rep = lambda v: row(jnp.repeat(v, SSD_HEAD_DIM, axis=1))
    ssd_cb, ssd_dtb, ssd_alog, ssd_dsk, ssd_nw = (row(ssd_conv_b), rep(ssd_dt_bias), rep(ssd_a_log),
                                                   rep(ssd_d), row(ssd_norm_w))

    s5_are = s5_a_re.reshape(depth, 1, S5_NS)
    s5_aim = s5_a_im.reshape(depth, 1, S5_NS)
    s5_ldt = row(jnp.repeat(s5_log_dt, S5_STATE, axis=1))
    s5_bre, s5_bim = s5_b_re.transpose(0, 1, 3, 2), s5_b_im.transpose(0, 1, 3, 2)
    s5_cre, s5_cim = s5_c_re.transpose(0, 1, 3, 2), s5_c_im.transpose(0, 1, 3, 2)
    s5_dsk, s5_gw, s5_gb = row(s5_d), s5_glu_w.astype(BF16), row(s5_glu_b)

    gla_w2 = jnp.pad(gla_gate_w2, ((0, 0), (0, 112), (0, 0))).astype(BF16)
    gla_gb, gla_nw = row(gla_gate_b), row(gla_norm_w)

    wuq = _pad_last(mla_w_uq.reshape(depth, MLA_Q_RANK, MLA_HEADS, MLA_NOPE + MLA_ROPE), MLA_HEAD_PAD)
    wuq = wuq.reshape(depth, MLA_Q_RANK, MLA_HEADS * MLA_HEAD_PAD).astype(BF16)
    wukv = mla_w_ukv.astype(BF16)
    qnw, kvnw = row(mla_q_norm_w), row(mla_kv_norm_w)
    qhw, khw = row(_pad_last(mla_q_head_norm_w, MLA_HEAD_PAD)), row(_pad_last(mla_k_head_norm_w, MLA_HEAD_PAD))

    u = _rmsnorm_bf16(h, norm_w3, 0)
    for l in range(depth):
        p_ssd = _in_proj(u, w_ssd, l, "in_proj_ssd")
        p_s5 = _in_proj(u, w_s5, l, "in_proj_s5")
        p_gla = _in_proj(u, w_gla, l, "in_proj_gla")
        p_mla = _in_proj(u, w_mla, l, "in_proj_mla")

        y_a = _ssd_branch(l, p_ssd, ssd_conv_w, ssd_cb, ssd_dtb, ssd_alog, ssd_dsk, ssd_nw, bsz, s)

        bb, cc, tab = _s5_prep(l, s5_are, s5_aim, s5_ldt, s5_bre, s5_bim, s5_cre, s5_cim)
        y_b = _s5_branch(l, p_s5, bb, cc, tab, s5_dsk, s5_gw, s5_gb, bsz, s)

        y_c = _gla_branch(l, p_gla, gla_w2, gla_gb, gla_nw, bsz, s)

        q, k, v = _mla_prep(l, p_mla, pos, invf, sign, qnw, wuq, kvnw, wukv, qhw, khw)
        y_d = _flash(q, k, v, p_mla, bsz, s)

        h, u = _out_proj((y_a, y_b, y_c, y_d), w_out_b, h, norm_w3, l, (l + 1) % depth)
    return h.reshape(bsz, s, d)
```

```python
import functools
import math

import jax
import jax.numpy as jnp
from jax import lax
from jax.experimental import pallas as pl
from jax.experimental.pallas import tpu as pltpu

F32 = jnp.float32
BF16 = jnp.bfloat16
NORM_EPS = 1e-6
HI = lax.Precision.HIGHEST

D_MODEL = 2048
GROUP_W = 512
SSD_HEADS = 8
SSD_HEAD_DIM = 64
SSD_STATE = 64
SSD_CHUNK = 128
SSD_XBC = 768
SSD_CONV = 4
S5_GROUPS = 32
S5_CH = 16
S5_STATE = 64
S5_NS = S5_GROUPS * S5_STATE
S5_HALF = S5_NS // 2
S5_TSTEP = 64
S5_SCAN_LANES = 512
GLA_HEADS = 4
GLA_DK = 64
GLA_DV = 128
GLA_CHUNK = 16
GLA_TILE = 128
GLA_TILES_PER_STEP = 4
GLA_GATE_NORM = 16.0
MLA_HEADS = 4
MLA_NOPE = 128
MLA_ROPE = 64
MLA_V = 128
MLA_Q_RANK = 384
MLA_KV_RANK = 128
MLA_HEAD_PAD = 256
ROPE_THETA = 10000.0
FLASH_TQ = 256

IN_WIDTHS = (512, 768, 8, 512, 512, 256, 256, 512, 512, 16, 384, 128, 64, 512)
W_SSD = 512 + 768 + 512
W_S5 = 1024
W_GLA = 256 + 256 + 512 + 512 + 128
W_MLA = 384 + 128 + 128 + 512

VMEM_LIMIT_BYTES = 56 * 1024 * 1024


def _params(*sem):
    return pltpu.CompilerParams(dimension_semantics=sem, vmem_limit_bytes=VMEM_LIMIT_BYTES)


def _layer(shape, l):
    return pl.BlockSpec((None,) + tuple(shape), lambda *_: (l,) + (0,) * len(shape))


def _silu(x):
    return x / (1.0 + jnp.exp(-x))


def _softplus(x):
    return jnp.maximum(x, 0.0) + jnp.log(1.0 + jnp.exp(-jnp.abs(x)))


def _dot(a, b):
    return jnp.dot(a, b, preferred_element_type=F32)


def _dot_nt(a, b):
    return lax.dot_general(a, b, (((1,), (1,)), ((), ())), preferred_element_type=F32)


def _dot_tn(a, b):
    return lax.dot_general(a, b, (((0,), (0,)), ((), ())), preferred_element_type=F32)


def _dot_01(mask, x):
    m = mask.astype(BF16)
    hi = x.astype(BF16)
    rest = x - hi.astype(F32)
    mid = rest.astype(BF16)
    lo = (rest - mid.astype(F32)).astype(BF16)
    return _dot(m, hi) + _dot(m, mid) + _dot(m, lo)


def _rms_kernel(x_ref, w_ref, o_ref):
    x = x_ref[...]
    ms = jnp.mean(x * x, axis=-1, keepdims=True)
    o_ref[...] = (x * lax.rsqrt(ms + NORM_EPS) * w_ref[...]).astype(BF16)


def _rmsnorm_bf16(x, w, l, tm=512):
    t, d = x.shape
    return pl.pallas_call(
        _rms_kernel,
        grid=(t // tm,),
        in_specs=[pl.BlockSpec((tm, d), lambda i: (i, 0)), _layer((1, d), l)],
        out_specs=pl.BlockSpec((tm, d), lambda i: (i, 0)),
        out_shape=jax.ShapeDtypeStruct((t, d), BF16),
        compiler_params=_params("arbitrary"),
        name="rmsnorm_in",
    )(x, w)


def _mm_kernel(x_ref, wt_ref, o_ref):
    o_ref[...] = _dot_nt(x_ref[...], wt_ref[...])


def _in_proj(u, wt, l, name, tm=512):
    t, k = u.shape
    n = wt.shape[1]
    return pl.pallas_call(
        _mm_kernel,
        grid=(t // tm,),
        in_specs=[pl.BlockSpec((tm, k), lambda i: (i, 0)), _layer((n, k), l)],
        out_specs=pl.BlockSpec((tm, n), lambda i: (i, 0)),
        out_shape=jax.ShapeDtypeStruct((t, n), F32),
        compiler_params=_params("arbitrary"),
        name=name,
    )(u, wt)


def _out_proj_kernel(ya_ref, yb_ref, yc_ref, yd_ref, w_ref, h_ref, nw_ref, ho_ref, uo_ref):
    acc = h_ref[...]
    for i, y_ref in enumerate((ya_ref, yb_ref, yc_ref, yd_ref)):
        acc = acc + _dot(y_ref[...], w_ref[GROUP_W * i:GROUP_W * (i + 1), :])
    ho_ref[...] = acc
    ms = jnp.mean(acc * acc, axis=-1, keepdims=True)
    uo_ref[...] = (acc * lax.rsqrt(ms + NORM_EPS) * nw_ref[...]).astype(BF16)


def _out_proj(ys, w, h, nw, l, l_next, tm=256):
    t, d = h.shape
    yspec = pl.BlockSpec((tm, GROUP_W), lambda i: (i, 0))
    return pl.pallas_call(
        _out_proj_kernel,
        grid=(t // tm,),
        in_specs=[yspec, yspec, yspec, yspec,
                  _layer((d, d), l),
                  pl.BlockSpec((tm, d), lambda i: (i, 0)),
                  _layer((1, d), l_next)],
        out_specs=[pl.BlockSpec((tm, d), lambda i: (i, 0)), pl.BlockSpec((tm, d), lambda i: (i, 0))],
        out_shape=[jax.ShapeDtypeStruct((t, d), F32), jax.ShapeDtypeStruct((t, d), BF16)],
        compiler_params=_params("arbitrary"),
        name="out_proj",
    )(*ys, w, h, nw)


def _ssd_kernel(p_ref, cw_ref, cb_ref, dtb_ref, alog_ref, d_ref, nw_ref, o_ref, cbuf, st_ref):
    L = SSD_CHUNK

    @pl.when(pl.program_id(1) == 0)
    def _():
        cbuf[0:8, :] = jnp.zeros((8, SSD_XBC), F32)
        st_ref[...] = jnp.zeros_like(st_ref)

    z = p_ref[:, 0:512]
    cbuf[8:8 + L, :] = p_ref[:, 512:512 + SSD_XBC]
    acc = cb_ref[...] + cbuf[pl.ds(8 - (SSD_CONV - 1), L), :] * cw_ref[0:1, :]
    for k in range(1, SSD_CONV):
        acc = acc + cbuf[pl.ds(8 - (SSD_CONV - 1) + k, L), :] * cw_ref[k:k + 1, :]
    cbuf[0:8, :] = cbuf[L:L + 8, :]
    xbc = _silu(acc)
    xs = xbc[:, 0:512]
    bm = xbc[:, 512:640]
    cm = xbc[:, 640:768]

    dt = _softplus(p_ref[:, 1280:1792] + dtb_ref[...])
    a = -jnp.exp(alog_ref[...])
    row = lax.broadcasted_iota(jnp.int32, (L, L), 0)
    col = lax.broadcasted_iota(jnp.int32, (L, L), 1)
    cs = _dot_01(col <= row, dt * a)
    cs_t = jnp.transpose(cs)
    cs_last = cs[L - 1:L, :]
    grow = jnp.exp(cs)
    tail = jnp.exp(cs_last - cs)
    total = jnp.exp(cs_last)
    xdt = (xs * dt).astype(BF16)
    bmb = bm.astype(BF16)

    lane = lax.broadcasted_iota(jnp.int32, (L, 128), 1)
    rowi = lax.broadcasted_iota(jnp.int32, (L, 128), 0)
    low = lane < 64
    low8 = lax.broadcasted_iota(jnp.int32, (8, 128), 1) < 64
    low64 = lax.broadcasted_iota(jnp.int32, (64, 128), 1) < 64
    keep = [rowi >= (lane % 64) + 64 * jh for jh in range(2)]
    blockdiag = (lax.broadcasted_iota(jnp.int32, (128, 128), 0) // 64
                 == lax.broadcasted_iota(jnp.int32, (128, 128), 1) // 64)

    def both_halves(x, g):
        r = pltpu.roll(x, 64, 1)
        return jnp.where(low, x, r) if g == 0 else jnp.where(low, r, x)

    ys = []
    for g in range(2):
        cmask = jnp.where(low if g == 0 else jnp.logical_not(low), cm, 0.0).astype(BF16)
        gdup = [_dot_nt(cmask, jnp.concatenate([bmb[64 * jh:64 * jh + 64, :]] * 2, axis=0)) for jh in range(2)]
        cdup = both_halves(cm, g)
        bdup = both_halves(bm, g)
        for pp in range(2):
            pair = 2 * g + pp
            lanes = slice(128 * pair, 128 * (pair + 1))
            csp = cs[:, lanes]
            xp = xdt[:, lanes]
            r0 = cs_t[128 * pair:128 * pair + 8, :]
            r1 = cs_t[128 * pair + 64:128 * pair + 72, :]
            crow = [jnp.where(low8, r0, pltpu.roll(r1, 64, 1))[0:1, :],
                    jnp.where(low8, pltpu.roll(r0, 64, 1), r1)[0:1, :]]
            y = None
            for jh in range(2):
                lmat = jnp.exp(jnp.where(keep[jh], csp - crow[jh], -jnp.inf))
                xj = xp[64 * jh:64 * jh + 64, :]
                zero = jnp.zeros_like(xj)
                xbd = jnp.concatenate([jnp.where(low64, xj, zero), jnp.where(low64, zero, xj)], axis=0)
                t = _dot((gdup[jh] * lmat).astype(BF16), xbd)
                y = t if y is None else y + t
            st = st_ref[pair]
            y = y + _dot_nt((cdup * grow[:, lanes]).astype(BF16), st.astype(BF16))
            upd = _dot_tn(xp, (bdup * tail[:, lanes]).astype(BF16))
            st_ref[pair] = total[:, lanes] * st + jnp.where(blockdiag, upd, 0.0)
            ys.append(y)
    y = jnp.concatenate(ys, axis=1) + d_ref[...] * xs
    y = y * _silu(z)
    ms = jnp.mean(y * y, axis=-1, keepdims=True)
    o_ref[...] = (y * lax.rsqrt(ms + NORM_EPS) * nw_ref[...]).astype(BF16)


def _ssd_branch(l, p, cw, cb, dtb, alog, dsk, nw, bsz, s):
    nc = s // SSD_CHUNK
    return pl.pallas_call(
        _ssd_kernel,
        grid=(bsz, nc),
        in_specs=[pl.BlockSpec((SSD_CHUNK, W_SSD), lambda b, c: (b * nc + c, 0)),
                  _layer((SSD_CONV, SSD_XBC), l), _layer((1, SSD_XBC), l),
                  _layer((1, 512), l), _layer((1, 512), l), _layer((1, 512), l), _layer((1, 512), l)],
        out_specs=pl.BlockSpec((SSD_CHUNK, GROUP_W), lambda b, c: (b * nc + c, 0)),
        out_shape=jax.ShapeDtypeStruct((bsz * s, GROUP_W), BF16),
        scratch_shapes=[pltpu.VMEM((SSD_CHUNK + 8, SSD_XBC), F32),
                        pltpu.VMEM((SSD_HEADS // 2, 2 * SSD_HEAD_DIM, 2 * SSD_STATE), F32)],
        compiler_params=_params("arbitrary", "arbitrary"),
        name="ssd_branch",
    )(p, cw, cb, dtb, alog, dsk, nw)


def _s5_prep_kernel(are_ref, aim_ref, ldt_ref, bre_ref, bim_ref, cre_ref, cim_ref, bb_ref, cc_ref, tab_ref):
    are = are_ref[...]
    aim = aim_ref[...]
    delta = jnp.exp(ldt_ref[...])
    mag = jnp.exp(are * delta)
    ar = mag * jnp.cos(aim * delta)
    ai = mag * jnp.sin(aim * delta)
    den = are * are + aim * aim
    coef_re = ((ar - 1.0) * are + ai * aim) / den
    coef_im = (ai * are - (ar - 1.0) * aim) / den

    bb_ref[...] = jnp.zeros_like(bb_ref)
    cc_ref[...] = jnp.zeros_like(cc_ref)
    for g in range(S5_GROUPS):
        kb, gl = divmod(g, 16)
        st = slice(S5_STATE * g, S5_STATE * (g + 1))
        rows = slice(S5_CH * gl, S5_CH * (gl + 1))
        re = slice(S5_STATE * gl, S5_STATE * (gl + 1))
        im = slice(S5_HALF + S5_STATE * gl, S5_HALF + S5_STATE * (gl + 1))
        bre = bre_ref[g]
        bim = bim_ref[g]
        bb_ref[kb, rows, re] = (coef_re[:, st] * bre - coef_im[:, st] * bim).astype(BF16)
        bb_ref[kb, rows, im] = (coef_re[:, st] * bim + coef_im[:, st] * bre).astype(BF16)
        cc_ref[kb, re, rows] = cre_ref[g].astype(BF16)
        cc_ref[kb, im, rows] = (-cim_ref[g]).astype(BF16)

    odd = lax.broadcasted_iota(jnp.int32, (8, S5_HALF), 0) % 2 == 1
    tab_ref[0] = jnp.where(odd, ar[:, S5_HALF:], ar[:, :S5_HALF])
    tab_ref[1] = jnp.where(odd, ai[:, S5_HALF:], ai[:, :S5_HALF])


def _s5_prep(l, are, aim, ldt, bre, bim, cre, cim):
    out3 = lambda shape: pl.BlockSpec(shape, lambda i: (0, 0, 0))
    return pl.pallas_call(
        _s5_prep_kernel,
        grid=(1,),
        in_specs=[_layer((1, S5_NS), l), _layer((1, S5_NS), l), _layer((1, S5_NS), l),
                  _layer((S5_GROUPS, S5_CH, S5_STATE), l), _layer((S5_GROUPS, S5_CH, S5_STATE), l),
                  _layer((S5_GROUPS, S5_STATE, S5_CH), l), _layer((S5_GROUPS, S5_STATE, S5_CH), l)],
        out_specs=[out3((2, 256, 2 * S5_HALF)), out3((2, 2 * S5_HALF, 256)), out3((2, 8, S5_HALF))],
        out_shape=[jax.ShapeDtypeStruct((2, 256, 2 * S5_HALF), BF16),
                   jax.ShapeDtypeStruct((2, 2 * S5_HALF, 256), BF16),
                   jax.ShapeDtypeStruct((2, 8, S5_HALF), F32)],
        compiler_params=_params("arbitrary"),
        name="s5_prep",
    )(are, aim, ldt, bre, bim, cre, cim)


def _s5_kernel(p_ref, bb_ref, cc_ref, tab_ref, d_ref, gw_ref, gb_ref, o_ref, pbuf, hbuf, ybuf, carry):
    nb = p_ref.shape[0]
    rt = S5_TSTEP
    n = nb * rt

    @pl.when(pl.program_id(0) == 0)
    def _():
        carry[...] = jnp.zeros_like(carry)

    def cat(ref, blocks, rows):
        return jnp.concatenate([ref[c, rows, :] for c in blocks], axis=1)

    for b in range(nb):
        for c in range(W_S5 // 128):
            pbuf[c, pl.ds(b, rt, stride=nb), :] = p_ref[b, :, 128 * c:128 * (c + 1)]
    u = cat(pbuf, range(0, 4), slice(None))
    ub = u.astype(BF16)
    nblk = 2 * S5_HALF // 128
    for j in range(2):
        hj = _dot(ub[:, 256 * j:256 * (j + 1)], bb_ref[j])
        for c in range(nblk):
            hbuf[c, pl.ds(j, n, stride=2), :] = hj[:, 128 * c:128 * (c + 1)]

    wb = S5_SCAN_LANES // 128
    for lc in range(S5_HALF // S5_SCAN_LANES):
        re_blocks = list(range(wb * lc, wb * (lc + 1)))
        im_blocks = [c + nblk // 2 for c in re_blocks]
        lanes = slice(S5_SCAN_LANES * lc, S5_SCAN_LANES * (lc + 1))
        ar = tab_ref[0, :, lanes]
        ai = tab_ref[1, :, lanes]

        def body(t, c, re_blocks=re_blocks, im_blocks=im_blocks, ar=ar, ai=ai):
            hr, hi = c
            rows = pl.ds(pl.multiple_of(t * 8, 8), 8)
            hr, hi = (ar * hr - ai * hi + cat(hbuf, re_blocks, rows),
                      ar * hi + ai * hr + cat(hbuf, im_blocks, rows))
            for k in range(wb):
                hbuf[re_blocks[k], rows, :] = hr[:, 128 * k:128 * (k + 1)]
                hbuf[im_blocks[k], rows, :] = hi[:, 128 * k:128 * (k + 1)]
            return hr, hi

        re = slice(S5_SCAN_LANES * lc, S5_SCAN_LANES * (lc + 1))
        im = slice(S5_HALF + re.start, S5_HALF + re.stop)
        hr, hi = lax.fori_loop(0, rt, body, (carry[:, re], carry[:, im]), unroll=4)
        carry[:, re] = hr
        carry[:, im] = hi

    ys = [_dot(cat(hbuf, range(nblk), pl.ds(j, n, stride=2)).astype(BF16), cc_ref[j]) for j in range(2)]
    y = jnp.concatenate(ys, axis=1) + d_ref[...] * u
    y = 0.5 * y * (1.0 + jnp.tanh(math.sqrt(2.0 / math.pi) * (y + 0.044715 * (y * y * y))))
    glu = _dot(y.astype(BF16), gw_ref[...]) + gb_ref[...]
    y = y / (1.0 + jnp.exp(-glu))
    y = y * _silu(cat(pbuf, range(4, 8), slice(None)))
    for c in range(GROUP_W // 128):
        ybuf[c] = y[:, 128 * c:128 * (c + 1)]
    for b in range(nb):
        o_ref[b] = cat(ybuf, range(GROUP_W // 128), pl.ds(b, rt, stride=nb)).astype(BF16)


def _s5_branch(l, p, bb, cc, tab, dsk, gw, gb, bsz, s):
    assert 2 * bsz == 8, "scan rows (batch, state block) must fill the 8 sublanes"
    const = lambda shape: pl.BlockSpec(shape, lambda t: tuple(0 for _ in shape))
    n = bsz * S5_TSTEP
    out = pl.pallas_call(
        _s5_kernel,
        grid=(s // S5_TSTEP,),
        in_specs=[pl.BlockSpec((bsz, S5_TSTEP, W_S5), lambda t: (0, t, 0)),
                  const((2, 256, 2 * S5_HALF)), const((2, 2 * S5_HALF, 256)), const((2, 8, S5_HALF)),
                  _layer((1, GROUP_W), l), _layer((GROUP_W, GROUP_W), l), _layer((1, GROUP_W), l)],
        out_specs=pl.BlockSpec((bsz, S5_TSTEP, GROUP_W), lambda t: (0, t, 0)),
        out_shape=jax.ShapeDtypeStruct((bsz, s, GROUP_W), BF16),
        scratch_shapes=[pltpu.VMEM((W_S5 // 128, n, 128), F32),
                        pltpu.VMEM((2 * S5_HALF // 128, 2 * n, 128), F32),
                        pltpu.VMEM((GROUP_W // 128, n, 128), F32),
                        pltpu.VMEM((8, 2 * S5_HALF), F32)],
        compiler_params=_params("arbitrary"),
        name="s5_branch",
    )(p.reshape(bsz, s, W_S5), bb, cc, tab, dsk, gw, gb)
    return out.reshape(bsz * s, GROUP_W)


def _gla_kernel(p_ref, w2_ref, gb_ref, nw_ref, o_ref, st_ref):
    T = GLA_TILE
    NCH = T // GLA_CHUNK

    @pl.when(pl.program_id(1) == 0)
    def _():
        st_ref[...] = jnp.zeros_like(st_ref)

    row = lax.broadcasted_iota(jnp.int32, (T, T), 0)
    col = lax.broadcasted_iota(jnp.int32, (T, T), 1)
    same = (row // GLA_CHUNK) == (col // GLA_CHUNK)
    intra = jnp.logical_and(same, col <= row)
    wide = (T, NCH * 128)
    blockmask = (lax.broadcasted_iota(jnp.int32, wide, 0) // GLA_CHUNK
                 == lax.broadcasted_iota(jnp.int32, wide, 1) // 128)
    wide_head = (lax.broadcasted_iota(jnp.int32, wide, 1) % 128) // GLA_DK
    lane_head = lax.broadcasted_iota(jnp.int32, (T, 128), 1) // GLA_DK
    zero = jnp.zeros((), BF16)
    scale = GLA_DK ** -0.5
    states = [st_ref[pair] for pair in range(GLA_HEADS // 2)]

    for sub in range(p_ref.shape[0] // T):
        rows = slice(T * sub, T * (sub + 1))
        x = _dot(p_ref[rows, 1536:1664].astype(BF16), w2_ref[...]) + gb_ref[...]
        g = -_softplus(-x) * (1.0 / GLA_GATE_NORM)
        b = _dot_01(intra, g)
        blast = _dot_01(same, g)
        q = p_ref[rows, 0:256] * scale
        k = p_ref[rows, 256:512]
        half = 0.5 * blast
        q_mid = (q * jnp.exp(b - half)).astype(BF16)
        k_mid = (k * jnp.exp(half - b)).astype(BF16)
        q_dec = (q * jnp.exp(b)).astype(BF16)
        k_dec = (k * jnp.exp(blast - b)).astype(BF16)

        for pair in range(GLA_HEADS // 2):
            lanes = slice(128 * pair, 128 * (pair + 1))
            qcat = jnp.where(blockmask, jnp.tile(q_dec[:, lanes], (1, NCH)), zero)
            kcat = jnp.where(blockmask, jnp.tile(k_dec[:, lanes], (1, NCH)), zero)
            vs = [p_ref[rows, 512 + GLA_DV * h:512 + GLA_DV * (h + 1)].astype(BF16)
                  for h in (2 * pair, 2 * pair + 1)]
            ds = sum(_dot_tn(vs[hh], jnp.where(wide_head == hh, kcat, zero)) for hh in range(2))
            st = states[pair]
            sts = []
            for c in range(NCH):
                sts.append(st)
                dec = jnp.exp(blast[GLA_CHUNK * c:GLA_CHUNK * c + 1, lanes])
                st = dec * st + ds[:, 128 * c:128 * (c + 1)]
            states[pair] = st
            st_all = jnp.concatenate(sts, axis=1).astype(BF16)
            for hh in range(2):
                h = 2 * pair + hh
                attn = _dot_nt(jnp.where(lane_head == hh, q_mid[:, lanes], zero), k_mid[:, lanes])
                o = _dot(jnp.where(intra, attn, 0.0).astype(BF16), vs[hh])
                o = o + _dot_nt(jnp.where(wide_head == hh, qcat, zero), st_all)
                ms = jnp.mean(o * o, axis=-1, keepdims=True)
                o = o * lax.rsqrt(ms + NORM_EPS) * nw_ref[...]
                gate = p_ref[rows, 1024 + GLA_DV * h:1024 + GLA_DV * (h + 1)]
                o_ref[rows, GLA_DV * h:GLA_DV * (h + 1)] = (o * _silu(gate)).astype(BF16)

    for pair in range(GLA_HEADS // 2):
        st_ref[pair] = states[pair]


def _gla_branch(l, p, w2, gb, nw, bsz, s):
    rows = GLA_TILE * GLA_TILES_PER_STEP
    nt = s // rows
    return pl.pallas_call(
        _gla_kernel,
        grid=(bsz, nt),
        in_specs=[pl.BlockSpec((rows, W_GLA), lambda b, t: (b * nt + t, 0)),
                  _layer((128, 256), l), _layer((1, 256), l), _layer((1, GLA_DV), l)],
        out_specs=pl.BlockSpec((rows, GROUP_W), lambda b, t: (b * nt + t, 0)),
        out_shape=jax.ShapeDtypeStruct((bsz * s, GROUP_W), BF16),
        scratch_shapes=[pltpu.VMEM((GLA_HEADS // 2, GLA_DV, 2 * GLA_DK), F32)],
        compiler_params=_params("arbitrary", "arbitrary"),
        name="gla_branch",
    )(p, w2, gb, nw)


def _rope_swap(t):
    lane = lax.broadcasted_iota(jnp.int32, t.shape, 1)
    return jnp.where(lane < MLA_ROPE // 2, pltpu.roll(t, 128 - MLA_ROPE // 2, 1), pltpu.roll(t, MLA_ROPE // 2, 1))


def _mla_prep_kernel(p_ref, pos_ref, invf_ref, sign_ref, qnw_ref, wuq_ref, kvnw_ref, wukv_ref,
                     qhw_ref, khw_ref, q_ref, k_ref, v_ref):
    cq = p_ref[:, 0:384]
    ckv = p_ref[:, 384:512]
    kpe = p_ref[:, 512:640]
    ms = jnp.mean(cq * cq, axis=-1, keepdims=True)
    qn = (cq * lax.rsqrt(ms + NORM_EPS) * qnw_ref[...]).astype(BF16)
    ms = jnp.mean(ckv * ckv, axis=-1, keepdims=True)
    kvn = (ckv * lax.rsqrt(ms + NORM_EPS) * kvnw_ref[...]).astype(BF16)
    q = _dot(qn, wuq_ref[...])
    kv = _dot(kvn, wukv_ref[...])
    ang = pos_ref[...].astype(F32) * invf_ref[...]
    cos_t = jnp.cos(ang)
    sin_t = jnp.sin(ang) * sign_ref[...]
    inv_d = 1.0 / (MLA_NOPE + MLA_ROPE)
    scale = (MLA_NOPE + MLA_ROPE) ** -0.5
    kpe_ss = jnp.sum(kpe * kpe, axis=-1, keepdims=True)
    qhw = qhw_ref[...]
    khw = khw_ref[...]
    for h in range(MLA_HEADS):
        lo = MLA_HEAD_PAD * h
        qh = q[:, lo:lo + MLA_HEAD_PAD]
        r = lax.rsqrt(jnp.sum(qh * qh, axis=-1, keepdims=True) * inv_d + NORM_EPS) * scale
        qh = qh * r * qhw
        t = qh[:, 128:256]
        q_ref[:, lo:lo + 128] = qh[:, 0:128].astype(BF16)
        q_ref[:, lo + 128:lo + 256] = (t * cos_t + _rope_swap(t) * sin_t).astype(BF16)
        kn = kv[:, lo:lo + 128]
        r = lax.rsqrt((jnp.sum(kn * kn, axis=-1, keepdims=True) + kpe_ss) * inv_d + NORM_EPS)
        k_ref[:, lo:lo + 128] = (kn * r * khw[:, 0:128]).astype(BF16)
        t = kpe * r * khw[:, 128:256]
        k_ref[:, lo + 128:lo + 256] = (t * cos_t + _rope_swap(t) * sin_t).astype(BF16)
        v_ref[:, MLA_V * h:MLA_V * (h + 1)] = kv[:, lo + 128:lo + 256].astype(BF16)


def _mla_prep(l, p, pos, invf, sign, qnw, wuq, kvnw, wukv, qhw, khw, tm=256):
    t = p.shape[0]
    const = lambda shape: pl.BlockSpec(shape, lambda i: tuple(0 for _ in shape))
    hp = MLA_HEADS * MLA_HEAD_PAD
    return pl.pallas_call(
        _mla_prep_kernel,
        grid=(t // tm,),
        in_specs=[pl.BlockSpec((tm, W_MLA), lambda i: (i, 0)),
                  pl.BlockSpec((tm, 1), lambda i: (i, 0)),
                  const((1, 128)), const((1, 128)),
                  _layer((1, MLA_Q_RANK), l), _layer((MLA_Q_RANK, hp), l),
                  _layer((1, MLA_KV_RANK), l), _layer((MLA_KV_RANK, hp), l),
                  _layer((1, MLA_HEAD_PAD), l), _layer((1, MLA_HEAD_PAD), l)],
        out_specs=[pl.BlockSpec((tm, hp), lambda i: (i, 0)),
                   pl.BlockSpec((tm, hp), lambda i: (i, 0)),
                   pl.BlockSpec((tm, MLA_HEADS * MLA_V), lambda i: (i, 0))],
        out_shape=[jax.ShapeDtypeStruct((t, hp), BF16),
                   jax.ShapeDtypeStruct((t, hp), BF16),
                   jax.ShapeDtypeStruct((t, MLA_HEADS * MLA_V), BF16)],
        compiler_params=_params("arbitrary"),
        name="mla_prep",
    )(p, pos, invf, sign, qnw, wuq, kvnw, wukv, qhw, khw)


def _flash_kernel(q_ref, k_ref, v_ref, p_ref, o_ref, s_ref):
    tq = FLASH_TQ
    hp = MLA_HEAD_PAD
    row = lax.broadcasted_iota(jnp.int32, (tq, tq), 0)
    col = lax.broadcasted_iota(jnp.int32, (tq, tq), 1)
    heads = range(MLA_HEADS)

    def attend(nb):
        q = [q_ref[:, hp * h:hp * (h + 1)] for h in heads]
        m = [None] * MLA_HEADS
        for j in range(nb):
            for h in heads:
                s = _dot_nt(q[h], k_ref[tq * j:tq * (j + 1), hp * h:hp * (h + 1)])
                if j == nb - 1:
                    s = jnp.where(col <= row, s, -jnp.inf)
                s_ref[h, :, tq * j:tq * (j + 1)] = s
                sm = jnp.maximum(s[:, 0:128], s[:, 128:256])
                m[h] = sm if m[h] is None else jnp.maximum(m[h], sm)
        m = [jnp.max(mh, axis=-1, keepdims=True) for mh in m]
        l = [jnp.zeros((tq, 128), F32) for _ in heads]
        acc = [jnp.zeros((tq, MLA_V), F32) for _ in heads]
        for j in range(nb):
            for h in heads:
                p = jnp.exp(s_ref[h, :, tq * j:tq * (j + 1)] - m[h])
                l[h] = l[h] + (p[:, 0:128] + p[:, 128:256])
                acc[h] = acc[h] + _dot(p.astype(BF16), v_ref[tq * j:tq * (j + 1), MLA_V * h:MLA_V * (h + 1)])
        for h in heads:
            lh = jnp.sum(l[h], axis=-1, keepdims=True)
            gate = p_ref[:, W_MLA - GROUP_W + MLA_V * h:W_MLA - GROUP_W + MLA_V * (h + 1)]
            o_ref[:, MLA_V * h:MLA_V * (h + 1)] = (acc[h] / lh * _silu(gate)).astype(BF16)

    for nb in range(1, s_ref.shape[2] // tq + 1):
        pl.when(pl.program_id(1) == nb - 1)(functools.partial(attend, nb))


def _flash(q, k, v, p, bsz, s):
    nq = s // FLASH_TQ
    hp = MLA_HEADS * MLA_HEAD_PAD
    return pl.pallas_call(
        _flash_kernel,
        grid=(bsz, nq),
        in_specs=[pl.BlockSpec((FLASH_TQ, hp), lambda b, i: (b * nq + i, 0)),
                  pl.BlockSpec((s, hp), lambda b, i: (b, 0)),
                  pl.BlockSpec((s, MLA_HEADS * MLA_V), lambda b, i: (b, 0)),
                  pl.BlockSpec((FLASH_TQ, W_MLA), lambda b, i: (b * nq + i, 0))],
        out_specs=pl.BlockSpec((FLASH_TQ, MLA_HEADS * MLA_V), lambda b, i: (b * nq + i, 0)),
        out_shape=jax.ShapeDtypeStruct((bsz * s, MLA_HEADS * MLA_V), BF16),
        scratch_shapes=[pltpu.VMEM((MLA_HEADS, FLASH_TQ, s), F32)],
        compiler_params=_params("arbitrary", "arbitrary"),
        name="mla_flash",
    )(q, k, v, p)


def _regroup_kernel(wt_ref, ssd_ref, s5_ref, gla_ref, mla_ref):
    offs = [0]
    for wd in IN_WIDTHS:
        offs.append(offs[-1] + wd)
    (z, xbc, dt, s5u, s5g, gq, gk, gv, gg, glr, cq, ckv, kpe, mg) = [
        (a, b) for a, b in zip(offs[:-1], offs[1:])]
    cols = wt_ref.shape[1]

    def put(dst, at, seg):
        dst[at:at + seg[1] - seg[0], :] = wt_ref[seg[0]:seg[1], :].astype(BF16)

    put(ssd_ref, 0, z)
    put(ssd_ref, 512, xbc)
    for h in range(SSD_HEADS):
        row = wt_ref[dt[0] + h:dt[0] + h + 1, :]
        ssd_ref[1280 + 64 * h:1280 + 64 * (h + 1), :] = jnp.broadcast_to(row, (64, cols)).astype(BF16)
    put(s5_ref, 0, s5u)
    put(s5_ref, 512, s5g)
    put(gla_ref, 0, gq)
    put(gla_ref, 256, gk)
    put(gla_ref, 512, gv)
    put(gla_ref, 1024, gg)
    gla_ref[1536:1664, :] = _pad_rows(wt_ref[glr[0]:glr[1], :], 128).astype(BF16)
    put(mla_ref, 0, cq)
    put(mla_ref, 384, ckv)
    mla_ref[512:640, :] = _pad_rows(wt_ref[kpe[0]:kpe[1], :], 128).astype(BF16)
    put(mla_ref, 640, mg)


def _pad_rows(x, n):
    return jnp.concatenate([x, jnp.zeros((n - x.shape[0], x.shape[1]), x.dtype)], axis=0)


def _regroup_w_in(w_in, tc=256):
    wt = jnp.swapaxes(w_in, 1, 2)
    depth, n, k = wt.shape
    widths = (W_SSD, W_S5, W_GLA, W_MLA)
    return pl.pallas_call(
        _regroup_kernel,
        grid=(depth, k // tc),
        in_specs=[pl.BlockSpec((None, n, tc), lambda l, i: (l, 0, i))],
        out_specs=[pl.BlockSpec((None, w, tc), lambda l, i: (l, 0, i)) for w in widths],
        out_shape=[jax.ShapeDtypeStruct((depth, w, k), BF16) for w in widths],
        compiler_params=_params("arbitrary", "arbitrary"),
        name="regroup_w_in",
    )(wt)


def _pad_last(w, n):
    return jnp.pad(w, [(0, 0)] * (w.ndim - 1) + [(0, n - w.shape[-1])])


def kernel(x, positions, norm_w, w_in, w_out, ssd_conv_w, ssd_conv_b, ssd_dt_bias, ssd_a_log, ssd_d, ssd_norm_w, s5_a_re, s5_a_im, s5_log_dt, s5_b_re, s5_b_im, s5_c_re, s5_c_im, s5_d, s5_glu_w, s5_glu_b, gla_gate_w2, gla_gate_b, gla_norm_w, mla_q_norm_w, mla_w_uq, mla_kv_norm_w, mla_w_ukv, mla_q_head_norm_w, mla_k_head_norm_w):
    bsz, s, d = x.shape
    depth = w_in.shape[0]
    t = bsz * s
    h = x.reshape(t, d)
    pos = positions.reshape(t, 1)
    row = lambda v: v[:, None, :]

    inv_freq = ROPE_THETA ** (-jnp.arange(0, MLA_ROPE, 2, dtype=F32) / MLA_ROPE)
    invf = jnp.concatenate([inv_freq, inv_freq, jnp.zeros((64,), F32)])[None, :]
    sign = jnp.concatenate([-jnp.ones((32,), F32), jnp.ones((32,), F32), jnp.zeros((64,), F32)])[None, :]

    w_ssd, w_s5, w_gla, w_mla = _regroup_w_in(w_in)
    w_out_b = w_out.astype(BF16)
    norm_w3 = row(norm_w)

    rep = lambda v: row(jnp.repeat(v, SSD_HEAD_DIM, axis=1))
    ssd_cb, ssd_dtb, ssd_alog, ssd_dsk, ssd_nw = (row(ssd_conv_b), rep(ssd_dt_bias), rep(ssd_a_log),
                                                   rep(ssd_d), row(ssd_norm_w))

    s5_are = s5_a_re.reshape(depth, 1, S5_NS)
    s5_aim = s5_a_im.reshape(depth, 1, S5_NS)
    s5_ldt = row(jnp.repeat(s5_log_dt, S5_STATE, axis=1))
    s5_bre, s5_bim = s5_b_re.transpose(0, 1, 3, 2), s5_b_im.transpose(0, 1, 3, 2)
    s5_cre, s5_cim = s5_c_re.transpose(0, 1, 3, 2), s5_c_im.transpose(0, 1, 3, 2)
    s5_dsk, s5_gw, s5_gb = row(s5_d), s5_glu_w.astype(BF16), row(s5_glu_b)

    gla_w2 = jnp.pad(gla_gate_w2, ((0, 0), (0, 112), (0, 0))).astype(BF16)
    gla_gb, gla_nw = row(gla_gate_b), row(gla_norm_w)

    wuq = _pad_last(mla_w_uq.reshape(depth, MLA_Q_RANK, MLA_HEADS, MLA_NOPE + MLA_ROPE), MLA_HEAD_PAD)
    wuq = wuq.reshape(depth, MLA_Q_RANK, MLA_HEADS * MLA_HEAD_PAD).astype(BF16)
    wukv = mla_w_ukv.astype(BF16)
    qnw, kvnw = row(mla_q_norm_w), row(mla_kv_norm_w)
    qhw, khw = row(_pad_last(mla_q_head_norm_w, MLA_HEAD_PAD)), row(_pad_last(mla_k_head_norm_w, MLA_HEAD_PAD))

    u = _rmsnorm_bf16(h, norm_w3, 0)
    for l in range(depth):
        p_ssd = _in_proj(u, w_ssd, l, "in_proj_ssd")
        p_s5 = _in_proj(u, w_s5, l, "in_proj_s5")
        p_gla = _in_proj(u, w_gla, l, "in_proj_gla")
        p_mla = _in_proj(u, w_mla, l, "in_proj_mla")

        y_a = _ssd_branch(l, p_ssd, ssd_conv_w, ssd_cb, ssd_dtb, ssd_alog, ssd_dsk, ssd_nw, bsz, s)

        bb, cc, tab = _s5_prep(l, s5_are, s5_aim, s5_ldt, s5_bre, s5_bim, s5_cre, s5_cim)
        y_b = _s5_branch(l, p_s5, bb, cc, tab, s5_dsk, s5_gw, s5_gb, bsz, s)

        y_c = _gla_branch(l, p_gla, gla_w2, gla_gb, gla_nw, bsz, s)

        q, k, v = _mla_prep(l, p_mla, pos, invf, sign, qnw, wuq, kvnw, wukv, qhw, khw)
        y_d = _flash(q, k, v, p_mla, bsz, s)

        h, u = _out_proj((y_a, y_b, y_c, y_d), w_out_b, h, norm_w3, l, (l + 1) % depth)
    return h.reshape(bsz, s, d)
```

```python
import functools
import math

import jax
import jax.numpy as jnp
from jax import lax
from jax.experimental import pallas as pl
from jax.experimental.pallas import tpu as pltpu

F32 = jnp.float32
BF16 = jnp.bfloat16
NORM_EPS = 1e-6
HI = lax.Precision.HIGHEST

D_MODEL = 2048
GROUP_W = 512
SSD_HEADS = 8
SSD_HEAD_DIM = 64
SSD_STATE = 64
SSD_CHUNK = 128
SSD_XBC = 768
SSD_CONV = 4
S5_GROUPS = 32
S5_CH = 16
S5_STATE = 64
S5_NS = S5_GROUPS * S5_STATE
S5_HALF = S5_NS // 2
S5_TSTEP = 64
S5_SCAN_LANES = 512
GLA_HEADS = 4
GLA_DK = 64
GLA_DV = 128
GLA_CHUNK = 16
GLA_TILE = 128
GLA_TILES_PER_STEP = 4
GLA_GATE_NORM = 16.0
MLA_HEADS = 4
MLA_NOPE = 128
MLA_ROPE = 64
MLA_V = 128
MLA_Q_RANK = 384
MLA_KV_RANK = 128
MLA_HEAD_PAD = 256
ROPE_THETA = 10000.0
FLASH_TQ = 256

IN_WIDTHS = (512, 768, 8, 512, 512, 256, 256, 512, 512, 16, 384, 128, 64, 512)
W_SSD = 512 + 768 + 512
W_S5 = 1024
W_GLA = 256 + 256 + 512 + 512 + 128
W_MLA = 384 + 128 + 128 + 512

VMEM_LIMIT_BYTES = 56 * 1024 * 1024


def _params(*sem):
    return pltpu.CompilerParams(dimension_semantics=sem, vmem_limit_bytes=VMEM_LIMIT_BYTES)


def _layer(shape, l):
    return pl.BlockSpec((None,) + tuple(shape), lambda *_: (l,) + (0,) * len(shape))


def _silu(x):
    return x / (1.0 + jnp.exp(-x))


def _softplus(x):
    return jnp.maximum(x, 0.0) + jnp.log(1.0 + jnp.exp(-jnp.abs(x)))


def _dot(a, b):
    return jnp.dot(a, b, preferred_element_type=F32)


def _dot_nt(a, b):
    return lax.dot_general(a, b, (((1,), (1,)), ((), ())), preferred_element_type=F32)


def _dot_tn(a, b):
    return lax.dot_general(a, b, (((0,), (0,)), ((), ())), preferred_element_type=F32)


def _dot_01(mask, x):
    m = mask.astype(BF16)
    hi = x.astype(BF16)
    rest = x - hi.astype(F32)
    mid = rest.astype(BF16)
    lo = (rest - mid.astype(F32)).astype(BF16)
    return _dot(m, hi) + _dot(m, mid) + _dot(m, lo)


def _rms_kernel(x_ref, w_ref, o_ref):
    x = x_ref[...]
    ms = jnp.mean(x * x, axis=-1, keepdims=True)
    o_ref[...] = (x * lax.rsqrt(ms + NORM_EPS) * w_ref[...]).astype(BF16)


def _rmsnorm_bf16(x, w, l, tm=512):
    t, d = x.shape
    return pl.pallas_call(
        _rms_kernel,
        grid=(t // tm,),
        in_specs=[pl.BlockSpec((tm, d), lambda i: (i, 0)), _layer((1, d), l)],
        out_specs=pl.BlockSpec((tm, d), lambda i: (i, 0)),
        out_shape=jax.ShapeDtypeStruct((t, d), BF16),
        compiler_params=_params("arbitrary"),
        name="rmsnorm_in",
    )(x, w)


def _mm_kernel(x_ref, wt_ref, o_ref):
    o_ref[...] = _dot_nt(x_ref[...], wt_ref[...])


def _in_proj(u, wt, l, name, tm=512):
    t, k = u.shape
    n = wt.shape[1]
    return pl.pallas_call(
        _mm_kernel,
        grid=(t // tm,),
        in_specs=[pl.BlockSpec((tm, k), lambda i: (i, 0)), _layer((n, k), l)],
        out_specs=pl.BlockSpec((tm, n), lambda i: (i, 0)),
        out_shape=jax.ShapeDtypeStruct((t, n), F32),
        compiler_params=_params("arbitrary"),
        name=name,
    )(u, wt)


def _out_proj_kernel(ya_ref, yb_ref, yc_ref, yd_ref, w_ref, h_ref, nw_ref, ho_ref, uo_ref):
    sub = 256
    for r in range(h_ref.shape[0] // sub):
        rows = slice(sub * r, sub * (r + 1))
        acc = h_ref[rows, :]
        for i, y_ref in enumerate((ya_ref, yb_ref, yc_ref, yd_ref)):
            acc = acc + _dot(y_ref[rows, :], w_ref[GROUP_W * i:GROUP_W * (i + 1), :])
        ho_ref[rows, :] = acc
        ms = jnp.mean(acc * acc, axis=-1, keepdims=True)
        uo_ref[rows, :] = (acc * lax.rsqrt(ms + NORM_EPS) * nw_ref[...]).astype(BF16)


def _out_proj(ys, w, h, nw, l, l_next, tm=512):
    t, d = h.shape
    yspec = pl.BlockSpec((tm, GROUP_W), lambda i: (i, 0))
    return pl.pallas_call(
        _out_proj_kernel,
        grid=(t // tm,),
        in_specs=[yspec, yspec, yspec, yspec,
                  _layer((d, d), l),
                  pl.BlockSpec((tm, d), lambda i: (i, 0)),
                  _layer((1, d), l_next)],
        out_specs=[pl.BlockSpec((tm, d), lambda i: (i, 0)), pl.BlockSpec((tm, d), lambda i: (i, 0))],
        out_shape=[jax.ShapeDtypeStruct((t, d), F32), jax.ShapeDtypeStruct((t, d), BF16)],
        compiler_params=_params("arbitrary"),
        name="out_proj",
    )(*ys, w, h, nw)


def _ssd_kernel(p_ref, cw_ref, cb_ref, dtb_ref, alog_ref, d_ref, nw_ref, o_ref, cbuf, st_ref):
    L = SSD_CHUNK

    @pl.when(pl.program_id(1) == 0)
    def _():
        cbuf[0:8, :] = jnp.zeros((8, SSD_XBC), F32)
        st_ref[...] = jnp.zeros_like(st_ref)

    z = p_ref[:, 0:512]
    cbuf[8:8 + L, :] = p_ref[:, 512:512 + SSD_XBC]
    acc = cb_ref[...] + cbuf[pl.ds(8 - (SSD_CONV - 1), L), :] * cw_ref[0:1, :]
    for k in range(1, SSD_CONV):
        acc = acc + cbuf[pl.ds(8 - (SSD_CONV - 1) + k, L), :] * cw_ref[k:k + 1, :]
    cbuf[0:8, :] = cbuf[L:L + 8, :]
    xbc = _silu(acc)
    xs = xbc[:, 0:512]
    bm = xbc[:, 512:640]
    cm = xbc[:, 640:768]

    dt = _softplus(p_ref[:, 1280:1792] + dtb_ref[...])
    a = -jnp.exp(alog_ref[...])
    row = lax.broadcasted_iota(jnp.int32, (L, L), 0)
    col = lax.broadcasted_iota(jnp.int32, (L, L), 1)
    cs = _dot_01(col <= row, dt * a)
    cs_t = jnp.transpose(cs)
    cs_last = cs[L - 1:L, :]
    grow = jnp.exp(cs)
    tail = jnp.exp(cs_last - cs)
    total = jnp.exp(cs_last)
    xdt = (xs * dt).astype(BF16)
    bmb = bm.astype(BF16)

    lane = lax.broadcasted_iota(jnp.int32, (L, 128), 1)
    rowi = lax.broadcasted_iota(jnp.int32, (L, 128), 0)
    low = lane < 64
    low8 = lax.broadcasted_iota(jnp.int32, (8, 128), 1) < 64
    low64 = lax.broadcasted_iota(jnp.int32, (64, 128), 1) < 64
    keep = [rowi >= (lane % 64) + 64 * jh for jh in range(2)]
    blockdiag = (lax.broadcasted_iota(jnp.int32, (128, 128), 0) // 64
                 == lax.broadcasted_iota(jnp.int32, (128, 128), 1) // 64)

    def both_halves(x, g):
        r = pltpu.roll(x, 64, 1)
        return jnp.where(low, x, r) if g == 0 else jnp.where(low, r, x)

    ys = []
    for g in range(2):
        cmask = jnp.where(low if g == 0 else jnp.logical_not(low), cm, 0.0).astype(BF16)
        gdup = [_dot_nt(cmask, jnp.concatenate([bmb[64 * jh:64 * jh + 64, :]] * 2, axis=0)) for jh in range(2)]
        cdup = both_halves(cm, g)
        bdup = both_halves(bm, g)
        for pp in range(2):
            pair = 2 * g + pp
            lanes = slice(128 * pair, 128 * (pair + 1))
            csp = cs[:, lanes]
            xp = xdt[:, lanes]
            r0 = cs_t[128 * pair:128 * pair + 8, :]
            r1 = cs_t[128 * pair + 64:128 * pair + 72, :]
            crow = [jnp.where(low8, r0, pltpu.roll(r1, 64, 1))[0:1, :],
                    jnp.where(low8, pltpu.roll(r0, 64, 1), r1)[0:1, :]]
            y = None
            for jh in range(2):
                lmat = jnp.exp(jnp.where(keep[jh], csp - crow[jh], -jnp.inf))
                xj = xp[64 * jh:64 * jh + 64, :]
                zero = jnp.zeros_like(xj)
                xbd = jnp.concatenate([jnp.where(low64, xj, zero), jnp.where(low64, zero, xj)], axis=0)
                t = _dot((gdup[jh] * lmat).astype(BF16), xbd)
                y = t if y is None else y + t
            st = st_ref[pair]
            y = y + _dot_nt((cdup * grow[:, lanes]).astype(BF16), st.astype(BF16))
            upd = _dot_tn(xp, (bdup * tail[:, lanes]).astype(BF16))
            st_ref[pair] = total[:, lanes] * st + jnp.where(blockdiag, upd, 0.0)
            ys.append(y)
    y = jnp.concatenate(ys, axis=1) + d_ref[...] * xs
    y = y * _silu(z)
    ms = jnp.mean(y * y, axis=-1, keepdims=True)
    o_ref[...] = (y * lax.rsqrt(ms + NORM_EPS) * nw_ref[...]).astype(BF16)


def _ssd_branch(l, p, cw, cb, dtb, alog, dsk, nw, bsz, s):
    nc = s // SSD_CHUNK
    return pl.pallas_call(
        _ssd_kernel,
        grid=(bsz, nc),
        in_specs=[pl.BlockSpec((SSD_CHUNK, W_SSD), lambda b, c: (b * nc + c, 0)),
                  _layer((SSD_CONV, SSD_XBC), l), _layer((1, SSD_XBC), l),
                  _layer((1, 512), l), _layer((1, 512), l), _layer((1, 512), l), _layer((1, 512), l)],
        out_specs=pl.BlockSpec((SSD_CHUNK, GROUP_W), lambda b, c: (b * nc + c, 0)),
        out_shape=jax.ShapeDtypeStruct((bsz * s, GROUP_W), BF16),
        scratch_shapes=[pltpu.VMEM((SSD_CHUNK + 8, SSD_XBC), F32),
                        pltpu.VMEM((SSD_HEADS // 2, 2 * SSD_HEAD_DIM, 2 * SSD_STATE), F32)],
        compiler_params=_params("arbitrary", "arbitrary"),
        name="ssd_branch",
    )(p, cw, cb, dtb, alog, dsk, nw)


def _s5_prep_kernel(are_ref, aim_ref, ldt_ref, bre_ref, bim_ref, cre_ref, cim_ref, bb_ref, cc_ref, tab_ref):
    are = are_ref[...]
    aim = aim_ref[...]
    delta = jnp.exp(ldt_ref[...])
    mag = jnp.exp(are * delta)
    ar = mag * jnp.cos(aim * delta)
    ai = mag * jnp.sin(aim * delta)
    den = are * are + aim * aim
    coef_re = ((ar - 1.0) * are + ai * aim) / den
    coef_im = (ai * are - (ar - 1.0) * aim) / den

    bb_ref[...] = jnp.zeros_like(bb_ref)
    cc_ref[...] = jnp.zeros_like(cc_ref)
    for g in range(S5_GROUPS):
        kb, gl = divmod(g, 16)
        st = slice(S5_STATE * g, S5_STATE * (g + 1))
        rows = slice(S5_CH * gl, S5_CH * (gl + 1))
        re = slice(S5_STATE * gl, S5_STATE * (gl + 1))
        im = slice(S5_HALF + S5_STATE * gl, S5_HALF + S5_STATE * (gl + 1))
        bre = bre_ref[g]
        bim = bim_ref[g]
        bb_ref[kb, rows, re] = (coef_re[:, st] * bre - coef_im[:, st] * bim).astype(BF16)
        bb_ref[kb, rows, im] = (coef_re[:, st] * bim + coef_im[:, st] * bre).astype(BF16)
        cc_ref[kb, re, rows] = cre_ref[g].astype(BF16)
        cc_ref[kb, im, rows] = (-cim_ref[g]).astype(BF16)

    odd = lax.broadcasted_iota(jnp.int32, (8, S5_HALF), 0) % 2 == 1
    tab_ref[0] = jnp.where(odd, ar[:, S5_HALF:], ar[:, :S5_HALF])
    tab_ref[1] = jnp.where(odd, ai[:, S5_HALF:], ai[:, :S5_HALF])


def _s5_prep(l, are, aim, ldt, bre, bim, cre, cim):
    out3 = lambda shape: pl.BlockSpec(shape, lambda i: (0, 0, 0))
    return pl.pallas_call(
        _s5_prep_kernel,
        grid=(1,),
        in_specs=[_layer((1, S5_NS), l), _layer((1, S5_NS), l), _layer((1, S5_NS), l),
                  _layer((S5_GROUPS, S5_CH, S5_STATE), l), _layer((S5_GROUPS, S5_CH, S5_STATE), l),
                  _layer((S5_GROUPS, S5_STATE, S5_CH), l), _layer((S5_GROUPS, S5_STATE, S5_CH), l)],
        out_specs=[out3((2, 256, 2 * S5_HALF)), out3((2, 2 * S5_HALF, 256)), out3((2, 8, S5_HALF))],
        out_shape=[jax.ShapeDtypeStruct((2, 256, 2 * S5_HALF), BF16),
                   jax.ShapeDtypeStruct((2, 2 * S5_HALF, 256), BF16),
                   jax.ShapeDtypeStruct((2, 8, S5_HALF), F32)],
        compiler_params=_params("arbitrary"),
        name="s5_prep",
    )(are, aim, ldt, bre, bim, cre, cim)


def _s5_kernel(p_ref, bb_ref, cc_ref, tab_ref, d_ref, gw_ref, gb_ref, o_ref, pbuf, hbuf, ybuf, carry):
    nb = p_ref.shape[0]
    rt = S5_TSTEP
    n = nb * rt

    @pl.when(pl.program_id(0) == 0)
    def _():
        carry[...] = jnp.zeros_like(carry)

    def cat(ref, blocks, rows):
        return jnp.concatenate([ref[c, rows, :] for c in blocks], axis=1)

    for b in range(nb):
        for c in range(W_S5 // 128):
            pbuf[c, pl.ds(b, rt, stride=nb), :] = p_ref[b, :, 128 * c:128 * (c + 1)]
    u = cat(pbuf, range(0, 4), slice(None))
    ub = u.astype(BF16)
    nblk = 2 * S5_HALF // 128
    for j in range(2):
        hj = _dot(ub[:, 256 * j:256 * (j + 1)], bb_ref[j])
        for c in range(nblk):
            hbuf[c, pl.ds(j, n, stride=2), :] = hj[:, 128 * c:128 * (c + 1)]

    wb = S5_SCAN_LANES // 128
    for lc in range(S5_HALF // S5_SCAN_LANES):
        re_blocks = list(range(wb * lc, wb * (lc + 1)))
        im_blocks = [c + nblk // 2 for c in re_blocks]
        lanes = slice(S5_SCAN_LANES * lc, S5_SCAN_LANES * (lc + 1))
        ar = tab_ref[0, :, lanes]
        ai = tab_ref[1, :, lanes]

        def body(t, c, re_blocks=re_blocks, im_blocks=im_blocks, ar=ar, ai=ai):
            hr, hi = c
            rows = pl.ds(pl.multiple_of(t * 8, 8), 8)
            hr, hi = (ar * hr - ai * hi + cat(hbuf, re_blocks, rows),
                      ar * hi + ai * hr + cat(hbuf, im_blocks, rows))
            for k in range(wb):
                hbuf[re_blocks[k], rows, :] = hr[:, 128 * k:128 * (k + 1)]
                hbuf[im_blocks[k], rows, :] = hi[:, 128 * k:128 * (k + 1)]
            return hr, hi

        re = slice(S5_SCAN_LANES * lc, S5_SCAN_LANES * (lc + 1))
        im = slice(S5_HALF + re.start, S5_HALF + re.stop)
        hr, hi = lax.fori_loop(0, rt, body, (carry[:, re], carry[:, im]), unroll=4)
        carry[:, re] = hr
        carry[:, im] = hi

    ys = [_dot(cat(hbuf, range(nblk), pl.ds(j, n, stride=2)).astype(BF16), cc_ref[j]) for j in range(2)]
    y = jnp.concatenate(ys, axis=1) + d_ref[...] * u
    y = 0.5 * y * (1.0 + jnp.tanh(math.sqrt(2.0 / math.pi) * (y + 0.044715 * (y * y * y))))
    glu = _dot(y.astype(BF16), gw_ref[...]) + gb_ref[...]
    y = y / (1.0 + jnp.exp(-glu))
    y = y * _silu(cat(pbuf, range(4, 8), slice(None)))
    for c in range(GROUP_W // 128):
        ybuf[c] = y[:, 128 * c:128 * (c + 1)]
    for b in range(nb):
        o_ref[b] = cat(ybuf, range(GROUP_W // 128), pl.ds(b, rt, stride=nb)).astype(BF16)


def _s5_branch(l, p, bb, cc, tab, dsk, gw, gb, bsz, s):
    assert 2 * bsz == 8, "scan rows (batch, state block) must fill the 8 sublanes"
    const = lambda shape: pl.BlockSpec(shape, lambda t: tuple(0 for _ in shape))
    n = bsz * S5_TSTEP
    out = pl.pallas_call(
        _s5_kernel,
        grid=(s // S5_TSTEP,),
        in_specs=[pl.BlockSpec((bsz, S5_TSTEP, W_S5), lambda t: (0, t, 0)),
                  const((2, 256, 2 * S5_HALF)), const((2, 2 * S5_HALF, 256)), const((2, 8, S5_HALF)),
                  _layer((1, GROUP_W), l), _layer((GROUP_W, GROUP_W), l), _layer((1, GROUP_W), l)],
        out_specs=pl.BlockSpec((bsz, S5_TSTEP, GROUP_W), lambda t: (0, t, 0)),
        out_shape=jax.ShapeDtypeStruct((bsz, s, GROUP_W), BF16),
        scratch_shapes=[pltpu.VMEM((W_S5 // 128, n, 128), F32),
                        pltpu.VMEM((2 * S5_HALF // 128, 2 * n, 128), F32),
                        pltpu.VMEM((GROUP_W // 128, n, 128), F32),
                        pltpu.VMEM((8, 2 * S5_HALF), F32)],
        compiler_params=_params("arbitrary"),
        name="s5_branch",
    )(p.reshape(bsz, s, W_S5), bb, cc, tab, dsk, gw, gb)
    return out.reshape(bsz * s, GROUP_W)


def _gla_kernel(p_ref, w2_ref, gb_ref, nw_ref, o_ref, st_ref):
    T = GLA_TILE
    NCH = T // GLA_CHUNK

    @pl.when(pl.program_id(1) == 0)
    def _():
        st_ref[...] = jnp.zeros_like(st_ref)

    row = lax.broadcasted_iota(jnp.int32, (T, T), 0)
    col = lax.broadcasted_iota(jnp.int32, (T, T), 1)
    same = (row // GLA_CHUNK) == (col // GLA_CHUNK)
    intra = jnp.logical_and(same, col <= row)
    wide = (T, NCH * 128)
    blockmask = (lax.broadcasted_iota(jnp.int32, wide, 0) // GLA_CHUNK
                 == lax.broadcasted_iota(jnp.int32, wide, 1) // 128)
    wide_head = (lax.broadcasted_iota(jnp.int32, wide, 1) % 128) // GLA_DK
    lane_head = lax.broadcasted_iota(jnp.int32, (T, 128), 1) // GLA_DK
    zero = jnp.zeros((), BF16)
    scale = GLA_DK ** -0.5
    states = [st_ref[pair] for pair in range(GLA_HEADS // 2)]

    subs = range(p_ref.shape[0] // T)
    pairs = range(GLA_HEADS // 2)
    rows = [slice(T * sub, T * (sub + 1)) for sub in subs]
    xs = [_dot(p_ref[rows[s], 1536:1664].astype(BF16), w2_ref[...]) + gb_ref[...] for s in subs]
    gs = [-_softplus(-x) * (1.0 / GLA_GATE_NORM) for x in xs]
    bs = [_dot_01(intra, g) for g in gs]
    blasts = [_dot_01(same, g) for g in gs]
    q_mid, k_mid, q_dec, k_dec = [], [], [], []
    for s in subs:
        q = p_ref[rows[s], 0:256] * scale
        k = p_ref[rows[s], 256:512]
        half = 0.5 * blasts[s]
        q_mid.append((q * jnp.exp(bs[s] - half)).astype(BF16))
        k_mid.append((k * jnp.exp(half - bs[s])).astype(BF16))
        q_dec.append((q * jnp.exp(bs[s])).astype(BF16))
        k_dec.append((k * jnp.exp(blasts[s] - bs[s])).astype(BF16))

    lanes = [slice(128 * pair, 128 * (pair + 1)) for pair in pairs]
    vs = [[p_ref[rows[s], 512 + GLA_DV * h:512 + GLA_DV * (h + 1)].astype(BF16) for h in range(GLA_HEADS)]
          for s in subs]
    ds = {}
    for s in subs:
        for pair in pairs:
            kcat = jnp.where(blockmask, jnp.tile(k_dec[s][:, lanes[pair]], (1, NCH)), zero)
            ds[s, pair] = sum(_dot_tn(vs[s][2 * pair + hh], jnp.where(wide_head == hh, kcat, zero))
                              for hh in range(2))
    st_all = {}
    for s in subs:
        for pair in pairs:
            st = states[pair]
            sts = []
            for c in range(NCH):
                sts.append(st)
                dec = jnp.exp(blasts[s][GLA_CHUNK * c:GLA_CHUNK * c + 1, lanes[pair]])
                st = dec * st + ds[s, pair][:, 128 * c:128 * (c + 1)]
            states[pair] = st
            st_all[s, pair] = jnp.concatenate(sts, axis=1).astype(BF16)
    for pair in pairs:
        st_ref[pair] = states[pair]

    for s in subs:
        for pair in pairs:
            qcat = jnp.where(blockmask, jnp.tile(q_dec[s][:, lanes[pair]], (1, NCH)), zero)
            for hh in range(2):
                h = 2 * pair + hh
                attn = _dot_nt(jnp.where(lane_head == hh, q_mid[s][:, lanes[pair]], zero), k_mid[s][:, lanes[pair]])
                o = _dot(jnp.where(intra, attn, 0.0).astype(BF16), vs[s][h])
                o = o + _dot_nt(jnp.where(wide_head == hh, qcat, zero), st_all[s, pair])
                ms = jnp.mean(o * o, axis=-1, keepdims=True)
                o = o * lax.rsqrt(ms + NORM_EPS) * nw_ref[...]
                gate = p_ref[rows[s], 1024 + GLA_DV * h:1024 + GLA_DV * (h + 1)]
                o_ref[rows[s], GLA_DV * h:GLA_DV * (h + 1)] = (o * _silu(gate)).astype(BF16)


def _gla_branch(l, p, w2, gb, nw, bsz, s):
    rows = GLA_TILE * GLA_TILES_PER_STEP
    nt = s // rows
    return pl.pallas_call(
        _gla_kernel,
        grid=(bsz, nt),
        in_specs=[pl.BlockSpec((rows, W_GLA), lambda b, t: (b * nt + t, 0)),
                  _layer((128, 256), l), _layer((1, 256), l), _layer((1, GLA_DV), l)],
        out_specs=pl.BlockSpec((rows, GROUP_W), lambda b, t: (b * nt + t, 0)),
        out_shape=jax.ShapeDtypeStruct((bsz * s, GROUP_W), BF16),
        scratch_shapes=[pltpu.VMEM((GLA_HEADS // 2, GLA_DV, 2 * GLA_DK), F32)],
        compiler_params=_params("arbitrary", "arbitrary"),
        name="gla_branch",
    )(p, w2, gb, nw)


def _rope_swap(t):
    return pltpu.roll(t, 64, 1)


def _spread_rope(w):
    half = MLA_ROPE // 2
    zeros = jnp.zeros(w.shape[:-1] + (64 - half,), w.dtype)
    return jnp.concatenate([w[..., :half], zeros, w[..., half:], zeros], axis=-1)


def _rope_kernel(pos_ref, invf_ref, sign_ref, cos_ref, sin_ref):
    ang = pos_ref[...].astype(F32) * invf_ref[...]
    cos_ref[...] = jnp.cos(ang)
    sin_ref[...] = jnp.sin(ang) * sign_ref[...]


def _rope_tables(pos, invf, sign, tm=512):
    t = pos.shape[0]
    const = pl.BlockSpec((1, 128), lambda i: (0, 0))
    tile = pl.BlockSpec((tm, 128), lambda i: (i, 0))
    return pl.pallas_call(
        _rope_kernel,
        grid=(t // tm,),
        in_specs=[pl.BlockSpec((tm, 1), lambda i: (i, 0)), const, const],
        out_specs=[tile, tile],
        out_shape=[jax.ShapeDtypeStruct((t, 128), F32)] * 2,
        compiler_params=_params("arbitrary"),
        name="rope_tables",
    )(pos, invf, sign)


def _mla_prep_kernel(p_ref, cos_ref, sin_ref, qnw_ref, wuq_ref, kvnw_ref, wukv_ref,
                     qhw_ref, khw_ref, q_ref, k_ref, v_ref):
    inv_d = 1.0 / (MLA_NOPE + MLA_ROPE)
    scale = (MLA_NOPE + MLA_ROPE) ** -0.5
    qhw = qhw_ref[...]
    khw = khw_ref[...]
    sub = 128
    for r0 in range(0, p_ref.shape[0], sub):
        rows = slice(r0, r0 + sub)
        cq = p_ref[rows, 0:384]
        ckv = p_ref[rows, 384:512]
        kpe = p_ref[rows, 512:640]
        ms = jnp.mean(cq * cq, axis=-1, keepdims=True)
        qn = (cq * lax.rsqrt(ms + NORM_EPS) * qnw_ref[...]).astype(BF16)
        ms = jnp.mean(ckv * ckv, axis=-1, keepdims=True)
        kvn = (ckv * lax.rsqrt(ms + NORM_EPS) * kvnw_ref[...]).astype(BF16)
        cos_t = cos_ref[rows, :]
        sin_t = sin_ref[rows, :]
        kpe_ss = jnp.sum(kpe * kpe, axis=-1, keepdims=True)
        heads = range(MLA_HEADS)
        hp = MLA_HEAD_PAD
        qs = [_dot(qn, wuq_ref[:, hp * h:hp * (h + 1)]) for h in heads]
        kvs = [_dot(kvn, wukv_ref[:, hp * h:hp * (h + 1)]) for h in heads]
        q_ss = [jnp.sum(q * q, axis=-1, keepdims=True) for q in qs]
        k_ss = [jnp.sum(kv[:, 0:128] * kv[:, 0:128], axis=-1, keepdims=True) for kv in kvs]
        q_rs = [lax.rsqrt(ss * inv_d + NORM_EPS) * scale for ss in q_ss]
        k_rs = [lax.rsqrt((ss + kpe_ss) * inv_d + NORM_EPS) for ss in k_ss]
        q_rot = [qs[h][:, 128:256] * q_rs[h] * qhw[:, 128:256] for h in heads]
        k_rot = [kpe * k_rs[h] * khw[:, 128:256] for h in heads]
        q_swap = [_rope_swap(t) for t in q_rot]
        k_swap = [_rope_swap(t) for t in k_rot]
        for h in heads:
            lo = hp * h
            q_ref[rows, lo:lo + 128] = (qs[h][:, 0:128] * q_rs[h] * qhw[:, 0:128]).astype(BF16)
            q_ref[rows, lo + 128:lo + 256] = (q_rot[h] * cos_t + q_swap[h] * sin_t).astype(BF16)
            k_ref[rows, lo:lo + 128] = (kvs[h][:, 0:128] * k_rs[h] * khw[:, 0:128]).astype(BF16)
            k_ref[rows, lo + 128:lo + 256] = (k_rot[h] * cos_t + k_swap[h] * sin_t).astype(BF16)
            v_ref[rows, MLA_V * h:MLA_V * (h + 1)] = kvs[h][:, 128:256].astype(BF16)


def _mla_prep(l, p, cos_t, sin_t, qnw, wuq, kvnw, wukv, qhw, khw, tm=256):
    t = p.shape[0]
    hp = MLA_HEADS * MLA_HEAD_PAD
    return pl.pallas_call(
        _mla_prep_kernel,
        grid=(t // tm,),
        in_specs=[pl.BlockSpec((tm, W_MLA), lambda i: (i, 0)),
                  pl.BlockSpec((tm, 128), lambda i: (i, 0)), pl.BlockSpec((tm, 128), lambda i: (i, 0)),
                  _layer((1, MLA_Q_RANK), l), _layer((MLA_Q_RANK, hp), l),
                  _layer((1, MLA_KV_RANK), l), _layer((MLA_KV_RANK, hp), l),
                  _layer((1, MLA_HEAD_PAD), l), _layer((1, MLA_HEAD_PAD), l)],
        out_specs=[pl.BlockSpec((tm, hp), lambda i: (i, 0)),
                   pl.BlockSpec((tm, hp), lambda i: (i, 0)),
                   pl.BlockSpec((tm, MLA_HEADS * MLA_V), lambda i: (i, 0))],
        out_shape=[jax.ShapeDtypeStruct((t, hp), BF16),
                   jax.ShapeDtypeStruct((t, hp), BF16),
                   jax.ShapeDtypeStruct((t, MLA_HEADS * MLA_V), BF16)],
        compiler_params=_params("arbitrary"),
        name="mla_prep",
    )(p, cos_t, sin_t, qnw, wuq, kvnw, wukv, qhw, khw)


def _flash_kernel(q_ref, k_ref, v_ref, p_ref, o_ref, s_ref):
    tq = FLASH_TQ
    hp = MLA_HEAD_PAD
    row = lax.broadcasted_iota(jnp.int32, (tq, tq), 0)
    col = lax.broadcasted_iota(jnp.int32, (tq, tq), 1)
    heads = range(MLA_HEADS)

    def attend(nb):
        q = [q_ref[:, hp * h:hp * (h + 1)] for h in heads]
        m = [None] * MLA_HEADS
        for j in range(nb):
            for h in heads:
                s = _dot_nt(q[h], k_ref[tq * j:tq * (j + 1), hp * h:hp * (h + 1)])
                if j == nb - 1:
                    s = jnp.where(col <= row, s, -jnp.inf)
                s_ref[h, :, tq * j:tq * (j + 1)] = s
                sm = jnp.maximum(s[:, 0:128], s[:, 128:256])
                m[h] = sm if m[h] is None else jnp.maximum(m[h], sm)
        m = [jnp.max(mh, axis=-1, keepdims=True) for mh in m]
        l = [jnp.zeros((tq, 128), F32) for _ in heads]
        acc = [jnp.zeros((tq, MLA_V), F32) for _ in heads]
        for j in range(nb):
            for h in heads:
                p = jnp.exp(s_ref[h, :, tq * j:tq * (j + 1)] - m[h])
                l[h] = l[h] + (p[:, 0:128] + p[:, 128:256])
                acc[h] = acc[h] + _dot(p.astype(BF16), v_ref[tq * j:tq * (j + 1), MLA_V * h:MLA_V * (h + 1)])
        for h in heads:
            lh = jnp.sum(l[h], axis=-1, keepdims=True)
            gate = p_ref[:, W_MLA - GROUP_W + MLA_V * h:W_MLA - GROUP_W + MLA_V * (h + 1)]
            o_ref[:, MLA_V * h:MLA_V * (h + 1)] = (acc[h] / lh * _silu(gate)).astype(BF16)

    for nb in range(1, s_ref.shape[2] // tq + 1):
        pl.when(pl.program_id(1) == nb - 1)(functools.partial(attend, nb))


def _flash(q, k, v, p, bsz, s):
    nq = s // FLASH_TQ
    hp = MLA_HEADS * MLA_HEAD_PAD
    return pl.pallas_call(
        _flash_kernel,
        grid=(bsz, nq),
        in_specs=[pl.BlockSpec((FLASH_TQ, hp), lambda b, i: (b * nq + i, 0)),
                  pl.BlockSpec((s, hp), lambda b, i: (b, 0)),
                  pl.BlockSpec((s, MLA_HEADS * MLA_V), lambda b, i: (b, 0)),
                  pl.BlockSpec((FLASH_TQ, W_MLA), lambda b, i: (b * nq + i, 0))],
        out_specs=pl.BlockSpec((FLASH_TQ, MLA_HEADS * MLA_V), lambda b, i: (b * nq + i, 0)),
        out_shape=jax.ShapeDtypeStruct((bsz * s, MLA_HEADS * MLA_V), BF16),
        scratch_shapes=[pltpu.VMEM((MLA_HEADS, FLASH_TQ, s), F32)],
        compiler_params=_params("arbitrary", "arbitrary"),
        name="mla_flash",
    )(q, k, v, p)


def _regroup_kernel(wt_ref, ssd_ref, s5_ref, gla_ref, mla_ref):
    offs = [0]
    for wd in IN_WIDTHS:
        offs.append(offs[-1] + wd)
    (z, xbc, dt, s5u, s5g, gq, gk, gv, gg, glr, cq, ckv, kpe, mg) = [
        (a, b) for a, b in zip(offs[:-1], offs[1:])]
    cols = wt_ref.shape[1]

    def put(dst, at, seg):
        dst[at:at + seg[1] - seg[0], :] = wt_ref[seg[0]:seg[1], :].astype(BF16)

    put(ssd_ref, 0, z)
    put(ssd_ref, 512, xbc)
    for h in range(SSD_HEADS):
        row = wt_ref[dt[0] + h:dt[0] + h + 1, :]
        ssd_ref[1280 + 64 * h:1280 + 64 * (h + 1), :] = jnp.broadcast_to(row, (64, cols)).astype(BF16)
    put(s5_ref, 0, s5u)
    put(s5_ref, 512, s5g)
    put(gla_ref, 0, gq)
    put(gla_ref, 256, gk)
    put(gla_ref, 512, gv)
    put(gla_ref, 1024, gg)
    gla_ref[1536:1664, :] = _pad_rows(wt_ref[glr[0]:glr[1], :], 128).astype(BF16)
    put(mla_ref, 0, cq)
    put(mla_ref, 384, ckv)
    half = MLA_ROPE // 2
    mla_ref[512:576, :] = _pad_rows(wt_ref[kpe[0]:kpe[0] + half, :], 64).astype(BF16)
    mla_ref[576:640, :] = _pad_rows(wt_ref[kpe[0] + half:kpe[1], :], 64).astype(BF16)
    put(mla_ref, 640, mg)


def _pad_rows(x, n):
    return jnp.concatenate([x, jnp.zeros((n - x.shape[0], x.shape[1]), x.dtype)], axis=0)


def _regroup_w_in(w_in, tc=256):
    wt = jnp.swapaxes(w_in, 1, 2)
    depth, n, k = wt.shape
    widths = (W_SSD, W_S5, W_GLA, W_MLA)
    return pl.pallas_call(
        _regroup_kernel,
        grid=(depth, k // tc),
        in_specs=[pl.BlockSpec((None, n, tc), lambda l, i: (l, 0, i))],
        out_specs=[pl.BlockSpec((None, w, tc), lambda l, i: (l, 0, i)) for w in widths],
        out_shape=[jax.ShapeDtypeStruct((depth, w, k), BF16) for w in widths],
        compiler_params=_params("arbitrary", "arbitrary"),
        name="regroup_w_in",
    )(wt)


def _pad_last(w, n):
    return jnp.pad(w, [(0, 0)] * (w.ndim - 1) + [(0, n - w.shape[-1])])


def kernel(x, positions, norm_w, w_in, w_out, ssd_conv_w, ssd_conv_b, ssd_dt_bias, ssd_a_log, ssd_d, ssd_norm_w, s5_a_re, s5_a_im, s5_log_dt, s5_b_re, s5_b_im, s5_c_re, s5_c_im, s5_d, s5_glu_w, s5_glu_b, gla_gate_w2, gla_gate_b, gla_norm_w, mla_q_norm_w, mla_w_uq, mla_kv_norm_w, mla_w_ukv, mla_q_head_norm_w, mla_k_head_norm_w):
    bsz, s, d = x.shape
    depth = w_in.shape[0]
    t = bsz * s
    h = x.reshape(t, d)
    pos = positions.reshape(t, 1)
    row = lambda v: v[:, None, :]

    inv_freq = ROPE_THETA ** (-jnp.arange(0, MLA_ROPE, 2, dtype=F32) / MLA_ROPE)
    invf = _spread_rope(jnp.concatenate([inv_freq, inv_freq]))[None, :]
    sign = _spread_rope(jnp.concatenate([-jnp.ones((32,), F32), jnp.ones((32,), F32)]))[None, :]

    w_ssd, w_s5, w_gla, w_mla = _regroup_w_in(w_in)
    w_out_b = w_out.astype(BF16)
    norm_w3 = row(norm_w)

    rep = lambda v: row(jnp.repeat(v, SSD_HEAD_DIM, axis=1))
    ssd_cb, ssd_dtb, ssd_alog, ssd_dsk, ssd_nw = (row(ssd_conv_b), rep(ssd_dt_bias), rep(ssd_a_log),
                                                   rep(ssd_d), row(ssd_norm_w))

    s5_are = s5_a_re.reshape(depth, 1, S5_NS)
    s5_aim = s5_a_im.reshape(depth, 1, S5_NS)
    s5_ldt = row(jnp.repeat(s5_log_dt, S5_STATE, axis=1))
    s5_bre, s5_bim = s5_b_re.transpose(0, 1, 3, 2), s5_b_im.transpose(0, 1, 3, 2)
    s5_cre, s5_cim = s5_c_re.transpose(0, 1, 3, 2), s5_c_im.transpose(0, 1, 3, 2)
    s5_dsk, s5_gw, s5_gb = row(s5_d), s5_glu_w.astype(BF16), row(s5_glu_b)

    gla_w2 = jnp.pad(gla_gate_w2, ((0, 0), (0, 112), (0, 0))).astype(BF16)
    gla_gb, gla_nw = row(gla_gate_b), row(gla_norm_w)

    head_pad = lambda w: jnp.concatenate([w[..., :MLA_NOPE], _spread_rope(w[..., MLA_NOPE:])], axis=-1)
    wuq = head_pad(mla_w_uq.reshape(depth, MLA_Q_RANK, MLA_HEADS, MLA_NOPE + MLA_ROPE))
    wuq = wuq.reshape(depth, MLA_Q_RANK, MLA_HEADS * MLA_HEAD_PAD).astype(BF16)
    wukv = mla_w_ukv.astype(BF16)
    qnw, kvnw = row(mla_q_norm_w), row(mla_kv_norm_w)
    qhw, khw = row(head_pad(mla_q_head_norm_w)), row(head_pad(mla_k_head_norm_w))

    cos_t, sin_t = _rope_tables(pos, invf, sign)
    u = _rmsnorm_bf16(h, norm_w3, 0)
    for l in range(depth):
        p_ssd = _in_proj(u, w_ssd, l, "in_proj_ssd")
        p_s5 = _in_proj(u, w_s5, l, "in_proj_s5")
        p_gla = _in_proj(u, w_gla, l, "in_proj_gla")
        p_mla = _in_proj(u, w_mla, l, "in_proj_mla")

        y_a = _ssd_branch(l, p_ssd, ssd_conv_w, ssd_cb, ssd_dtb, ssd_alog, ssd_dsk, ssd_nw, bsz, s)

        bb, cc, tab = _s5_prep(l, s5_are, s5_aim, s5_ldt, s5_bre, s5_bim, s5_cre, s5_cim)
        y_b = _s5_branch(l, p_s5, bb, cc, tab, s5_dsk, s5_gw, s5_gb, bsz, s)

        y_c = _gla_branch(l, p_gla, gla_w2, gla_gb, gla_nw, bsz, s)

        q, k, v = _mla_prep(l, p_mla, cos_t, sin_t, qnw, wuq, kvnw, wukv, qhw, khw)
        y_d = _flash(q, k, v, p_mla, bsz, s)

        h, u = _out_proj((y_a, y_b, y_c, y_d), w_out_b, h, norm_w3, l, (l + 1) % depth)
    return h.reshape(bsz, s, d)
```

```python
import functools
import math

import jax
import jax.numpy as jnp
from jax import lax
from jax.experimental import pallas as pl
from jax.experimental.pallas import tpu as pltpu

F32 = jnp.float32
BF16 = jnp.bfloat16
NORM_EPS = 1e-6
HI = lax.Precision.HIGHEST

D_MODEL = 2048
GROUP_W = 512
SSD_HEADS = 8
SSD_HEAD_DIM = 64
SSD_STATE = 64
SSD_CHUNK = 128
SSD_CHUNKS_PER_STEP = 2
SSD_XBC = 768
SSD_CONV = 4
S5_GROUPS = 32
S5_CH = 16
S5_STATE = 64
S5_NS = S5_GROUPS * S5_STATE
S5_HALF = S5_NS // 2
S5_TSTEP = 128
S5_SCAN_LANES = 512
GLA_HEADS = 4
GLA_DK = 64
GLA_DV = 128
GLA_CHUNK = 16
GLA_TILE = 128
GLA_TILES_PER_STEP = 4
GLA_GATE_NORM = 16.0
MLA_HEADS = 4
MLA_NOPE = 128
MLA_ROPE = 64
MLA_V = 128
MLA_Q_RANK = 384
MLA_KV_RANK = 128
MLA_HEAD_PAD = 256
ROPE_THETA = 10000.0
FLASH_TQ = 256

IN_WIDTHS = (512, 768, 8, 512, 512, 256, 256, 512, 512, 16, 384, 128, 64, 512)
W_SSD = 512 + 768 + 128
W_S5 = 1024
W_GLA = 256 + 256 + 512 + 512 + 128
W_MLA = 384 + 128 + 128 + 512

VMEM_LIMIT_BYTES = 56 * 1024 * 1024


def _params(*sem):
    return pltpu.CompilerParams(dimension_semantics=sem, vmem_limit_bytes=VMEM_LIMIT_BYTES)


def _layer(shape, l):
    return pl.BlockSpec((None,) + tuple(shape), lambda *_: (l,) + (0,) * len(shape))


def _silu(x):
    return x / (1.0 + jnp.exp(-x))


def _softplus(x):
    return jnp.maximum(x, 0.0) + jnp.log(1.0 + jnp.exp(-jnp.abs(x)))


def _dot(a, b):
    return jnp.dot(a, b, preferred_element_type=F32)


def _dot_nt(a, b):
    return lax.dot_general(a, b, (((1,), (1,)), ((), ())), preferred_element_type=F32)


def _dot_tn(a, b):
    return lax.dot_general(a, b, (((0,), (0,)), ((), ())), preferred_element_type=F32)


def _dot_01(mask, x):
    m = mask.astype(BF16)
    hi = x.astype(BF16)
    rest = x - hi.astype(F32)
    mid = rest.astype(BF16)
    lo = (rest - mid.astype(F32)).astype(BF16)
    return _dot(m, hi) + _dot(m, mid) + _dot(m, lo)


def _dot_r01(x, mask):
    m = mask.astype(BF16)
    hi = x.astype(BF16)
    rest = x - hi.astype(F32)
    mid = rest.astype(BF16)
    lo = (rest - mid.astype(F32)).astype(BF16)
    return _dot(hi, m) + _dot(mid, m) + _dot(lo, m)


def _rms_kernel(x_ref, w_ref, o_ref):
    x = x_ref[...]
    ms = jnp.mean(x * x, axis=-1, keepdims=True)
    o_ref[...] = (x * lax.rsqrt(ms + NORM_EPS) * w_ref[...]).astype(BF16)


def _rmsnorm_bf16(x, w, l, tm=512):
    t, d = x.shape
    return pl.pallas_call(
        _rms_kernel,
        grid=(t // tm,),
        in_specs=[pl.BlockSpec((tm, d), lambda i: (i, 0)), _layer((1, d), l)],
        out_specs=pl.BlockSpec((tm, d), lambda i: (i, 0)),
        out_shape=jax.ShapeDtypeStruct((t, d), BF16),
        compiler_params=_params("arbitrary"),
        name="rmsnorm_in",
    )(x, w)


def _mm_kernel(x_ref, wt_ref, o_ref):
    o_ref[...] = _dot_nt(x_ref[...], wt_ref[...])


def _in_proj(u, wt, l, name, tm=1024):
    t, k = u.shape
    n = wt.shape[1]
    return pl.pallas_call(
        _mm_kernel,
        grid=(t // tm,),
        in_specs=[pl.BlockSpec((tm, k), lambda i: (i, 0)), _layer((n, k), l)],
        out_specs=pl.BlockSpec((tm, n), lambda i: (i, 0)),
        out_shape=jax.ShapeDtypeStruct((t, n), F32),
        compiler_params=_params("arbitrary"),
        name=name,
    )(u, wt)


def _out_proj_kernel(ya_ref, yb_ref, yc_ref, yd_ref, w_ref, h_ref, nw_ref, ho_ref, uo_ref):
    sub = 256
    for r in range(h_ref.shape[0] // sub):
        rows = slice(sub * r, sub * (r + 1))
        acc = h_ref[rows, :]
        for i, y_ref in enumerate((ya_ref, yb_ref, yc_ref, yd_ref)):
            acc = acc + _dot(y_ref[rows, :], w_ref[GROUP_W * i:GROUP_W * (i + 1), :])
        ho_ref[rows, :] = acc
        ms = jnp.mean(acc * acc, axis=-1, keepdims=True)
        uo_ref[rows, :] = (acc * lax.rsqrt(ms + NORM_EPS) * nw_ref[...]).astype(BF16)


def _out_proj(ys, w, h, nw, l, l_next, tm=512):
    t, d = h.shape
    yspec = pl.BlockSpec((tm, GROUP_W), lambda i: (i, 0))
    return pl.pallas_call(
        _out_proj_kernel,
        grid=(t // tm,),
        in_specs=[yspec, yspec, yspec, yspec,
                  _layer((d, d), l),
                  pl.BlockSpec((tm, d), lambda i: (i, 0)),
                  _layer((1, d), l_next)],
        out_specs=[pl.BlockSpec((tm, d), lambda i: (i, 0)), pl.BlockSpec((tm, d), lambda i: (i, 0))],
        out_shape=[jax.ShapeDtypeStruct((t, d), F32), jax.ShapeDtypeStruct((t, d), BF16)],
        compiler_params=_params("arbitrary"),
        name="out_proj",
    )(*ys, w, h, nw)


def _ssd_kernel(p_ref, cw_ref, cb_ref, dtb_ref, alog_ref, d_ref, nw_ref, o_ref, cbuf, st_ref):
    L = SSD_CHUNK

    @pl.when(pl.program_id(1) == 0)
    def _():
        cbuf[0:8, :] = jnp.zeros((8, SSD_XBC), F32)
        st_ref[...] = jnp.zeros_like(st_ref)

    row = lax.broadcasted_iota(jnp.int32, (L, L), 0)
    col = lax.broadcasted_iota(jnp.int32, (L, L), 1)
    causal = col <= row
    expand = (lax.broadcasted_iota(jnp.int32, (128, 512), 1) // SSD_HEAD_DIM
              == lax.broadcasted_iota(jnp.int32, (128, 512), 0))
    lane = lax.broadcasted_iota(jnp.int32, (L, 128), 1)
    rowi = lax.broadcasted_iota(jnp.int32, (L, 128), 0)
    low = lane < 64
    low8 = lax.broadcasted_iota(jnp.int32, (8, 128), 1) < 64
    low64 = lax.broadcasted_iota(jnp.int32, (64, 128), 1) < 64
    keep = [rowi >= (lane % 64) + 64 * jh for jh in range(2)]
    blockdiag = (lax.broadcasted_iota(jnp.int32, (128, 128), 0) // 64
                 == lax.broadcasted_iota(jnp.int32, (128, 128), 1) // 64)

    def both_halves(x, g):
        r = pltpu.roll(x, 64, 1)
        return jnp.where(low, x, r) if g == 0 else jnp.where(low, r, x)

    def chunk(rows):
        z = p_ref[rows, 0:512]
        cbuf[8:8 + L, :] = p_ref[rows, 512:512 + SSD_XBC]
        acc = cb_ref[...] + cbuf[pl.ds(8 - (SSD_CONV - 1), L), :] * cw_ref[0:1, :]
        for k in range(1, SSD_CONV):
            acc = acc + cbuf[pl.ds(8 - (SSD_CONV - 1) + k, L), :] * cw_ref[k:k + 1, :]
        cbuf[0:8, :] = cbuf[L:L + 8, :]
        xbc = _silu(acc)
        xs = xbc[:, 0:512]
        bm = xbc[:, 512:640]
        cm = xbc[:, 640:768]

        dt_c = _softplus(p_ref[rows, 1280:1408] + dtb_ref[...])
        cs_c = _dot_01(causal, dt_c * -jnp.exp(alog_ref[...]))
        dt = _dot_r01(dt_c, expand)
        cs = _dot_r01(cs_c, expand)
        cs_t = jnp.transpose(cs_c)
        cs_last = cs[L - 1:L, :]
        grow = jnp.exp(cs)
        tail = jnp.exp(cs_last - cs)
        total = jnp.exp(cs_last)
        xdt = (xs * dt).astype(BF16)
        bmb = bm.astype(BF16)

        ys = []
        for g in range(2):
            cmask = jnp.where(low if g == 0 else jnp.logical_not(low), cm, 0.0).astype(BF16)
            gdup = [_dot_nt(cmask, jnp.concatenate([bmb[64 * jh:64 * jh + 64, :]] * 2, axis=0)) for jh in range(2)]
            cdup = both_halves(cm, g)
            bdup = both_halves(bm, g)
            for pp in range(2):
                pair = 2 * g + pp
                lanes = slice(128 * pair, 128 * (pair + 1))
                csp = cs[:, lanes]
                xp = xdt[:, lanes]
                r0 = jnp.broadcast_to(cs_t[2 * pair:2 * pair + 1, :], (8, L))
                r1 = jnp.broadcast_to(cs_t[2 * pair + 1:2 * pair + 2, :], (8, L))
                crow = [jnp.where(low8, r0, pltpu.roll(r1, 64, 1))[0:1, :],
                        jnp.where(low8, pltpu.roll(r0, 64, 1), r1)[0:1, :]]
                y = None
                for jh in range(2):
                    lmat = jnp.exp(jnp.where(keep[jh], csp - crow[jh], -jnp.inf))
                    xj = xp[64 * jh:64 * jh + 64, :]
                    zero = jnp.zeros_like(xj)
                    xbd = jnp.concatenate([jnp.where(low64, xj, zero), jnp.where(low64, zero, xj)], axis=0)
                    t = _dot((gdup[jh] * lmat).astype(BF16), xbd)
                    y = t if y is None else y + t
                st = st_ref[pair]
                y = y + _dot_nt((cdup * grow[:, lanes]).astype(BF16), st.astype(BF16))
                upd = _dot_tn(xp, (bdup * tail[:, lanes]).astype(BF16))
                st_ref[pair] = total[:, lanes] * st + jnp.where(blockdiag, upd, 0.0)
                ys.append(y)
        y = jnp.concatenate(ys, axis=1) + d_ref[...] * xs
        y = y * _silu(z)
        ms = jnp.mean(y * y, axis=-1, keepdims=True)
        o_ref[rows, :] = (y * lax.rsqrt(ms + NORM_EPS) * nw_ref[...]).astype(BF16)

    for c in range(p_ref.shape[0] // L):
        chunk(slice(L * c, L * (c + 1)))


def _ssd_branch(l, p, cw, cb, dtb, alog, dsk, nw, bsz, s):
    rows = SSD_CHUNK * SSD_CHUNKS_PER_STEP
    nc = s // rows
    return pl.pallas_call(
        _ssd_kernel,
        grid=(bsz, nc),
        in_specs=[pl.BlockSpec((rows, W_SSD), lambda b, c: (b * nc + c, 0)),
                  _layer((SSD_CONV, SSD_XBC), l), _layer((1, SSD_XBC), l),
                  _layer((1, 128), l), _layer((1, 128), l), _layer((1, 512), l), _layer((1, 512), l)],
        out_specs=pl.BlockSpec((rows, GROUP_W), lambda b, c: (b * nc + c, 0)),
        out_shape=jax.ShapeDtypeStruct((bsz * s, GROUP_W), BF16),
        scratch_shapes=[pltpu.VMEM((SSD_CHUNK + 8, SSD_XBC), F32),
                        pltpu.VMEM((SSD_HEADS // 2, 2 * SSD_HEAD_DIM, 2 * SSD_STATE), F32)],
        compiler_params=_params("arbitrary", "arbitrary"),
        name="ssd_branch",
    )(p, cw, cb, dtb, alog, dsk, nw)


def _s5_prep_kernel(are_ref, aim_ref, ldt_ref, bre_ref, bim_ref, cre_ref, cim_ref, bb_ref, cc_ref, tab_ref):
    are = are_ref[...]
    aim = aim_ref[...]
    delta = jnp.exp(ldt_ref[...])
    mag = jnp.exp(are * delta)
    ar = mag * jnp.cos(aim * delta)
    ai = mag * jnp.sin(aim * delta)
    den = are * are + aim * aim
    coef_re = ((ar - 1.0) * are + ai * aim) / den
    coef_im = (ai * are - (ar - 1.0) * aim) / den

    bb_ref[...] = jnp.zeros_like(bb_ref)
    cc_ref[...] = jnp.zeros_like(cc_ref)
    for g in range(S5_GROUPS):
        kb, gl = divmod(g, 16)
        st = slice(S5_STATE * g, S5_STATE * (g + 1))
        rows = slice(S5_CH * gl, S5_CH * (gl + 1))
        re = slice(S5_STATE * gl, S5_STATE * (gl + 1))
        im = slice(S5_HALF + S5_STATE * gl, S5_HALF + S5_STATE * (gl + 1))
        bre = bre_ref[g]
        bim = bim_ref[g]
        bb_ref[kb, rows, re] = (coef_re[:, st] * bre - coef_im[:, st] * bim).astype(BF16)
        bb_ref[kb, rows, im] = (coef_re[:, st] * bim + coef_im[:, st] * bre).astype(BF16)
        cc_ref[kb, re, rows] = cre_ref[g].astype(BF16)
        cc_ref[kb, im, rows] = (-cim_ref[g]).astype(BF16)

    odd = lax.broadcasted_iota(jnp.int32, (8, S5_HALF), 0) % 2 == 1
    tab_ref[0] = jnp.where(odd, ar[:, S5_HALF:], ar[:, :S5_HALF])
    tab_ref[1] = jnp.where(odd, ai[:, S5_HALF:], ai[:, :S5_HALF])


def _s5_prep(l, are, aim, ldt, bre, bim, cre, cim):
    out3 = lambda shape: pl.BlockSpec(shape, lambda i: (0, 0, 0))
    return pl.pallas_call(
        _s5_prep_kernel,
        grid=(1,),
        in_specs=[_layer((1, S5_NS), l), _layer((1, S5_NS), l), _layer((1, S5_NS), l),
                  _layer((S5_GROUPS, S5_CH, S5_STATE), l), _layer((S5_GROUPS, S5_CH, S5_STATE), l),
                  _layer((S5_GROUPS, S5_STATE, S5_CH), l), _layer((S5_GROUPS, S5_STATE, S5_CH), l)],
        out_specs=[out3((2, 256, 2 * S5_HALF)), out3((2, 2 * S5_HALF, 256)), out3((2, 8, S5_HALF))],
        out_shape=[jax.ShapeDtypeStruct((2, 256, 2 * S5_HALF), BF16),
                   jax.ShapeDtypeStruct((2, 2 * S5_HALF, 256), BF16),
                   jax.ShapeDtypeStruct((2, 8, S5_HALF), F32)],
        compiler_params=_params("arbitrary"),
        name="s5_prep",
    )(are, aim, ldt, bre, bim, cre, cim)


def _s5_kernel(p_ref, bb_ref, cc_ref, tab_ref, d_ref, gw_ref, gb_ref, o_ref, pbuf, hbuf, ybuf, carry):
    nb = p_ref.shape[0]
    rt = S5_TSTEP
    n = nb * rt

    @pl.when(pl.program_id(0) == 0)
    def _():
        carry[...] = jnp.zeros_like(carry)

    def cat(ref, blocks, rows):
        return jnp.concatenate([ref[c, rows, :] for c in blocks], axis=1)

    for b in range(nb):
        for c in range(W_S5 // 128):
            pbuf[c, pl.ds(b, rt, stride=nb), :] = p_ref[b, :, 128 * c:128 * (c + 1)]
    u = cat(pbuf, range(0, 4), slice(None))
    ub = u.astype(BF16)
    nblk = 2 * S5_HALF // 128
    for j in range(2):
        hj = _dot(ub[:, 256 * j:256 * (j + 1)], bb_ref[j])
        for c in range(nblk):
            hbuf[c, pl.ds(j, n, stride=2), :] = hj[:, 128 * c:128 * (c + 1)]

    wb = S5_SCAN_LANES // 128
    for lc in range(S5_HALF // S5_SCAN_LANES):
        re_blocks = list(range(wb * lc, wb * (lc + 1)))
        im_blocks = [c + nblk // 2 for c in re_blocks]
        lanes = slice(S5_SCAN_LANES * lc, S5_SCAN_LANES * (lc + 1))
        ar = tab_ref[0, :, lanes]
        ai = tab_ref[1, :, lanes]

        def body(t, c, re_blocks=re_blocks, im_blocks=im_blocks, ar=ar, ai=ai):
            hr, hi = c
            rows = pl.ds(pl.multiple_of(t * 8, 8), 8)
            hr, hi = (ar * hr - ai * hi + cat(hbuf, re_blocks, rows),
                      ar * hi + ai * hr + cat(hbuf, im_blocks, rows))
            for k in range(wb):
                hbuf[re_blocks[k], rows, :] = hr[:, 128 * k:128 * (k + 1)]
                hbuf[im_blocks[k], rows, :] = hi[:, 128 * k:128 * (k + 1)]
            return hr, hi

        re = slice(S5_SCAN_LANES * lc, S5_SCAN_LANES * (lc + 1))
        im = slice(S5_HALF + re.start, S5_HALF + re.stop)
        hr, hi = lax.fori_loop(0, rt, body, (carry[:, re], carry[:, im]), unroll=4)
        carry[:, re] = hr
        carry[:, im] = hi

    ys = [_dot(cat(hbuf, range(nblk), pl.ds(j, n, stride=2)).astype(BF16), cc_ref[j]) for j in range(2)]
    y = jnp.concatenate(ys, axis=1) + d_ref[...] * u
    y = 0.5 * y * (1.0 + jnp.tanh(math.sqrt(2.0 / math.pi) * (y + 0.044715 * (y * y * y))))
    glu = _dot(y.astype(BF16), gw_ref[...]) + gb_ref[...]
    y = y / (1.0 + jnp.exp(-glu))
    y = y * _silu(cat(pbuf, range(4, 8), slice(None)))
    for c in range(GROUP_W // 128):
        ybuf[c] = y[:, 128 * c:128 * (c + 1)]
    for b in range(nb):
        o_ref[b] = cat(ybuf, range(GROUP_W // 128), pl.ds(b, rt, stride=nb)).astype(BF16)


def _s5_branch(l, p, bb, cc, tab, dsk, gw, gb, bsz, s):
    assert 2 * bsz == 8, "scan rows (batch, state block) must fill the 8 sublanes"
    const = lambda shape: pl.BlockSpec(shape, lambda t: tuple(0 for _ in shape))
    n = bsz * S5_TSTEP
    out = pl.pallas_call(
        _s5_kernel,
        grid=(s // S5_TSTEP,),
        in_specs=[pl.BlockSpec((bsz, S5_TSTEP, W_S5), lambda t: (0, t, 0)),
                  const((2, 256, 2 * S5_HALF)), const((2, 2 * S5_HALF, 256)), const((2, 8, S5_HALF)),
                  _layer((1, GROUP_W), l), _layer((GROUP_W, GROUP_W), l), _layer((1, GROUP_W), l)],
        out_specs=pl.BlockSpec((bsz, S5_TSTEP, GROUP_W), lambda t: (0, t, 0)),
        out_shape=jax.ShapeDtypeStruct((bsz, s, GROUP_W), BF16),
        scratch_shapes=[pltpu.VMEM((W_S5 // 128, n, 128), F32),
                        pltpu.VMEM((2 * S5_HALF // 128, 2 * n, 128), F32),
                        pltpu.VMEM((GROUP_W // 128, n, 128), F32),
                        pltpu.VMEM((8, 2 * S5_HALF), F32)],
        compiler_params=_params("arbitrary"),
        name="s5_branch",
    )(p.reshape(bsz, s, W_S5), bb, cc, tab, dsk, gw, gb)
    return out.reshape(bsz * s, GROUP_W)


def _gla_kernel(p_ref, w2_ref, gb_ref, nw_ref, o_ref, st_ref):
    T = GLA_TILE
    NCH = T // GLA_CHUNK

    @pl.when(pl.program_id(1) == 0)
    def _():
        st_ref[...] = jnp.zeros_like(st_ref)

    row = lax.broadcasted_iota(jnp.int32, (T, T), 0)
    col = lax.broadcasted_iota(jnp.int32, (T, T), 1)
    same = (row // GLA_CHUNK) == (col // GLA_CHUNK)
    intra = jnp.logical_and(same, col <= row)
    wide = (T, NCH * 128)
    blockmask = (lax.broadcasted_iota(jnp.int32, wide, 0) // GLA_CHUNK
                 == lax.broadcasted_iota(jnp.int32, wide, 1) // 128)
    wide_head = (lax.broadcasted_iota(jnp.int32, wide, 1) % 128) // GLA_DK
    lane_head = lax.broadcasted_iota(jnp.int32, (T, 128), 1) // GLA_DK
    zero = jnp.zeros((), BF16)
    scale = GLA_DK ** -0.5
    states = [st_ref[pair] for pair in range(GLA_HEADS // 2)]

    subs = range(p_ref.shape[0] // T)
    pairs = range(GLA_HEADS // 2)
    rows = [slice(T * sub, T * (sub + 1)) for sub in subs]
    xs = [_dot(p_ref[rows[s], 1536:1664].astype(BF16), w2_ref[...]) + gb_ref[...] for s in subs]
    gs = [-_softplus(-x) * (1.0 / GLA_GATE_NORM) for x in xs]
    bs = [_dot_01(intra, g) for g in gs]
    blasts = [_dot_01(same, g) for g in gs]
    q_mid, k_mid, q_dec, k_dec = [], [], [], []
    for s in subs:
        q = p_ref[rows[s], 0:256] * scale
        k = p_ref[rows[s], 256:512]
        half = 0.5 * blasts[s]
        q_mid.append((q * jnp.exp(bs[s] - half)).astype(BF16))
        k_mid.append((k * jnp.exp(half - bs[s])).astype(BF16))
        q_dec.append((q * jnp.exp(bs[s])).astype(BF16))
        k_dec.append((k * jnp.exp(blasts[s] - bs[s])).astype(BF16))

    lanes = [slice(128 * pair, 128 * (pair + 1)) for pair in pairs]
    vs = [[p_ref[rows[s], 512 + GLA_DV * h:512 + GLA_DV * (h + 1)].astype(BF16) for h in range(GLA_HEADS)]
          for s in subs]
    ds = {}
    for s in subs:
        for pair in pairs:
            kcat = jnp.where(blockmask, jnp.tile(k_dec[s][:, lanes[pair]], (1, NCH)), zero)
            ds[s, pair] = sum(_dot_tn(vs[s][2 * pair + hh], jnp.where(wide_head == hh, kcat, zero))
                              for hh in range(2))
    st_all = {}
    for s in subs:
        for pair in pairs:
            st = states[pair]
            sts = []
            for c in range(NCH):
                sts.append(st)
                dec = jnp.exp(blasts[s][GLA_CHUNK * c:GLA_CHUNK * c + 1, lanes[pair]])
                st = dec * st + ds[s, pair][:, 128 * c:128 * (c + 1)]
            states[pair] = st
            st_all[s, pair] = jnp.concatenate(sts, axis=1).astype(BF16)
    for pair in pairs:
        st_ref[pair] = states[pair]

    for s in subs:
        for pair in pairs:
            qcat = jnp.where(blockmask, jnp.tile(q_dec[s][:, lanes[pair]], (1, NCH)), zero)
            for hh in range(2):
                h = 2 * pair + hh
                attn = _dot_nt(jnp.where(lane_head == hh, q_mid[s][:, lanes[pair]], zero), k_mid[s][:, lanes[pair]])
                o = _dot(jnp.where(intra, attn, 0.0).astype(BF16), vs[s][h])
                o = o + _dot_nt(jnp.where(wide_head == hh, qcat, zero), st_all[s, pair])
                ms = jnp.mean(o * o, axis=-1, keepdims=True)
                o = o * lax.rsqrt(ms + NORM_EPS) * nw_ref[...]
                gate = p_ref[rows[s], 1024 + GLA_DV * h:1024 + GLA_DV * (h + 1)]
                o_ref[rows[s], GLA_DV * h:GLA_DV * (h + 1)] = (o * _silu(gate)).astype(BF16)


def _gla_branch(l, p, w2, gb, nw, bsz, s):
    rows = GLA_TILE * GLA_TILES_PER_STEP
    nt = s // rows
    return pl.pallas_call(
        _gla_kernel,
        grid=(bsz, nt),
        in_specs=[pl.BlockSpec((rows, W_GLA), lambda b, t: (b * nt + t, 0)),
                  _layer((128, 256), l), _layer((1, 256), l), _layer((1, GLA_DV), l)],
        out_specs=pl.BlockSpec((rows, GROUP_W), lambda b, t: (b * nt + t, 0)),
        out_shape=jax.ShapeDtypeStruct((bsz * s, GROUP_W), BF16),
        scratch_shapes=[pltpu.VMEM((GLA_HEADS // 2, GLA_DV, 2 * GLA_DK), F32)],
        compiler_params=_params("arbitrary", "arbitrary"),
        name="gla_branch",
    )(p, w2, gb, nw)


def _rope_swap(t):
    return pltpu.roll(t, 64, 1)


def _spread_rope(w):
    half = MLA_ROPE // 2
    zeros = jnp.zeros(w.shape[:-1] + (64 - half,), w.dtype)
    return jnp.concatenate([w[..., :half], zeros, w[..., half:], zeros], axis=-1)


def _rope_kernel(pos_ref, invf_ref, sign_ref, cos_ref, sin_ref):
    ang = pos_ref[...].astype(F32) * invf_ref[...]
    cos_ref[...] = jnp.cos(ang)
    sin_ref[...] = jnp.sin(ang) * sign_ref[...]


def _rope_tables(pos, invf, sign, tm=512):
    t = pos.shape[0]
    const = pl.BlockSpec((1, 128), lambda i: (0, 0))
    tile = pl.BlockSpec((tm, 128), lambda i: (i, 0))
    return pl.pallas_call(
        _rope_kernel,
        grid=(t // tm,),
        in_specs=[pl.BlockSpec((tm, 1), lambda i: (i, 0)), const, const],
        out_specs=[tile, tile],
        out_shape=[jax.ShapeDtypeStruct((t, 128), F32)] * 2,
        compiler_params=_params("arbitrary"),
        name="rope_tables",
    )(pos, invf, sign)


def _mla_prep_kernel(p_ref, cos_ref, sin_ref, qnw_ref, wuq_ref, kvnw_ref, wukv_ref,
                     qhw_ref, khw_ref, q_ref, k_ref, v_ref):
    inv_d = 1.0 / (MLA_NOPE + MLA_ROPE)
    scale = (MLA_NOPE + MLA_ROPE) ** -0.5
    qhw = qhw_ref[...]
    khw = khw_ref[...]
    sub = 128
    for r0 in range(0, p_ref.shape[0], sub):
        rows = slice(r0, r0 + sub)
        cq = p_ref[rows, 0:384]
        ckv = p_ref[rows, 384:512]
        kpe = p_ref[rows, 512:640]
        ms = jnp.mean(cq * cq, axis=-1, keepdims=True)
        qn = (cq * lax.rsqrt(ms + NORM_EPS) * qnw_ref[...]).astype(BF16)
        ms = jnp.mean(ckv * ckv, axis=-1, keepdims=True)
        kvn = (ckv * lax.rsqrt(ms + NORM_EPS) * kvnw_ref[...]).astype(BF16)
        cos_t = cos_ref[rows, :]
        sin_t = sin_ref[rows, :]
        kpe_ss = jnp.sum(kpe * kpe, axis=-1, keepdims=True)
        heads = range(MLA_HEADS)
        hp = MLA_HEAD_PAD
        qs = [_dot(qn, wuq_ref[:, hp * h:hp * (h + 1)]) for h in heads]
        kvs = [_dot(kvn, wukv_ref[:, hp * h:hp * (h + 1)]) for h in heads]
        q_ss = [jnp.sum(q * q, axis=-1, keepdims=True) for q in qs]
        k_ss = [jnp.sum(kv[:, 0:128] * kv[:, 0:128], axis=-1, keepdims=True) for kv in kvs]
        q_rs = [lax.rsqrt(ss * inv_d + NORM_EPS) * scale for ss in q_ss]
        k_rs = [lax.rsqrt((ss + kpe_ss) * inv_d + NORM_EPS) for ss in k_ss]
        q_rot = [qs[h][:, 128:256] * q_rs[h] * qhw[:, 128:256] for h in heads]
        k_rot = [kpe * k_rs[h] * khw[:, 128:256] for h in heads]
        q_swap = [_rope_swap(t) for t in q_rot]
        k_swap = [_rope_swap(t) for t in k_rot]
        for h in heads:
            lo = hp * h
            q_ref[rows, lo:lo + 128] = (qs[h][:, 0:128] * q_rs[h] * qhw[:, 0:128]).astype(BF16)
            q_ref[rows, lo + 128:lo + 256] = (q_rot[h] * cos_t + q_swap[h] * sin_t).astype(BF16)
            k_ref[rows, lo:lo + 128] = (kvs[h][:, 0:128] * k_rs[h] * khw[:, 0:128]).astype(BF16)
            k_ref[rows, lo + 128:lo + 256] = (k_rot[h] * cos_t + k_swap[h] * sin_t).astype(BF16)
            v_ref[rows, MLA_V * h:MLA_V * (h + 1)] = kvs[h][:, 128:256].astype(BF16)


def _mla_prep(l, p, cos_t, sin_t, qnw, wuq, kvnw, wukv, qhw, khw, tm=512):
    t = p.shape[0]
    hp = MLA_HEADS * MLA_HEAD_PAD
    return pl.pallas_call(
        _mla_prep_kernel,
        grid=(t // tm,),
        in_specs=[pl.BlockSpec((tm, W_MLA), lambda i: (i, 0)),
                  pl.BlockSpec((tm, 128), lambda i: (i, 0)), pl.BlockSpec((tm, 128), lambda i: (i, 0)),
                  _layer((1, MLA_Q_RANK), l), _layer((MLA_Q_RANK, hp), l),
                  _layer((1, MLA_KV_RANK), l), _layer((MLA_KV_RANK, hp), l),
                  _layer((1, MLA_HEAD_PAD), l), _layer((1, MLA_HEAD_PAD), l)],
        out_specs=[pl.BlockSpec((tm, hp), lambda i: (i, 0)),
                   pl.BlockSpec((tm, hp), lambda i: (i, 0)),
                   pl.BlockSpec((tm, MLA_HEADS * MLA_V), lambda i: (i, 0))],
        out_shape=[jax.ShapeDtypeStruct((t, hp), BF16),
                   jax.ShapeDtypeStruct((t, hp), BF16),
                   jax.ShapeDtypeStruct((t, MLA_HEADS * MLA_V), BF16)],
        compiler_params=_params("arbitrary"),
        name="mla_prep",
    )(p, cos_t, sin_t, qnw, wuq, kvnw, wukv, qhw, khw)


def _flash_kernel(q_ref, k_ref, v_ref, p_ref, o_ref, s_ref):
    tq = FLASH_TQ
    hp = MLA_HEAD_PAD
    row = lax.broadcasted_iota(jnp.int32, (tq, tq), 0)
    col = lax.broadcasted_iota(jnp.int32, (tq, tq), 1)
    heads = range(MLA_HEADS)

    def attend(nb):
        q = [q_ref[:, hp * h:hp * (h + 1)] for h in heads]
        m = [None] * MLA_HEADS
        for j in range(nb):
            for h in heads:
                s = _dot_nt(q[h], k_ref[tq * j:tq * (j + 1), hp * h:hp * (h + 1)])
                if j == nb - 1:
                    s = jnp.where(col <= row, s, -jnp.inf)
                s_ref[h, :, tq * j:tq * (j + 1)] = s
                sm = jnp.maximum(s[:, 0:128], s[:, 128:256])
                m[h] = sm if m[h] is None else jnp.maximum(m[h], sm)
        m = [jnp.max(mh, axis=-1, keepdims=True) for mh in m]
        l = [jnp.zeros((tq, 128), F32) for _ in heads]
        acc = [jnp.zeros((tq, MLA_V), F32) for _ in heads]
        for j in range(nb):
            for h in heads:
                p = jnp.exp(s_ref[h, :, tq * j:tq * (j + 1)] - m[h])
                l[h] = l[h] + (p[:, 0:128] + p[:, 128:256])
                acc[h] = acc[h] + _dot(p.astype(BF16), v_ref[tq * j:tq * (j + 1), MLA_V * h:MLA_V * (h + 1)])
        for h in heads:
            lh = jnp.sum(l[h], axis=-1, keepdims=True)
            gate = p_ref[:, W_MLA - GROUP_W + MLA_V * h:W_MLA - GROUP_W + MLA_V * (h + 1)]
            o_ref[:, MLA_V * h:MLA_V * (h + 1)] = (acc[h] / lh * _silu(gate)).astype(BF16)

    for nb in range(1, s_ref.shape[2] // tq + 1):
        pl.when(pl.program_id(1) == nb - 1)(functools.partial(attend, nb))


def _flash(q, k, v, p, bsz, s):
    nq = s // FLASH_TQ
    hp = MLA_HEADS * MLA_HEAD_PAD
    return pl.pallas_call(
        _flash_kernel,
        grid=(bsz, nq),
        in_specs=[pl.BlockSpec((FLASH_TQ, hp), lambda b, i: (b * nq + i, 0)),
                  pl.BlockSpec((s, hp), lambda b, i: (b, 0)),
                  pl.BlockSpec((s, MLA_HEADS * MLA_V), lambda b, i: (b, 0)),
                  pl.BlockSpec((FLASH_TQ, W_MLA), lambda b, i: (b * nq + i, 0))],
        out_specs=pl.BlockSpec((FLASH_TQ, MLA_HEADS * MLA_V), lambda b, i: (b * nq + i, 0)),
        out_shape=jax.ShapeDtypeStruct((bsz * s, MLA_HEADS * MLA_V), BF16),
        scratch_shapes=[pltpu.VMEM((MLA_HEADS, FLASH_TQ, s), F32)],
        compiler_params=_params("arbitrary", "arbitrary"),
        name="mla_flash",
    )(q, k, v, p)


def _regroup_kernel(wt_ref, ssd_ref, s5_ref, gla_ref, mla_ref):
    offs = [0]
    for wd in IN_WIDTHS:
        offs.append(offs[-1] + wd)
    (z, xbc, dt, s5u, s5g, gq, gk, gv, gg, glr, cq, ckv, kpe, mg) = [
        (a, b) for a, b in zip(offs[:-1], offs[1:])]

    def put(dst, at, seg):
        dst[at:at + seg[1] - seg[0], :] = wt_ref[seg[0]:seg[1], :].astype(BF16)

    put(ssd_ref, 0, z)
    put(ssd_ref, 512, xbc)
    ssd_ref[1280:1408, :] = _pad_rows(wt_ref[dt[0]:dt[1], :], 128).astype(BF16)
    put(s5_ref, 0, s5u)
    put(s5_ref, 512, s5g)
    put(gla_ref, 0, gq)
    put(gla_ref, 256, gk)
    put(gla_ref, 512, gv)
    put(gla_ref, 1024, gg)
    gla_ref[1536:1664, :] = _pad_rows(wt_ref[glr[0]:glr[1], :], 128).astype(BF16)
    put(mla_ref, 0, cq)
    put(mla_ref, 384, ckv)
    half = MLA_ROPE // 2
    mla_ref[512:576, :] = _pad_rows(wt_ref[kpe[0]:kpe[0] + half, :], 64).astype(BF16)
    mla_ref[576:640, :] = _pad_rows(wt_ref[kpe[0] + half:kpe[1], :], 64).astype(BF16)
    put(mla_ref, 640, mg)


def _pad_rows(x, n):
    return jnp.concatenate([x, jnp.zeros((n - x.shape[0], x.shape[1]), x.dtype)], axis=0)


def _regroup_w_in(w_in, tc=256):
    wt = jnp.swapaxes(w_in, 1, 2)
    depth, n, k = wt.shape
    widths = (W_SSD, W_S5, W_GLA, W_MLA)
    return pl.pallas_call(
        _regroup_kernel,
        grid=(depth, k // tc),
        in_specs=[pl.BlockSpec((None, n, tc), lambda l, i: (l, 0, i))],
        out_specs=[pl.BlockSpec((None, w, tc), lambda l, i: (l, 0, i)) for w in widths],
        out_shape=[jax.ShapeDtypeStruct((depth, w, k), BF16) for w in widths],
        compiler_params=_params("arbitrary", "arbitrary"),
        name="regroup_w_in",
    )(wt)


def _pad_last(w, n):
    return jnp.pad(w, [(0, 0)] * (w.ndim - 1) + [(0, n - w.shape[-1])])


def kernel(x, positions, norm_w, w_in, w_out, ssd_conv_w, ssd_conv_b, ssd_dt_bias, ssd_a_log, ssd_d, ssd_norm_w, s5_a_re, s5_a_im, s5_log_dt, s5_b_re, s5_b_im, s5_c_re, s5_c_im, s5_d, s5_glu_w, s5_glu_b, gla_gate_w2, gla_gate_b, gla_norm_w, mla_q_norm_w, mla_w_uq, mla_kv_norm_w, mla_w_ukv, mla_q_head_norm_w, mla_k_head_norm_w):
    bsz, s, d = x.shape
    depth = w_in.shape[0]
    t = bsz * s
    h = x.reshape(t, d)
    pos = positions.reshape(t, 1)
    row = lambda v: v[:, None, :]

    inv_freq = ROPE_THETA ** (-jnp.arange(0, MLA_ROPE, 2, dtype=F32) / MLA_ROPE)
    invf = _spread_rope(jnp.concatenate([inv_freq, inv_freq]))[None, :]
    sign = _spread_rope(jnp.concatenate([-jnp.ones((32,), F32), jnp.ones((32,), F32)]))[None, :]

    w_ssd, w_s5, w_gla, w_mla = _regroup_w_in(w_in)
    w_out_b = w_out.astype(BF16)
    norm_w3 = row(norm_w)

    ssd_cb, ssd_nw = row(ssd_conv_b), row(ssd_norm_w)
    ssd_dtb, ssd_alog = row(_pad_last(ssd_dt_bias, 128)), row(_pad_last(ssd_a_log, 128))
    ssd_dsk = row(jnp.repeat(ssd_d, SSD_HEAD_DIM, axis=1))

    s5_are = s5_a_re.reshape(depth, 1, S5_NS)
    s5_aim = s5_a_im.reshape(depth, 1, S5_NS)
    s5_ldt = row(jnp.repeat(s5_log_dt, S5_STATE, axis=1))
    s5_bre, s5_bim = s5_b_re.transpose(0, 1, 3, 2), s5_b_im.transpose(0, 1, 3, 2)
    s5_cre, s5_cim = s5_c_re.transpose(0, 1, 3, 2), s5_c_im.transpose(0, 1, 3, 2)
    s5_dsk, s5_gw, s5_gb = row(s5_d), s5_glu_w.astype(BF16), row(s5_glu_b)

    gla_w2 = jnp.pad(gla_gate_w2, ((0, 0), (0, 112), (0, 0))).astype(BF16)
    gla_gb, gla_nw = row(gla_gate_b), row(gla_norm_w)

    head_pad = lambda w: jnp.concatenate([w[..., :MLA_NOPE], _spread_rope(w[..., MLA_NOPE:])], axis=-1)
    wuq = head_pad(mla_w_uq.reshape(depth, MLA_Q_RANK, MLA_HEADS, MLA_NOPE + MLA_ROPE))
    wuq = wuq.reshape(depth, MLA_Q_RANK, MLA_HEADS * MLA_HEAD_PAD).astype(BF16)
    wukv = mla_w_ukv.astype(BF16)
    qnw, kvnw = row(mla_q_norm_w), row(mla_kv_norm_w)
    qhw, khw = row(head_pad(mla_q_head_norm_w)), row(head_pad(mla_k_head_norm_w))

    cos_t, sin_t = _rope_tables(pos, invf, sign)
    u = _rmsnorm_bf16(h, norm_w3, 0)
    for l in range(depth):
        p_ssd = _in_proj(u, w_ssd, l, "in_proj_ssd")
        p_s5 = _in_proj(u, w_s5, l, "in_proj_s5")
        p_gla = _in_proj(u, w_gla, l, "in_proj_gla")
        p_mla = _in_proj(u, w_mla, l, "in_proj_mla")

        y_a = _ssd_branch(l, p_ssd, ssd_conv_w, ssd_cb, ssd_dtb, ssd_alog, ssd_dsk, ssd_nw, bsz, s)

        bb, cc, tab = _s5_prep(l, s5_are, s5_aim, s5_ldt, s5_bre, s5_bim, s5_cre, s5_cim)
        y_b = _s5_branch(l, p_s5, bb, cc, tab, s5_dsk, s5_gw, s5_gb, bsz, s)

        y_c = _gla_branch(l, p_gla, gla_w2, gla_gb, gla_nw, bsz, s)

        q, k, v = _mla_prep(l, p_mla, cos_t, sin_t, qnw, wuq, kvnw, wukv, qhw, khw)
        y_d = _flash(q, k, v, p_mla, bsz, s)

        h, u = _out_proj((y_a, y_b, y_c, y_d), w_out_b, h, norm_w3, l, (l + 1) % depth)
    return h.reshape(bsz, s, d)
```

```python
import functools
import math

import jax
import jax.numpy as jnp
from jax import lax
from jax.experimental import pallas as pl
from jax.experimental.pallas import tpu as pltpu

F32 = jnp.float32
BF16 = jnp.bfloat16
NORM_EPS = 1e-6
HI = lax.Precision.HIGHEST

D_MODEL = 2048
GROUP_W = 512
SSD_HEADS = 8
SSD_HEAD_DIM = 64
SSD_STATE = 64
SSD_CHUNK = 128
SSD_CHUNKS_PER_STEP = 2
SSD_XBC = 768
SSD_CONV = 4
S5_GROUPS = 32
S5_CH = 16
S5_STATE = 64
S5_NS = S5_GROUPS * S5_STATE
S5_HALF = S5_NS // 2
S5_TSTEP = 128
S5_SCAN_LANES = 512
GLA_HEADS = 4
GLA_DK = 64
GLA_DV = 128
GLA_CHUNK = 16
GLA_TILE = 128
GLA_TILES_PER_STEP = 4
GLA_GATE_NORM = 16.0
MLA_HEADS = 4
MLA_NOPE = 128
MLA_ROPE = 64
MLA_V = 128
MLA_Q_RANK = 384
MLA_KV_RANK = 128
MLA_HEAD_PAD = 256
ROPE_THETA = 10000.0
FLASH_TQ = 256

IN_WIDTHS = (512, 768, 8, 512, 512, 256, 256, 512, 512, 16, 384, 128, 64, 512)
W_SSD = 512 + 768 + 128
W_S5 = 1024
W_GLA = 256 + 256 + 512 + 512 + 128
W_MLA = 512 + 384 + 128 + 128

VMEM_LIMIT_BYTES = 56 * 1024 * 1024


def _params(*sem):
    return pltpu.CompilerParams(dimension_semantics=sem, vmem_limit_bytes=VMEM_LIMIT_BYTES)


def _layer(shape, l):
    return pl.BlockSpec((None,) + tuple(shape), lambda *_: (l,) + (0,) * len(shape))


def _silu(x):
    return x / (1.0 + jnp.exp(-x))


def _softplus(x):
    return jnp.maximum(x, 0.0) + jnp.log(1.0 + jnp.exp(-jnp.abs(x)))


def _dot(a, b):
    return jnp.dot(a, b, preferred_element_type=F32)


def _dot_nt(a, b):
    return lax.dot_general(a, b, (((1,), (1,)), ((), ())), preferred_element_type=F32)


def _dot_tn(a, b):
    return lax.dot_general(a, b, (((0,), (0,)), ((), ())), preferred_element_type=F32)


def _dot_01(mask, x):
    m = mask.astype(BF16)
    hi = x.astype(BF16)
    rest = x - hi.astype(F32)
    mid = rest.astype(BF16)
    lo = (rest - mid.astype(F32)).astype(BF16)
    return _dot(m, hi) + _dot(m, mid) + _dot(m, lo)


def _dot_r01(x, mask):
    m = mask.astype(BF16)
    hi = x.astype(BF16)
    rest = x - hi.astype(F32)
    mid = rest.astype(BF16)
    lo = (rest - mid.astype(F32)).astype(BF16)
    return _dot(hi, m) + _dot(mid, m) + _dot(lo, m)


def _rms_kernel(x_ref, w_ref, o_ref):
    x = x_ref[...]
    ms = jnp.mean(x * x, axis=-1, keepdims=True)
    o_ref[...] = (x * lax.rsqrt(ms + NORM_EPS) * w_ref[...]).astype(BF16)


def _rmsnorm_bf16(x, w, l, tm=512):
    t, d = x.shape
    return pl.pallas_call(
        _rms_kernel,
        grid=(t // tm,),
        in_specs=[pl.BlockSpec((tm, d), lambda i: (i, 0)), _layer((1, d), l)],
        out_specs=pl.BlockSpec((tm, d), lambda i: (i, 0)),
        out_shape=jax.ShapeDtypeStruct((t, d), BF16),
        compiler_params=_params("arbitrary"),
        name="rmsnorm_in",
    )(x, w)


def _mm_kernel(x_ref, wt_ref, o_ref):
    o_ref[...] = _dot_nt(x_ref[...], wt_ref[...])


def _in_proj(u, wt, l, name, tm=1024):
    t, k = u.shape
    n = wt.shape[1]
    return pl.pallas_call(
        _mm_kernel,
        grid=(t // tm,),
        in_specs=[pl.BlockSpec((tm, k), lambda i: (i, 0)), _layer((n, k), l)],
        out_specs=pl.BlockSpec((tm, n), lambda i: (i, 0)),
        out_shape=jax.ShapeDtypeStruct((t, n), F32),
        compiler_params=_params("arbitrary"),
        name=name,
    )(u, wt)


def _out_proj_kernel(ya_ref, yb_ref, yc_ref, yd_ref, w_ref, h_ref, nw_ref, ho_ref, uo_ref):
    sub = 256
    for r in range(h_ref.shape[0] // sub):
        rows = slice(sub * r, sub * (r + 1))
        acc = h_ref[rows, :]
        for i, y_ref in enumerate((ya_ref, yb_ref, yc_ref, yd_ref)):
            acc = acc + _dot(y_ref[rows, :], w_ref[GROUP_W * i:GROUP_W * (i + 1), :])
        ho_ref[rows, :] = acc
        ms = jnp.mean(acc * acc, axis=-1, keepdims=True)
        uo_ref[rows, :] = (acc * lax.rsqrt(ms + NORM_EPS) * nw_ref[...]).astype(BF16)


def _out_proj(ys, w, h, nw, l, l_next, tm=512):
    t, d = h.shape
    yspec = pl.BlockSpec((tm, GROUP_W), lambda i: (i, 0))
    return pl.pallas_call(
        _out_proj_kernel,
        grid=(t // tm,),
        in_specs=[yspec, yspec, yspec, yspec,
                  _layer((d, d), l),
                  pl.BlockSpec((tm, d), lambda i: (i, 0)),
                  _layer((1, d), l_next)],
        out_specs=[pl.BlockSpec((tm, d), lambda i: (i, 0)), pl.BlockSpec((tm, d), lambda i: (i, 0))],
        out_shape=[jax.ShapeDtypeStruct((t, d), F32), jax.ShapeDtypeStruct((t, d), BF16)],
        compiler_params=_params("arbitrary"),
        name="out_proj",
    )(*ys, w, h, nw)


def _ssd_kernel(p_ref, cw_ref, cb_ref, dtb_ref, alog_ref, d_ref, nw_ref, o_ref, cbuf, st_ref):
    L = SSD_CHUNK

    @pl.when(pl.program_id(1) == 0)
    def _():
        cbuf[0:8, :] = jnp.zeros((8, SSD_XBC), F32)
        st_ref[...] = jnp.zeros_like(st_ref)

    row = lax.broadcasted_iota(jnp.int32, (L, L), 0)
    col = lax.broadcasted_iota(jnp.int32, (L, L), 1)
    causal = col <= row
    expand = (lax.broadcasted_iota(jnp.int32, (128, 512), 1) // SSD_HEAD_DIM
              == lax.broadcasted_iota(jnp.int32, (128, 512), 0))
    lane = lax.broadcasted_iota(jnp.int32, (L, 128), 1)
    rowi = lax.broadcasted_iota(jnp.int32, (L, 128), 0)
    low = lane < 64
    low8 = lax.broadcasted_iota(jnp.int32, (8, 128), 1) < 64
    low64 = lax.broadcasted_iota(jnp.int32, (64, 128), 1) < 64
    keep = [rowi >= (lane % 64) + 64 * jh for jh in range(2)]
    blockdiag = (lax.broadcasted_iota(jnp.int32, (128, 128), 0) // 64
                 == lax.broadcasted_iota(jnp.int32, (128, 128), 1) // 64)

    def both_halves(x, g):
        r = pltpu.roll(x, 64, 1)
        return jnp.where(low, x, r) if g == 0 else jnp.where(low, r, x)

    def chunk(rows):
        z = p_ref[rows, 0:512]
        cbuf[8:8 + L, :] = p_ref[rows, 512:512 + SSD_XBC]
        acc = cb_ref[...] + cbuf[pl.ds(8 - (SSD_CONV - 1), L), :] * cw_ref[0:1, :]
        for k in range(1, SSD_CONV):
            acc = acc + cbuf[pl.ds(8 - (SSD_CONV - 1) + k, L), :] * cw_ref[k:k + 1, :]
        cbuf[0:8, :] = cbuf[L:L + 8, :]
        xbc = _silu(acc)
        xs = xbc[:, 0:512]
        bm = xbc[:, 512:640]
        cm = xbc[:, 640:768]

        dt_c = _softplus(p_ref[rows, 1280:1408] + dtb_ref[...])
        cs_c = _dot_01(causal, dt_c * -jnp.exp(alog_ref[...]))
        dt = _dot_r01(dt_c, expand)
        cs = _dot_r01(cs_c, expand)
        cs_t = jnp.transpose(cs_c)
        cs_last = cs[L - 1:L, :]
        grow = jnp.exp(cs)
        tail = jnp.exp(cs_last - cs)
        total = jnp.exp(cs_last)
        xdt = (xs * dt).astype(BF16)
        bmb = bm.astype(BF16)

        ys = []
        for g in range(2):
            cmask = jnp.where(low if g == 0 else jnp.logical_not(low), cm, 0.0).astype(BF16)
            gdup = [_dot_nt(cmask, jnp.concatenate([bmb[64 * jh:64 * jh + 64, :]] * 2, axis=0)) for jh in range(2)]
            cdup = both_halves(cm, g)
            bdup = both_halves(bm, g)
            for pp in range(2):
                pair = 2 * g + pp
                lanes = slice(128 * pair, 128 * (pair + 1))
                csp = cs[:, lanes]
                xp = xdt[:, lanes]
                r0 = jnp.broadcast_to(cs_t[2 * pair:2 * pair + 1, :], (8, L))
                r1 = jnp.broadcast_to(cs_t[2 * pair + 1:2 * pair + 2, :], (8, L))
                crow = [jnp.where(low8, r0, pltpu.roll(r1, 64, 1))[0:1, :],
                        jnp.where(low8, pltpu.roll(r0, 64, 1), r1)[0:1, :]]
                y = None
                for jh in range(2):
                    lmat = jnp.exp(jnp.where(keep[jh], csp - crow[jh], -jnp.inf))
                    xj = xp[64 * jh:64 * jh + 64, :]
                    zero = jnp.zeros_like(xj)
                    xbd = jnp.concatenate([jnp.where(low64, xj, zero), jnp.where(low64, zero, xj)], axis=0)
                    t = _dot((gdup[jh] * lmat).astype(BF16), xbd)
                    y = t if y is None else y + t
                st = st_ref[pair]
                y = y + _dot_nt((cdup * grow[:, lanes]).astype(BF16), st.astype(BF16))
                upd = _dot_tn(xp, (bdup * tail[:, lanes]).astype(BF16))
                st_ref[pair] = total[:, lanes] * st + jnp.where(blockdiag, upd, 0.0)
                ys.append(y)
        y = jnp.concatenate(ys, axis=1) + d_ref[...] * xs
        y = y * _silu(z)
        ms = jnp.mean(y * y, axis=-1, keepdims=True)
        o_ref[rows, :] = (y * lax.rsqrt(ms + NORM_EPS) * nw_ref[...]).astype(BF16)

    for c in range(p_ref.shape[0] // L):
        chunk(slice(L * c, L * (c + 1)))


def _ssd_branch(l, p, cw, cb, dtb, alog, dsk, nw, bsz, s):
    rows = SSD_CHUNK * SSD_CHUNKS_PER_STEP
    nc = s // rows
    return pl.pallas_call(
        _ssd_kernel,
        grid=(bsz, nc),
        in_specs=[pl.BlockSpec((rows, W_SSD), lambda b, c: (b * nc + c, 0)),
                  _layer((SSD_CONV, SSD_XBC), l), _layer((1, SSD_XBC), l),
                  _layer((1, 128), l), _layer((1, 128), l), _layer((1, 512), l), _layer((1, 512), l)],
        out_specs=pl.BlockSpec((rows, GROUP_W), lambda b, c: (b * nc + c, 0)),
        out_shape=jax.ShapeDtypeStruct((bsz * s, GROUP_W), BF16),
        scratch_shapes=[pltpu.VMEM((SSD_CHUNK + 8, SSD_XBC), F32),
                        pltpu.VMEM((SSD_HEADS // 2, 2 * SSD_HEAD_DIM, 2 * SSD_STATE), F32)],
        compiler_params=_params("arbitrary", "arbitrary"),
        name="ssd_branch",
    )(p, cw, cb, dtb, alog, dsk, nw)


def _s5_prep_kernel(are_ref, aim_ref, ldt_ref, bre_ref, bim_ref, cre_ref, cim_ref, bb_ref, cc_ref, tab_ref):
    are = are_ref[...]
    aim = aim_ref[...]
    delta = jnp.exp(ldt_ref[...])
    mag = jnp.exp(are * delta)
    ar = mag * jnp.cos(aim * delta)
    ai = mag * jnp.sin(aim * delta)
    den = are * are + aim * aim
    coef_re = ((ar - 1.0) * are + ai * aim) / den
    coef_im = (ai * are - (ar - 1.0) * aim) / den

    bb_ref[...] = jnp.zeros_like(bb_ref)
    cc_ref[...] = jnp.zeros_like(cc_ref)
    for g in range(S5_GROUPS):
        kb, gl = divmod(g, 16)
        st = slice(S5_STATE * g, S5_STATE * (g + 1))
        rows = slice(S5_CH * gl, S5_CH * (gl + 1))
        re = slice(S5_STATE * gl, S5_STATE * (gl + 1))
        im = slice(S5_HALF + S5_STATE * gl, S5_HALF + S5_STATE * (gl + 1))
        bre = bre_ref[g]
        bim = bim_ref[g]
        bb_ref[kb, rows, re] = (coef_re[:, st] * bre - coef_im[:, st] * bim).astype(BF16)
        bb_ref[kb, rows, im] = (coef_re[:, st] * bim + coef_im[:, st] * bre).astype(BF16)
        cc_ref[kb, re, rows] = cre_ref[g].astype(BF16)
        cc_ref[kb, im, rows] = (-cim_ref[g]).astype(BF16)

    odd = lax.broadcasted_iota(jnp.int32, (8, S5_HALF), 0) % 2 == 1
    tab_ref[0] = jnp.where(odd, ar[:, S5_HALF:], ar[:, :S5_HALF])
    tab_ref[1] = jnp.where(odd, ai[:, S5_HALF:], ai[:, :S5_HALF])


def _s5_prep(l, are, aim, ldt, bre, bim, cre, cim):
    out3 = lambda shape: pl.BlockSpec(shape, lambda i: (0, 0, 0))
    return pl.pallas_call(
        _s5_prep_kernel,
        grid=(1,),
        in_specs=[_layer((1, S5_NS), l), _layer((1, S5_NS), l), _layer((1, S5_NS), l),
                  _layer((S5_GROUPS, S5_CH, S5_STATE), l), _layer((S5_GROUPS, S5_CH, S5_STATE), l),
                  _layer((S5_GROUPS, S5_STATE, S5_CH), l), _layer((S5_GROUPS, S5_STATE, S5_CH), l)],
        out_specs=[out3((2, 256, 2 * S5_HALF)), out3((2, 2 * S5_HALF, 256)), out3((2, 8, S5_HALF))],
        out_shape=[jax.ShapeDtypeStruct((2, 256, 2 * S5_HALF), BF16),
                   jax.ShapeDtypeStruct((2, 2 * S5_HALF, 256), BF16),
                   jax.ShapeDtypeStruct((2, 8, S5_HALF), F32)],
        compiler_params=_params("arbitrary"),
        name="s5_prep",
    )(are, aim, ldt, bre, bim, cre, cim)


def _s5_kernel(p_ref, bb_ref, cc_ref, tab_ref, d_ref, gw_ref, gb_ref, o_ref, pbuf, hbuf, ybuf, carry):
    nb = p_ref.shape[0]
    rt = S5_TSTEP
    n = nb * rt

    @pl.when(pl.program_id(0) == 0)
    def _():
        carry[...] = jnp.zeros_like(carry)

    def cat(ref, blocks, rows):
        return jnp.concatenate([ref[c, rows, :] for c in blocks], axis=1)

    for b in range(nb):
        for c in range(W_S5 // 128):
            pbuf[c, pl.ds(b, rt, stride=nb), :] = p_ref[b, :, 128 * c:128 * (c + 1)]
    u = cat(pbuf, range(0, 4), slice(None))
    ub = u.astype(BF16)
    nblk = 2 * S5_HALF // 128
    for j in range(2):
        hj = _dot(ub[:, 256 * j:256 * (j + 1)], bb_ref[j])
        for c in range(nblk):
            hbuf[c, pl.ds(j, n, stride=2), :] = hj[:, 128 * c:128 * (c + 1)]

    wb = S5_SCAN_LANES // 128
    for lc in range(S5_HALF // S5_SCAN_LANES):
        re_blocks = list(range(wb * lc, wb * (lc + 1)))
        im_blocks = [c + nblk // 2 for c in re_blocks]
        lanes = slice(S5_SCAN_LANES * lc, S5_SCAN_LANES * (lc + 1))
        ar = tab_ref[0, :, lanes]
        ai = tab_ref[1, :, lanes]

        def body(t, c, re_blocks=re_blocks, im_blocks=im_blocks, ar=ar, ai=ai):
            hr, hi = c
            rows = pl.ds(pl.multiple_of(t * 8, 8), 8)
            hr, hi = (ar * hr - ai * hi + cat(hbuf, re_blocks, rows),
                      ar * hi + ai * hr + cat(hbuf, im_blocks, rows))
            for k in range(wb):
                hbuf[re_blocks[k], rows, :] = hr[:, 128 * k:128 * (k + 1)]
                hbuf[im_blocks[k], rows, :] = hi[:, 128 * k:128 * (k + 1)]
            return hr, hi

        re = slice(S5_SCAN_LANES * lc, S5_SCAN_LANES * (lc + 1))
        im = slice(S5_HALF + re.start, S5_HALF + re.stop)
        hr, hi = lax.fori_loop(0, rt, body, (carry[:, re], carry[:, im]), unroll=4)
        carry[:, re] = hr
        carry[:, im] = hi

    ys = [_dot(cat(hbuf, range(nblk), pl.ds(j, n, stride=2)).astype(BF16), cc_ref[j]) for j in range(2)]
    y = jnp.concatenate(ys, axis=1) + d_ref[...] * u
    y = 0.5 * y * (1.0 + jnp.tanh(math.sqrt(2.0 / math.pi) * (y + 0.044715 * (y * y * y))))
    glu = _dot(y.astype(BF16), gw_ref[...]) + gb_ref[...]
    y = y / (1.0 + jnp.exp(-glu))
    y = y * _silu(cat(pbuf, range(4, 8), slice(None)))
    for c in range(GROUP_W // 128):
        ybuf[c] = y[:, 128 * c:128 * (c + 1)]
    for b in range(nb):
        o_ref[b] = cat(ybuf, range(GROUP_W // 128), pl.ds(b, rt, stride=nb)).astype(BF16)


def _s5_branch(l, p, bb, cc, tab, dsk, gw, gb, bsz, s):
    assert 2 * bsz == 8, "scan rows (batch, state block) must fill the 8 sublanes"
    const = lambda shape: pl.BlockSpec(shape, lambda t: tuple(0 for _ in shape))
    n = bsz * S5_TSTEP
    out = pl.pallas_call(
        _s5_kernel,
        grid=(s // S5_TSTEP,),
        in_specs=[pl.BlockSpec((bsz, S5_TSTEP, W_S5), lambda t: (0, t, 0)),
                  const((2, 256, 2 * S5_HALF)), const((2, 2 * S5_HALF, 256)), const((2, 8, S5_HALF)),
                  _layer((1, GROUP_W), l), _layer((GROUP_W, GROUP_W), l), _layer((1, GROUP_W), l)],
        out_specs=pl.BlockSpec((bsz, S5_TSTEP, GROUP_W), lambda t: (0, t, 0)),
        out_shape=jax.ShapeDtypeStruct((bsz, s, GROUP_W), BF16),
        scratch_shapes=[pltpu.VMEM((W_S5 // 128, n, 128), F32),
                        pltpu.VMEM((2 * S5_HALF // 128, 2 * n, 128), F32),
                        pltpu.VMEM((GROUP_W // 128, n, 128), F32),
                        pltpu.VMEM((8, 2 * S5_HALF), F32)],
        compiler_params=_params("arbitrary"),
        name="s5_branch",
    )(p.reshape(bsz, s, W_S5), bb, cc, tab, dsk, gw, gb)
    return out.reshape(bsz * s, GROUP_W)


def _gla_kernel(p_ref, w2_ref, gb_ref, nw_ref, o_ref, st_ref):
    T = GLA_TILE
    NCH = T // GLA_CHUNK

    @pl.when(pl.program_id(1) == 0)
    def _():
        st_ref[...] = jnp.zeros_like(st_ref)

    row = lax.broadcasted_iota(jnp.int32, (T, T), 0)
    col = lax.broadcasted_iota(jnp.int32, (T, T), 1)
    same = (row // GLA_CHUNK) == (col // GLA_CHUNK)
    intra = jnp.logical_and(same, col <= row)
    wide = (T, NCH * 128)
    blockmask = (lax.broadcasted_iota(jnp.int32, wide, 0) // GLA_CHUNK
                 == lax.broadcasted_iota(jnp.int32, wide, 1) // 128)
    wide_head = (lax.broadcasted_iota(jnp.int32, wide, 1) % 128) // GLA_DK
    lane_head = lax.broadcasted_iota(jnp.int32, (T, 128), 1) // GLA_DK
    zero = jnp.zeros((), BF16)
    scale = GLA_DK ** -0.5
    states = [st_ref[pair] for pair in range(GLA_HEADS // 2)]

    subs = range(p_ref.shape[0] // T)
    pairs = range(GLA_HEADS // 2)
    rows = [slice(T * sub, T * (sub + 1)) for sub in subs]
    xs = [_dot(p_ref[rows[s], 1536:1664].astype(BF16), w2_ref[...]) + gb_ref[...] for s in subs]
    gs = [-_softplus(-x) * (1.0 / GLA_GATE_NORM) for x in xs]
    bs = [_dot_01(intra, g) for g in gs]
    blasts = [_dot_01(same, g) for g in gs]
    q_mid, k_mid, q_dec, k_dec = [], [], [], []
    for s in subs:
        q = p_ref[rows[s], 0:256] * scale
        k = p_ref[rows[s], 256:512]
        half = 0.5 * blasts[s]
        q_mid.append((q * jnp.exp(bs[s] - half)).astype(BF16))
        k_mid.append((k * jnp.exp(half - bs[s])).astype(BF16))
        q_dec.append((q * jnp.exp(bs[s])).astype(BF16))
        k_dec.append((k * jnp.exp(blasts[s] - bs[s])).astype(BF16))

    lanes = [slice(128 * pair, 128 * (pair + 1)) for pair in pairs]
    vs = [[p_ref[rows[s], 512 + GLA_DV * h:512 + GLA_DV * (h + 1)].astype(BF16) for h in range(GLA_HEADS)]
          for s in subs]
    ds = {}
    for s in subs:
        for pair in pairs:
            kcat = jnp.where(blockmask, jnp.tile(k_dec[s][:, lanes[pair]], (1, NCH)), zero)
            ds[s, pair] = sum(_dot_tn(vs[s][2 * pair + hh], jnp.where(wide_head == hh, kcat, zero))
                              for hh in range(2))
    st_all = {}
    for s in subs:
        for pair in pairs:
            st = states[pair]
            sts = []
            for c in range(NCH):
                sts.append(st)
                dec = jnp.exp(blasts[s][GLA_CHUNK * c:GLA_CHUNK * c + 1, lanes[pair]])
                st = dec * st + ds[s, pair][:, 128 * c:128 * (c + 1)]
            states[pair] = st
            st_all[s, pair] = jnp.concatenate(sts, axis=1).astype(BF16)
    for pair in pairs:
        st_ref[pair] = states[pair]

    for s in subs:
        for pair in pairs:
            qcat = jnp.where(blockmask, jnp.tile(q_dec[s][:, lanes[pair]], (1, NCH)), zero)
            for hh in range(2):
                h = 2 * pair + hh
                attn = _dot_nt(jnp.where(lane_head == hh, q_mid[s][:, lanes[pair]], zero), k_mid[s][:, lanes[pair]])
                o = _dot(jnp.where(intra, attn, 0.0).astype(BF16), vs[s][h])
                o = o + _dot_nt(jnp.where(wide_head == hh, qcat, zero), st_all[s, pair])
                ms = jnp.mean(o * o, axis=-1, keepdims=True)
                o = o * lax.rsqrt(ms + NORM_EPS) * nw_ref[...]
                gate = p_ref[rows[s], 1024 + GLA_DV * h:1024 + GLA_DV * (h + 1)]
                o_ref[rows[s], GLA_DV * h:GLA_DV * (h + 1)] = (o * _silu(gate)).astype(BF16)


def _gla_branch(l, p, w2, gb, nw, bsz, s):
    rows = GLA_TILE * GLA_TILES_PER_STEP
    nt = s // rows
    return pl.pallas_call(
        _gla_kernel,
        grid=(bsz, nt),
        in_specs=[pl.BlockSpec((rows, W_GLA), lambda b, t: (b * nt + t, 0)),
                  _layer((128, 256), l), _layer((1, 256), l), _layer((1, GLA_DV), l)],
        out_specs=pl.BlockSpec((rows, GROUP_W), lambda b, t: (b * nt + t, 0)),
        out_shape=jax.ShapeDtypeStruct((bsz * s, GROUP_W), BF16),
        scratch_shapes=[pltpu.VMEM((GLA_HEADS // 2, GLA_DV, 2 * GLA_DK), F32)],
        compiler_params=_params("arbitrary", "arbitrary"),
        name="gla_branch",
    )(p, w2, gb, nw)


def _rope_swap(t):
    return pltpu.roll(t, 64, 1)


def _spread_rope(w):
    half = MLA_ROPE // 2
    zeros = jnp.zeros(w.shape[:-1] + (64 - half,), w.dtype)
    return jnp.concatenate([w[..., :half], zeros, w[..., half:], zeros], axis=-1)


def _rope_kernel(pos_ref, invf_ref, sign_ref, cos_ref, sin_ref):
    ang = pos_ref[...].astype(F32) * invf_ref[...]
    cos_ref[...] = jnp.cos(ang)
    sin_ref[...] = jnp.sin(ang) * sign_ref[...]


def _rope_tables(pos, invf, sign, tm=512):
    t = pos.shape[0]
    const = pl.BlockSpec((1, 128), lambda i: (0, 0))
    tile = pl.BlockSpec((tm, 128), lambda i: (i, 0))
    return pl.pallas_call(
        _rope_kernel,
        grid=(t // tm,),
        in_specs=[pl.BlockSpec((tm, 1), lambda i: (i, 0)), const, const],
        out_specs=[tile, tile],
        out_shape=[jax.ShapeDtypeStruct((t, 128), F32)] * 2,
        compiler_params=_params("arbitrary"),
        name="rope_tables",
    )(pos, invf, sign)


def _mla_prep_kernel(p_ref, cos_ref, sin_ref, qnw_ref, wuq_ref, kvnw_ref, wukv_ref,
                     qhw_ref, khw_ref, q_ref, k_ref, vt_ref):
    inv_d = 1.0 / (MLA_NOPE + MLA_ROPE)
    scale = (MLA_NOPE + MLA_ROPE) ** -0.5 * math.log2(math.e)
    qhw = qhw_ref[...]
    khw = khw_ref[...]
    sub = 128
    for r0 in range(0, p_ref.shape[0], sub):
        rows = slice(r0, r0 + sub)
        cq = p_ref[rows, 512:896]
        ckv = p_ref[rows, 896:1024]
        kpe = p_ref[rows, 1024:1152]
        ms = jnp.mean(cq * cq, axis=-1, keepdims=True)
        qn = (cq * lax.rsqrt(ms + NORM_EPS) * qnw_ref[...]).astype(BF16)
        ms = jnp.mean(ckv * ckv, axis=-1, keepdims=True)
        kvn = (ckv * lax.rsqrt(ms + NORM_EPS) * kvnw_ref[...]).astype(BF16)
        cos_t = cos_ref[rows, :]
        sin_t = sin_ref[rows, :]
        kpe_ss = jnp.sum(kpe * kpe, axis=-1, keepdims=True)
        heads = range(MLA_HEADS)
        hp = MLA_HEAD_PAD
        qs = [_dot(qn, wuq_ref[:, hp * h:hp * (h + 1)]) for h in heads]
        kvs = [_dot(kvn, wukv_ref[:, hp * h:hp * (h + 1)]) for h in heads]
        q_ss = [jnp.sum(q * q, axis=-1, keepdims=True) for q in qs]
        k_ss = [jnp.sum(kv[:, 0:128] * kv[:, 0:128], axis=-1, keepdims=True) for kv in kvs]
        q_rs = [lax.rsqrt(ss * inv_d + NORM_EPS) * scale for ss in q_ss]
        k_rs = [lax.rsqrt((ss + kpe_ss) * inv_d + NORM_EPS) for ss in k_ss]
        q_rot = [qs[h][:, 128:256] * q_rs[h] * qhw[:, 128:256] for h in heads]
        k_rot = [kpe * k_rs[h] * khw[:, 128:256] for h in heads]
        q_swap = [_rope_swap(t) for t in q_rot]
        k_swap = [_rope_swap(t) for t in k_rot]
        for h in heads:
            lo = hp * h
            q_ref[rows, lo:lo + 128] = (qs[h][:, 0:128] * q_rs[h] * qhw[:, 0:128]).astype(BF16)
            q_ref[rows, lo + 128:lo + 256] = (q_rot[h] * cos_t + q_swap[h] * sin_t).astype(BF16)
            k_ref[rows, lo:lo + 128] = (kvs[h][:, 0:128] * k_rs[h] * khw[:, 0:128]).astype(BF16)
            k_ref[rows, lo + 128:lo + 256] = (k_rot[h] * cos_t + k_swap[h] * sin_t).astype(BF16)
            vt_ref[MLA_V * h:MLA_V * (h + 1), rows] = jnp.transpose(kvs[h][:, 128:256]).astype(BF16)


def _mla_prep(l, p, cos_t, sin_t, qnw, wuq, kvnw, wukv, qhw, khw, tm=512):
    t = p.shape[0]
    hp = MLA_HEADS * MLA_HEAD_PAD
    return pl.pallas_call(
        _mla_prep_kernel,
        grid=(t // tm,),
        in_specs=[pl.BlockSpec((tm, W_MLA), lambda i: (i, 0)),
                  pl.BlockSpec((tm, 128), lambda i: (i, 0)), pl.BlockSpec((tm, 128), lambda i: (i, 0)),
                  _layer((1, MLA_Q_RANK), l), _layer((MLA_Q_RANK, hp), l),
                  _layer((1, MLA_KV_RANK), l), _layer((MLA_KV_RANK, hp), l),
                  _layer((1, MLA_HEAD_PAD), l), _layer((1, MLA_HEAD_PAD), l)],
        out_specs=[pl.BlockSpec((tm, hp), lambda i: (i, 0)),
                   pl.BlockSpec((tm, hp), lambda i: (i, 0)),
                   pl.BlockSpec((MLA_HEADS * MLA_V, tm), lambda i: (0, i))],
        out_shape=[jax.ShapeDtypeStruct((t, hp), BF16),
                   jax.ShapeDtypeStruct((t, hp), BF16),
                   jax.ShapeDtypeStruct((MLA_HEADS * MLA_V, t), BF16)],
        compiler_params=_params("arbitrary"),
        name="mla_prep",
    )(p, cos_t, sin_t, qnw, wuq, kvnw, wukv, qhw, khw)


def _flash_kernel(q_ref, k_ref, vt_ref, g_ref, o_ref, s_ref):
    tq = FLASH_TQ
    hp = MLA_HEAD_PAD
    keep = (lax.broadcasted_iota(jnp.int32, (tq, tq), 0)
            <= lax.broadcasted_iota(jnp.int32, (tq, tq), 1))
    heads = range(MLA_HEADS)

    def attend(nb):
        n = nb * tq
        scores = [_dot_nt(k_ref[0:n, hp * h:hp * (h + 1)], q_ref[:, hp * h:hp * (h + 1)]) for h in heads]
        m = []
        for h in heads:
            blocks = [scores[h][tq * j:tq * (j + 1), :] for j in range(nb)]
            blocks[-1] = jnp.where(keep, blocks[-1], -jnp.inf)
            top = blocks[0]
            for j in range(nb):
                s_ref[h, tq * j:tq * (j + 1), :] = blocks[j]
                top = jnp.maximum(top, blocks[j])
            m.append(jnp.max(top, axis=0, keepdims=True))
        pt = [jnp.exp2(s_ref[h, 0:n, :] - m[h]) for h in heads]
        l = [jnp.sum(p, axis=0, keepdims=True) for p in pt]
        acc = [_dot(vt_ref[MLA_V * h:MLA_V * (h + 1), 0:n], pt[h].astype(BF16)) for h in heads]
        for h in heads:
            o = jnp.transpose(acc[h] / l[h])
            o_ref[:, MLA_V * h:MLA_V * (h + 1)] = (o * _silu(g_ref[:, MLA_V * h:MLA_V * (h + 1)])).astype(BF16)

    for nb in range(1, s_ref.shape[1] // tq + 1):
        pl.when(pl.program_id(1) == nb - 1)(functools.partial(attend, nb))


def _flash(q, k, vt, p, bsz, s):
    nq = s // FLASH_TQ
    hp = MLA_HEADS * MLA_HEAD_PAD
    hv = MLA_HEADS * MLA_V
    return pl.pallas_call(
        _flash_kernel,
        grid=(bsz, nq),
        in_specs=[pl.BlockSpec((FLASH_TQ, hp), lambda b, i: (b * nq + i, 0)),
                  pl.BlockSpec((s, hp), lambda b, i: (b, 0)),
                  pl.BlockSpec((hv, s), lambda b, i: (0, b)),
                  pl.BlockSpec((FLASH_TQ, GROUP_W), lambda b, i: (b * nq + i, 0))],
        out_specs=pl.BlockSpec((FLASH_TQ, hv), lambda b, i: (b * nq + i, 0)),
        out_shape=jax.ShapeDtypeStruct((bsz * s, hv), BF16),
        scratch_shapes=[pltpu.VMEM((MLA_HEADS, s, FLASH_TQ), F32)],
        compiler_params=_params("arbitrary", "arbitrary"),
        name="mla_flash",
    )(q, k, vt, p)


def _regroup_kernel(wt_ref, ssd_ref, s5_ref, gla_ref, mla_ref):
    offs = [0]
    for wd in IN_WIDTHS:
        offs.append(offs[-1] + wd)
    (z, xbc, dt, s5u, s5g, gq, gk, gv, gg, glr, cq, ckv, kpe, mg) = [
        (a, b) for a, b in zip(offs[:-1], offs[1:])]

    def put(dst, at, seg):
        dst[at:at + seg[1] - seg[0], :] = wt_ref[seg[0]:seg[1], :].astype(BF16)

    put(ssd_ref, 0, z)
    put(ssd_ref, 512, xbc)
    ssd_ref[1280:1408, :] = _pad_rows(wt_ref[dt[0]:dt[1], :], 128).astype(BF16)
    put(s5_ref, 0, s5u)
    put(s5_ref, 512, s5g)
    put(gla_ref, 0, gq)
    put(gla_ref, 256, gk)
    put(gla_ref, 512, gv)
    put(gla_ref, 1024, gg)
    gla_ref[1536:1664, :] = _pad_rows(wt_ref[glr[0]:glr[1], :], 128).astype(BF16)
    put(mla_ref, 0, mg)
    put(mla_ref, 512, cq)
    put(mla_ref, 896, ckv)
    half = MLA_ROPE // 2
    mla_ref[1024:1088, :] = _pad_rows(wt_ref[kpe[0]:kpe[0] + half, :], 64).astype(BF16)
    mla_ref[1088:1152, :] = _pad_rows(wt_ref[kpe[0] + half:kpe[1], :], 64).astype(BF16)


def _pad_rows(x, n):
    return jnp.concatenate([x, jnp.zeros((n - x.shape[0], x.shape[1]), x.dtype)], axis=0)


def _regroup_w_in(w_in, tc=256):
    wt = jnp.swapaxes(w_in, 1, 2)
    depth, n, k = wt.shape
    widths = (W_SSD, W_S5, W_GLA, W_MLA)
    return pl.pallas_call(
        _regroup_kernel,
        grid=(depth, k // tc),
        in_specs=[pl.BlockSpec((None, n, tc), lambda l, i: (l, 0, i))],
        out_specs=[pl.BlockSpec((None, w, tc), lambda l, i: (l, 0, i)) for w in widths],
        out_shape=[jax.ShapeDtypeStruct((depth, w, k), BF16) for w in widths],
        compiler_params=_params("arbitrary", "arbitrary"),
        name="regroup_w_in",
    )(wt)


def _pad_last(w, n):
    return jnp.pad(w, [(0, 0)] * (w.ndim - 1) + [(0, n - w.shape[-1])])


def kernel(x, positions, norm_w, w_in, w_out, ssd_conv_w, ssd_conv_b, ssd_dt_bias, ssd_a_log, ssd_d, ssd_norm_w, s5_a_re, s5_a_im, s5_log_dt, s5_b_re, s5_b_im, s5_c_re, s5_c_im, s5_d, s5_glu_w, s5_glu_b, gla_gate_w2, gla_gate_b, gla_norm_w, mla_q_norm_w, mla_w_uq, mla_kv_norm_w, mla_w_ukv, mla_q_head_norm_w, mla_k_head_norm_w):
    bsz, s, d = x.shape
    depth = w_in.shape[0]
    t = bsz * s
    h = x.reshape(t, d)
    pos = positions.reshape(t, 1)
    row = lambda v: v[:, None, :]

    inv_freq = ROPE_THETA ** (-jnp.arange(0, MLA_ROPE, 2, dtype=F32) / MLA_ROPE)
    invf = _spread_rope(jnp.concatenate([inv_freq, inv_freq]))[None, :]
    sign = _spread_rope(jnp.concatenate([-jnp.ones((32,), F32), jnp.ones((32,), F32)]))[None, :]

    w_ssd, w_s5, w_gla, w_mla = _regroup_w_in(w_in)
    w_out_b = w_out.astype(BF16)
    norm_w3 = row(norm_w)

    ssd_cb, ssd_nw = row(ssd_conv_b), row(ssd_norm_w)
    ssd_dtb, ssd_alog = row(_pad_last(ssd_dt_bias, 128)), row(_pad_last(ssd_a_log, 128))
    ssd_dsk = row(jnp.repeat(ssd_d, SSD_HEAD_DIM, axis=1))

    s5_are = s5_a_re.reshape(depth, 1, S5_NS)
    s5_aim = s5_a_im.reshape(depth, 1, S5_NS)
    s5_ldt = row(jnp.repeat(s5_log_dt, S5_STATE, axis=1))
    s5_bre, s5_bim = s5_b_re.transpose(0, 1, 3, 2), s5_b_im.transpose(0, 1, 3, 2)
    s5_cre, s5_cim = s5_c_re.transpose(0, 1, 3, 2), s5_c_im.transpose(0, 1, 3, 2)
    s5_dsk, s5_gw, s5_gb = row(s5_d), s5_glu_w.astype(BF16), row(s5_glu_b)

    gla_w2 = jnp.pad(gla_gate_w2, ((0, 0), (0, 112), (0, 0))).astype(BF16)
    gla_gb, gla_nw = row(gla_gate_b), row(gla_norm_w)

    head_pad = lambda w: jnp.concatenate([w[..., :MLA_NOPE], _spread_rope(w[..., MLA_NOPE:])], axis=-1)
    wuq = head_pad(mla_w_uq.reshape(depth, MLA_Q_RANK, MLA_HEADS, MLA_NOPE + MLA_ROPE))
    wuq = wuq.reshape(depth, MLA_Q_RANK, MLA_HEADS * MLA_HEAD_PAD).astype(BF16)
    wukv = mla_w_ukv.astype(BF16)
    qnw, kvnw = row(mla_q_norm_w), row(mla_kv_norm_w)
    qhw, khw = row(head_pad(mla_q_head_norm_w)), row(head_pad(mla_k_head_norm_w))

    cos_t, sin_t = _rope_tables(pos, invf, sign)
    u = _rmsnorm_bf16(h, norm_w3, 0)
    for l in range(depth):
        p_ssd = _in_proj(u, w_ssd, l, "in_proj_ssd")
        p_s5 = _in_proj(u, w_s5, l, "in_proj_s5")
        p_gla = _in_proj(u, w_gla, l, "in_proj_gla")
        p_mla = _in_proj(u, w_mla, l, "in_proj_mla")

        y_a = _ssd_branch(l, p_ssd, ssd_conv_w, ssd_cb, ssd_dtb, ssd_alog, ssd_dsk, ssd_nw, bsz, s)

        bb, cc, tab = _s5_prep(l, s5_are, s5_aim, s5_ldt, s5_bre, s5_bim, s5_cre, s5_cim)
        y_b = _s5_branch(l, p_s5, bb, cc, tab, s5_dsk, s5_gw, s5_gb, bsz, s)

        y_c = _gla_branch(l, p_gla, gla_w2, gla_gb, gla_nw, bsz, s)

        q, k, vt = _mla_prep(l, p_mla, cos_t, sin_t, qnw, wuq, kvnw, wukv, qhw, khw)
        y_d = _flash(q, k, vt, p_mla, bsz, s)

        h, u = _out_proj((y_a, y_b, y_c, y_d), w_out_b, h, norm_w3, l, (l + 1) % depth)
    return h.reshape(bsz, s, d)
```

```python
import functools
import math

import jax
import jax.numpy as jnp
from jax import lax
from jax.experimental import pallas as pl
from jax.experimental.pallas import tpu as pltpu

F32 = jnp.float32
BF16 = jnp.bfloat16
NORM_EPS = 1e-6
HI = lax.Precision.HIGHEST

D_MODEL = 2048
GROUP_W = 512
SSD_HEADS = 8
SSD_HEAD_DIM = 64
SSD_STATE = 64
SSD_CHUNK = 128
SSD_CHUNKS_PER_STEP = 2
SSD_XBC = 768
SSD_CONV = 4
S5_GROUPS = 32
S5_CH = 16
S5_STATE = 64
S5_NS = S5_GROUPS * S5_STATE
S5_HALF = S5_NS // 2
S5_TSTEP = 128
S5_SCAN_LANES = 512
GLA_HEADS = 4
GLA_DK = 64
GLA_DV = 128
GLA_CHUNK = 16
GLA_TILE = 128
GLA_TILES_PER_STEP = 4
GLA_GATE_NORM = 16.0
MLA_HEADS = 4
MLA_NOPE = 128
MLA_ROPE = 64
MLA_V = 128
MLA_Q_RANK = 384
MLA_KV_RANK = 128
MLA_HEAD_PAD = 256
ROPE_THETA = 10000.0
FLASH_TQ = 256

IN_WIDTHS = (512, 768, 8, 512, 512, 256, 256, 512, 512, 16, 384, 128, 64, 512)
W_SSD = 512 + 768
W_S5 = 1024
W_GLA = 256 + 256 + 512 + 512
W_MLA = 512 + 384 + 128 + 128
SHARED_TILE = W_MLA // 128 - 1
DT_LANE = 32
GLR_LANE = 96

VMEM_LIMIT_BYTES = 56 * 1024 * 1024


def _params(*sem):
    return pltpu.CompilerParams(dimension_semantics=sem, vmem_limit_bytes=VMEM_LIMIT_BYTES)


def _layer(shape, l):
    return pl.BlockSpec((None,) + tuple(shape), lambda *_: (l,) + (0,) * len(shape))


def _silu(x):
    return x / (1.0 + jnp.exp(-x))


def _softplus(x):
    return jnp.maximum(x, 0.0) + jnp.log(1.0 + jnp.exp(-jnp.abs(x)))


def _dot(a, b):
    return jnp.dot(a, b, preferred_element_type=F32)


def _dot_nt(a, b):
    return lax.dot_general(a, b, (((1,), (1,)), ((), ())), preferred_element_type=F32)


def _dot_tn(a, b):
    return lax.dot_general(a, b, (((0,), (0,)), ((), ())), preferred_element_type=F32)


def _dot_01(mask, x):
    m = mask.astype(BF16)
    hi = x.astype(BF16)
    rest = x - hi.astype(F32)
    mid = rest.astype(BF16)
    lo = (rest - mid.astype(F32)).astype(BF16)
    return _dot(m, hi) + _dot(m, mid) + _dot(m, lo)


def _dot_r01(x, mask):
    m = mask.astype(BF16)
    hi = x.astype(BF16)
    rest = x - hi.astype(F32)
    mid = rest.astype(BF16)
    lo = (rest - mid.astype(F32)).astype(BF16)
    return _dot(hi, m) + _dot(mid, m) + _dot(lo, m)


def _rms_kernel(x_ref, w_ref, o_ref):
    x = x_ref[...]
    ms = jnp.mean(x * x, axis=-1, keepdims=True)
    o_ref[...] = (x * lax.rsqrt(ms + NORM_EPS) * w_ref[...]).astype(BF16)


def _rmsnorm_bf16(x, w, l, tm=512):
    t, d = x.shape
    return pl.pallas_call(
        _rms_kernel,
        grid=(t // tm,),
        in_specs=[pl.BlockSpec((tm, d), lambda i: (i, 0)), _layer((1, d), l)],
        out_specs=pl.BlockSpec((tm, d), lambda i: (i, 0)),
        out_shape=jax.ShapeDtypeStruct((t, d), BF16),
        compiler_params=_params("arbitrary"),
        name="rmsnorm_in",
    )(x, w)


def _mm_kernel(x_ref, wt_ref, o_ref):
    o_ref[...] = _dot_nt(x_ref[...], wt_ref[...])


def _in_proj(u, wt, l, name, tm=1024):
    t, k = u.shape
    n = wt.shape[1]
    return pl.pallas_call(
        _mm_kernel,
        grid=(t // tm,),
        in_specs=[pl.BlockSpec((tm, k), lambda i: (i, 0)), _layer((n, k), l)],
        out_specs=pl.BlockSpec((tm, n), lambda i: (i, 0)),
        out_shape=jax.ShapeDtypeStruct((t, n), F32),
        compiler_params=_params("arbitrary"),
        name=name,
    )(u, wt)


def _out_proj_kernel(ya_ref, yb_ref, yc_ref, yd_ref, w_ref, h_ref, nw_ref, ho_ref, uo_ref):
    sub = 256
    for r in range(h_ref.shape[0] // sub):
        rows = slice(sub * r, sub * (r + 1))
        acc = h_ref[rows, :]
        for i, y_ref in enumerate((ya_ref, yb_ref, yc_ref, yd_ref)):
            acc = acc + _dot(y_ref[rows, :], w_ref[GROUP_W * i:GROUP_W * (i + 1), :])
        ho_ref[rows, :] = acc
        ms = jnp.mean(acc * acc, axis=-1, keepdims=True)
        uo_ref[rows, :] = (acc * lax.rsqrt(ms + NORM_EPS) * nw_ref[...]).astype(BF16)


def _out_proj(ys, w, h, nw, l, l_next, tm=512):
    t, d = h.shape
    yspec = pl.BlockSpec((tm, GROUP_W), lambda i: (i, 0))
    return pl.pallas_call(
        _out_proj_kernel,
        grid=(t // tm,),
        in_specs=[yspec, yspec, yspec, yspec,
                  _layer((d, d), l),
                  pl.BlockSpec((tm, d), lambda i: (i, 0)),
                  _layer((1, d), l_next)],
        out_specs=[pl.BlockSpec((tm, d), lambda i: (i, 0)), pl.BlockSpec((tm, d), lambda i: (i, 0))],
        out_shape=[jax.ShapeDtypeStruct((t, d), F32), jax.ShapeDtypeStruct((t, d), BF16)],
        compiler_params=_params("arbitrary"),
        name="out_proj",
    )(*ys, w, h, nw)


def _ssd_kernel(p_ref, sh_ref, cw_ref, cb_ref, dtb_ref, alog_ref, d_ref, nw_ref, o_ref, cbuf, st_ref):
    L = SSD_CHUNK

    @pl.when(pl.program_id(1) == 0)
    def _():
        cbuf[0:8, :] = jnp.zeros((8, SSD_XBC), F32)
        st_ref[...] = jnp.zeros_like(st_ref)

    row = lax.broadcasted_iota(jnp.int32, (L, L), 0)
    col = lax.broadcasted_iota(jnp.int32, (L, L), 1)
    causal = col <= row
    expand = (lax.broadcasted_iota(jnp.int32, (128, 512), 1) // SSD_HEAD_DIM + DT_LANE
              == lax.broadcasted_iota(jnp.int32, (128, 512), 0))
    lane = lax.broadcasted_iota(jnp.int32, (L, 128), 1)
    rowi = lax.broadcasted_iota(jnp.int32, (L, 128), 0)
    low = lane < 64
    low8 = lax.broadcasted_iota(jnp.int32, (8, 128), 1) < 64
    low64 = lax.broadcasted_iota(jnp.int32, (64, 128), 1) < 64
    keep = [rowi >= (lane % 64) + 64 * jh for jh in range(2)]
    blockdiag = (lax.broadcasted_iota(jnp.int32, (128, 128), 0) // 64
                 == lax.broadcasted_iota(jnp.int32, (128, 128), 1) // 64)

    def both_halves(x, g):
        r = pltpu.roll(x, 64, 1)
        return jnp.where(low, x, r) if g == 0 else jnp.where(low, r, x)

    def chunk(rows):
        z = p_ref[rows, 0:512]
        cbuf[8:8 + L, :] = p_ref[rows, 512:512 + SSD_XBC]
        acc = cb_ref[...] + cbuf[pl.ds(8 - (SSD_CONV - 1), L), :] * cw_ref[0:1, :]
        for k in range(1, SSD_CONV):
            acc = acc + cbuf[pl.ds(8 - (SSD_CONV - 1) + k, L), :] * cw_ref[k:k + 1, :]
        cbuf[0:8, :] = cbuf[L:L + 8, :]
        xbc = _silu(acc)
        xs = xbc[:, 0:512]
        bm = xbc[:, 512:640]
        cm = xbc[:, 640:768]

        dt_c = _softplus(sh_ref[rows, :] + dtb_ref[...])
        cs_c = _dot_01(causal, dt_c * -jnp.exp(alog_ref[...]))
        dt = _dot_r01(dt_c, expand)
        cs = _dot_r01(cs_c, expand)
        cs_t = jnp.transpose(cs_c)
        cs_last = cs[L - 1:L, :]
        grow = jnp.exp(cs)
        tail = jnp.exp(cs_last - cs)
        total = jnp.exp(cs_last)
        xdt = (xs * dt).astype(BF16)
        bmb = bm.astype(BF16)

        ys = []
        for g in range(2):
            cmask = jnp.where(low if g == 0 else jnp.logical_not(low), cm, 0.0).astype(BF16)
            gdup = [_dot_nt(cmask, jnp.concatenate([bmb[64 * jh:64 * jh + 64, :]] * 2, axis=0)) for jh in range(2)]
            cdup = both_halves(cm, g)
            bdup = both_halves(bm, g)
            for pp in range(2):
                pair = 2 * g + pp
                lanes = slice(128 * pair, 128 * (pair + 1))
                csp = cs[:, lanes]
                xp = xdt[:, lanes]
                h0 = DT_LANE + 2 * pair
                r0 = jnp.broadcast_to(cs_t[h0:h0 + 1, :], (8, L))
                r1 = jnp.broadcast_to(cs_t[h0 + 1:h0 + 2, :], (8, L))
                crow = [jnp.where(low8, r0, pltpu.roll(r1, 64, 1))[0:1, :],
                        jnp.where(low8, pltpu.roll(r0, 64, 1), r1)[0:1, :]]
                y = None
                for jh in range(2):
                    lmat = jnp.exp(jnp.where(keep[jh], csp - crow[jh], -jnp.inf))
                    xj = xp[64 * jh:64 * jh + 64, :]
                    zero = jnp.zeros_like(xj)
                    xbd = jnp.concatenate([jnp.where(low64, xj, zero), jnp.where(low64, zero, xj)], axis=0)
                    t = _dot((gdup[jh] * lmat).astype(BF16), xbd)
                    y = t if y is None else y + t
                st = st_ref[pair]
                y = y + _dot_nt((cdup * grow[:, lanes]).astype(BF16), st.astype(BF16))
                upd = _dot_tn(xp, (bdup * tail[:, lanes]).astype(BF16))
                st_ref[pair] = total[:, lanes] * st + jnp.where(blockdiag, upd, 0.0)
                ys.append(y)
        y = jnp.concatenate(ys, axis=1) + d_ref[...] * xs
        y = y * _silu(z)
        ms = jnp.mean(y * y, axis=-1, keepdims=True)
        o_ref[rows, :] = (y * lax.rsqrt(ms + NORM_EPS) * nw_ref[...]).astype(BF16)

    for c in range(p_ref.shape[0] // L):
        chunk(slice(L * c, L * (c + 1)))


def _ssd_branch(l, p, shared, cw, cb, dtb, alog, dsk, nw, bsz, s):
    rows = SSD_CHUNK * SSD_CHUNKS_PER_STEP
    nc = s // rows
    return pl.pallas_call(
        _ssd_kernel,
        grid=(bsz, nc),
        in_specs=[pl.BlockSpec((rows, W_SSD), lambda b, c: (b * nc + c, 0)),
                  pl.BlockSpec((rows, 128), lambda b, c: (b * nc + c, SHARED_TILE)),
                  _layer((SSD_CONV, SSD_XBC), l), _layer((1, SSD_XBC), l),
                  _layer((1, 128), l), _layer((1, 128), l), _layer((1, 512), l), _layer((1, 512), l)],
        out_specs=pl.BlockSpec((rows, GROUP_W), lambda b, c: (b * nc + c, 0)),
        out_shape=jax.ShapeDtypeStruct((bsz * s, GROUP_W), BF16),
        scratch_shapes=[pltpu.VMEM((SSD_CHUNK + 8, SSD_XBC), F32),
                        pltpu.VMEM((SSD_HEADS // 2, 2 * SSD_HEAD_DIM, 2 * SSD_STATE), F32)],
        compiler_params=_params("arbitrary", "arbitrary"),
        name="ssd_branch",
    )(p, shared, cw, cb, dtb, alog, dsk, nw)


def _s5_prep_kernel(are_ref, aim_ref, ldt_ref, bre_ref, bim_ref, cre_ref, cim_ref, bb_ref, cc_ref, tab_ref):
    are = are_ref[...]
    aim = aim_ref[...]
    delta = jnp.exp(ldt_ref[...])
    mag = jnp.exp(are * delta)
    ar = mag * jnp.cos(aim * delta)
    ai = mag * jnp.sin(aim * delta)
    den = are * are + aim * aim
    coef_re = ((ar - 1.0) * are + ai * aim) / den
    coef_im = (ai * are - (ar - 1.0) * aim) / den

    bb_ref[...] = jnp.zeros_like(bb_ref)
    cc_ref[...] = jnp.zeros_like(cc_ref)
    for g in range(S5_GROUPS):
        kb, gl = divmod(g, 16)
        st = slice(S5_STATE * g, S5_STATE * (g + 1))
        rows = slice(S5_CH * gl, S5_CH * (gl + 1))
        re = slice(S5_STATE * gl, S5_STATE * (gl + 1))
        im = slice(S5_HALF + S5_STATE * gl, S5_HALF + S5_STATE * (gl + 1))
        bre = bre_ref[g]
        bim = bim_ref[g]
        bb_ref[kb, rows, re] = (coef_re[:, st] * bre - coef_im[:, st] * bim).astype(BF16)
        bb_ref[kb, rows, im] = (coef_re[:, st] * bim + coef_im[:, st] * bre).astype(BF16)
        cc_ref[kb, re, rows] = cre_ref[g].astype(BF16)
        cc_ref[kb, im, rows] = (-cim_ref[g]).astype(BF16)

    odd = lax.broadcasted_iota(jnp.int32, (8, S5_HALF), 0) % 2 == 1
    tab_ref[0] = jnp.where(odd, ar[:, S5_HALF:], ar[:, :S5_HALF])
    tab_ref[1] = jnp.where(odd, ai[:, S5_HALF:], ai[:, :S5_HALF])


def _s5_prep(l, are, aim, ldt, bre, bim, cre, cim):
    out3 = lambda shape: pl.BlockSpec(shape, lambda i: (0, 0, 0))
    return pl.pallas_call(
        _s5_prep_kernel,
        grid=(1,),
        in_specs=[_layer((1, S5_NS), l), _layer((1, S5_NS), l), _layer((1, S5_NS), l),
                  _layer((S5_GROUPS, S5_CH, S5_STATE), l), _layer((S5_GROUPS, S5_CH, S5_STATE), l),
                  _layer((S5_GROUPS, S5_STATE, S5_CH), l), _layer((S5_GROUPS, S5_STATE, S5_CH), l)],
        out_specs=[out3((2, 256, 2 * S5_HALF)), out3((2, 2 * S5_HALF, 256)), out3((2, 8, S5_HALF))],
        out_shape=[jax.ShapeDtypeStruct((2, 256, 2 * S5_HALF), BF16),
                   jax.ShapeDtypeStruct((2, 2 * S5_HALF, 256), BF16),
                   jax.ShapeDtypeStruct((2, 8, S5_HALF), F32)],
        compiler_params=_params("arbitrary"),
        name="s5_prep",
    )(are, aim, ldt, bre, bim, cre, cim)


def _s5_kernel(p_ref, bb_ref, cc_ref, tab_ref, d_ref, gw_ref, gb_ref, o_ref, pbuf, hbuf, ybuf, carry):
    nb = p_ref.shape[0]
    rt = S5_TSTEP
    n = nb * rt

    @pl.when(pl.program_id(0) == 0)
    def _():
        carry[...] = jnp.zeros_like(carry)

    def cat(ref, blocks, rows):
        return jnp.concatenate([ref[c, rows, :] for c in blocks], axis=1)

    for b in range(nb):
        for c in range(W_S5 // 128):
            pbuf[c, pl.ds(b, rt, stride=nb), :] = p_ref[b, :, 128 * c:128 * (c + 1)]
    u = cat(pbuf, range(0, 4), slice(None))
    ub = u.astype(BF16)
    nblk = 2 * S5_HALF // 128
    for j in range(2):
        hj = _dot(ub[:, 256 * j:256 * (j + 1)], bb_ref[j])
        for c in range(nblk):
            hbuf[c, pl.ds(j, n, stride=2), :] = hj[:, 128 * c:128 * (c + 1)]

    wb = S5_SCAN_LANES // 128
    for lc in range(S5_HALF // S5_SCAN_LANES):
        re_blocks = list(range(wb * lc, wb * (lc + 1)))
        im_blocks = [c + nblk // 2 for c in re_blocks]
        lanes = slice(S5_SCAN_LANES * lc, S5_SCAN_LANES * (lc + 1))
        ar = tab_ref[0, :, lanes]
        ai = tab_ref[1, :, lanes]

        def body(t, c, re_blocks=re_blocks, im_blocks=im_blocks, ar=ar, ai=ai):
            hr, hi = c
            rows = pl.ds(pl.multiple_of(t * 8, 8), 8)
            hr, hi = (ar * hr - ai * hi + cat(hbuf, re_blocks, rows),
                      ar * hi + ai * hr + cat(hbuf, im_blocks, rows))
            for k in range(wb):
                hbuf[re_blocks[k], rows, :] = hr[:, 128 * k:128 * (k + 1)]
                hbuf[im_blocks[k], rows, :] = hi[:, 128 * k:128 * (k + 1)]
            return hr, hi

        re = slice(S5_SCAN_LANES * lc, S5_SCAN_LANES * (lc + 1))
        im = slice(S5_HALF + re.start, S5_HALF + re.stop)
        hr, hi = lax.fori_loop(0, rt, body, (carry[:, re], carry[:, im]), unroll=4)
        carry[:, re] = hr
        carry[:, im] = hi

    ys = [_dot(cat(hbuf, range(nblk), pl.ds(j, n, stride=2)).astype(BF16), cc_ref[j]) for j in range(2)]
    y = jnp.concatenate(ys, axis=1) + d_ref[...] * u
    y = 0.5 * y * (1.0 + jnp.tanh(math.sqrt(2.0 / math.pi) * (y + 0.044715 * (y * y * y))))
    glu = _dot(y.astype(BF16), gw_ref[...]) + gb_ref[...]
    y = y / (1.0 + jnp.exp(-glu))
    y = y * _silu(cat(pbuf, range(4, 8), slice(None)))
    for c in range(GROUP_W // 128):
        ybuf[c] = y[:, 128 * c:128 * (c + 1)]
    for b in range(nb):
        o_ref[b] = cat(ybuf, range(GROUP_W // 128), pl.ds(b, rt, stride=nb)).astype(BF16)


def _s5_branch(l, p, bb, cc, tab, dsk, gw, gb, bsz, s):
    assert 2 * bsz == 8, "scan rows (batch, state block) must fill the 8 sublanes"
    const = lambda shape: pl.BlockSpec(shape, lambda t: tuple(0 for _ in shape))
    n = bsz * S5_TSTEP
    out = pl.pallas_call(
        _s5_kernel,
        grid=(s // S5_TSTEP,),
        in_specs=[pl.BlockSpec((bsz, S5_TSTEP, W_S5), lambda t: (0, t, 0)),
                  const((2, 256, 2 * S5_HALF)), const((2, 2 * S5_HALF, 256)), const((2, 8, S5_HALF)),
                  _layer((1, GROUP_W), l), _layer((GROUP_W, GROUP_W), l), _layer((1, GROUP_W), l)],
        out_specs=pl.BlockSpec((bsz, S5_TSTEP, GROUP_W), lambda t: (0, t, 0)),
        out_shape=jax.ShapeDtypeStruct((bsz, s, GROUP_W), BF16),
        scratch_shapes=[pltpu.VMEM((W_S5 // 128, n, 128), F32),
                        pltpu.VMEM((2 * S5_HALF // 128, 2 * n, 128), F32),
                        pltpu.VMEM((GROUP_W // 128, n, 128), F32),
                        pltpu.VMEM((8, 2 * S5_HALF), F32)],
        compiler_params=_params("arbitrary"),
        name="s5_branch",
    )(p.reshape(bsz, s, W_S5), bb, cc, tab, dsk, gw, gb)
    return out.reshape(bsz * s, GROUP_W)


def _gla_kernel(p_ref, sh_ref, w2_ref, gb_ref, nw_ref, o_ref, st_ref):
    T = GLA_TILE
    NCH = T // GLA_CHUNK

    @pl.when(pl.program_id(1) == 0)
    def _():
        st_ref[...] = jnp.zeros_like(st_ref)

    row = lax.broadcasted_iota(jnp.int32, (T, T), 0)
    col = lax.broadcasted_iota(jnp.int32, (T, T), 1)
    same = (row // GLA_CHUNK) == (col // GLA_CHUNK)
    intra = jnp.logical_and(same, col <= row)
    wide = (T, NCH * 128)
    blockmask = (lax.broadcasted_iota(jnp.int32, wide, 0) // GLA_CHUNK
                 == lax.broadcasted_iota(jnp.int32, wide, 1) // 128)
    wide_head = (lax.broadcasted_iota(jnp.int32, wide, 1) % 128) // GLA_DK
    lane_head = lax.broadcasted_iota(jnp.int32, (T, 128), 1) // GLA_DK
    zero = jnp.zeros((), BF16)
    scale = GLA_DK ** -0.5
    states = [st_ref[pair] for pair in range(GLA_HEADS // 2)]

    subs = range(p_ref.shape[0] // T)
    pairs = range(GLA_HEADS // 2)
    rows = [slice(T * sub, T * (sub + 1)) for sub in subs]
    xs = [_dot(sh_ref[rows[s], :].astype(BF16), w2_ref[...]) + gb_ref[...] for s in subs]
    gs = [-_softplus(-x) * (1.0 / GLA_GATE_NORM) for x in xs]
    bs = [_dot_01(intra, g) for g in gs]
    blasts = [_dot_01(same, g) for g in gs]
    q_mid, k_mid, q_dec, k_dec = [], [], [], []
    for s in subs:
        q = p_ref[rows[s], 0:256] * scale
        k = p_ref[rows[s], 256:512]
        half = 0.5 * blasts[s]
        q_mid.append((q * jnp.exp(bs[s] - half)).astype(BF16))
        k_mid.append((k * jnp.exp(half - bs[s])).astype(BF16))
        q_dec.append((q * jnp.exp(bs[s])).astype(BF16))
        k_dec.append((k * jnp.exp(blasts[s] - bs[s])).astype(BF16))

    lanes = [slice(128 * pair, 128 * (pair + 1)) for pair in pairs]
    vs = [[p_ref[rows[s], 512 + GLA_DV * h:512 + GLA_DV * (h + 1)].astype(BF16) for h in range(GLA_HEADS)]
          for s in subs]
    ds = {}
    for s in subs:
        for pair in pairs:
            kcat = jnp.where(blockmask, jnp.tile(k_dec[s][:, lanes[pair]], (1, NCH)), zero)
            ds[s, pair] = sum(_dot_tn(vs[s][2 * pair + hh], jnp.where(wide_head == hh, kcat, zero))
                              for hh in range(2))
    st_all = {}
    for s in subs:
        for pair in pairs:
            st = states[pair]
            sts = []
            for c in range(NCH):
                sts.append(st)
                dec = jnp.exp(blasts[s][GLA_CHUNK * c:GLA_CHUNK * c + 1, lanes[pair]])
                st = dec * st + ds[s, pair][:, 128 * c:128 * (c + 1)]
            states[pair] = st
            st_all[s, pair] = jnp.concatenate(sts, axis=1).astype(BF16)
    for pair in pairs:
        st_ref[pair] = states[pair]

    for s in subs:
        for pair in pairs:
            qcat = jnp.where(blockmask, jnp.tile(q_dec[s][:, lanes[pair]], (1, NCH)), zero)
            for hh in range(2):
                h = 2 * pair + hh
                attn = _dot_nt(jnp.where(lane_head == hh, q_mid[s][:, lanes[pair]], zero), k_mid[s][:, lanes[pair]])
                o = _dot(jnp.where(intra, attn, 0.0).astype(BF16), vs[s][h])
                o = o + _dot_nt(jnp.where(wide_head == hh, qcat, zero), st_all[s, pair])
                ms = jnp.mean(o * o, axis=-1, keepdims=True)
                o = o * lax.rsqrt(ms + NORM_EPS) * nw_ref[...]
                gate = p_ref[rows[s], 1024 + GLA_DV * h:1024 + GLA_DV * (h + 1)]
                o_ref[rows[s], GLA_DV * h:GLA_DV * (h + 1)] = (o * _silu(gate)).astype(BF16)


def _gla_branch(l, p, shared, w2, gb, nw, bsz, s):
    rows = GLA_TILE * GLA_TILES_PER_STEP
    nt = s // rows
    return pl.pallas_call(
        _gla_kernel,
        grid=(bsz, nt),
        in_specs=[pl.BlockSpec((rows, W_GLA), lambda b, t: (b * nt + t, 0)),
                  pl.BlockSpec((rows, 128), lambda b, t: (b * nt + t, SHARED_TILE)),
                  _layer((128, 256), l), _layer((1, 256), l), _layer((1, GLA_DV), l)],
        out_specs=pl.BlockSpec((rows, GROUP_W), lambda b, t: (b * nt + t, 0)),
        out_shape=jax.ShapeDtypeStruct((bsz * s, GROUP_W), BF16),
        scratch_shapes=[pltpu.VMEM((GLA_HEADS // 2, GLA_DV, 2 * GLA_DK), F32)],
        compiler_params=_params("arbitrary", "arbitrary"),
        name="gla_branch",
    )(p, shared, w2, gb, nw)


def _rope_swap(t):
    return pltpu.roll(t, 64, 1)


def _spread_rope(w):
    half = MLA_ROPE // 2
    zeros = jnp.zeros(w.shape[:-1] + (64 - half,), w.dtype)
    return jnp.concatenate([w[..., :half], zeros, w[..., half:], zeros], axis=-1)


def _rope_kernel(pos_ref, invf_ref, sign_ref, cos_ref, sin_ref):
    ang = pos_ref[...].astype(F32) * invf_ref[...]
    cos_ref[...] = jnp.cos(ang)
    sin_ref[...] = jnp.sin(ang) * sign_ref[...]


def _rope_tables(pos, invf, sign, tm=512):
    t = pos.shape[0]
    const = pl.BlockSpec((1, 128), lambda i: (0, 0))
    tile = pl.BlockSpec((tm, 128), lambda i: (i, 0))
    return pl.pallas_call(
        _rope_kernel,
        grid=(t // tm,),
        in_specs=[pl.BlockSpec((tm, 1), lambda i: (i, 0)), const, const],
        out_specs=[tile, tile],
        out_shape=[jax.ShapeDtypeStruct((t, 128), F32)] * 2,
        compiler_params=_params("arbitrary"),
        name="rope_tables",
    )(pos, invf, sign)


def _mla_prep_kernel(p_ref, cos_ref, sin_ref, qnw_ref, wuq_ref, kvnw_ref, wukv_ref,
                     qhw_ref, khw_ref, q_ref, k_ref, vt_ref):
    inv_d = 1.0 / (MLA_NOPE + MLA_ROPE)
    scale = (MLA_NOPE + MLA_ROPE) ** -0.5 * math.log2(math.e)
    qhw = qhw_ref[...]
    khw = khw_ref[...]
    sub = 128
    lane = lax.broadcasted_iota(jnp.int32, (sub, 128), 1)
    rope_lanes = (lane % 64) < MLA_ROPE // 2
    for r0 in range(0, p_ref.shape[0], sub):
        rows = slice(r0, r0 + sub)
        cq = p_ref[rows, 512:896]
        ckv = p_ref[rows, 896:1024]
        kpe = jnp.where(rope_lanes, p_ref[rows, 1024:1152], 0.0)
        ms = jnp.mean(cq * cq, axis=-1, keepdims=True)
        qn = (cq * lax.rsqrt(ms + NORM_EPS) * qnw_ref[...]).astype(BF16)
        ms = jnp.mean(ckv * ckv, axis=-1, keepdims=True)
        kvn = (ckv * lax.rsqrt(ms + NORM_EPS) * kvnw_ref[...]).astype(BF16)
        cos_t = cos_ref[rows, :]
        sin_t = sin_ref[rows, :]
        kpe_ss = jnp.sum(kpe * kpe, axis=-1, keepdims=True)
        heads = range(MLA_HEADS)
        hp = MLA_HEAD_PAD
        qs = [_dot(qn, wuq_ref[:, hp * h:hp * (h + 1)]) for h in heads]
        kvs = [_dot(kvn, wukv_ref[:, hp * h:hp * (h + 1)]) for h in heads]
        q_ss = [jnp.sum(q * q, axis=-1, keepdims=True) for q in qs]
        k_ss = [jnp.sum(kv[:, 0:128] * kv[:, 0:128], axis=-1, keepdims=True) for kv in kvs]
        q_rs = [lax.rsqrt(ss * inv_d + NORM_EPS) * scale for ss in q_ss]
        k_rs = [lax.rsqrt((ss + kpe_ss) * inv_d + NORM_EPS) for ss in k_ss]
        q_rot = [qs[h][:, 128:256] * q_rs[h] * qhw[:, 128:256] for h in heads]
        k_rot = [kpe * k_rs[h] * khw[:, 128:256] for h in heads]
        q_swap = [_rope_swap(t) for t in q_rot]
        k_swap = [_rope_swap(t) for t in k_rot]
        for h in heads:
            lo = hp * h
            q_ref[rows, lo:lo + 128] = (qs[h][:, 0:128] * q_rs[h] * qhw[:, 0:128]).astype(BF16)
            q_ref[rows, lo + 128:lo + 256] = (q_rot[h] * cos_t + q_swap[h] * sin_t).astype(BF16)
            k_ref[rows, lo:lo + 128] = (kvs[h][:, 0:128] * k_rs[h] * khw[:, 0:128]).astype(BF16)
            k_ref[rows, lo + 128:lo + 256] = (k_rot[h] * cos_t + k_swap[h] * sin_t).astype(BF16)
            vt_ref[MLA_V * h:MLA_V * (h + 1), rows] = jnp.transpose(kvs[h][:, 128:256]).astype(BF16)


def _mla_prep(l, p, cos_t, sin_t, qnw, wuq, kvnw, wukv, qhw, khw, tm=512):
    t = p.shape[0]
    hp = MLA_HEADS * MLA_HEAD_PAD
    return pl.pallas_call(
        _mla_prep_kernel,
        grid=(t // tm,),
        in_specs=[pl.BlockSpec((tm, W_MLA), lambda i: (i, 0)),
                  pl.BlockSpec((tm, 128), lambda i: (i, 0)), pl.BlockSpec((tm, 128), lambda i: (i, 0)),
                  _layer((1, MLA_Q_RANK), l), _layer((MLA_Q_RANK, hp), l),
                  _layer((1, MLA_KV_RANK), l), _layer((MLA_KV_RANK, hp), l),
                  _layer((1, MLA_HEAD_PAD), l), _layer((1, MLA_HEAD_PAD), l)],
        out_specs=[pl.BlockSpec((tm, hp), lambda i: (i, 0)),
                   pl.BlockSpec((tm, hp), lambda i: (i, 0)),
                   pl.BlockSpec((MLA_HEADS * MLA_V, tm), lambda i: (0, i))],
        out_shape=[jax.ShapeDtypeStruct((t, hp), BF16),
                   jax.ShapeDtypeStruct((t, hp), BF16),
                   jax.ShapeDtypeStruct((MLA_HEADS * MLA_V, t), BF16)],
        compiler_params=_params("arbitrary"),
        name="mla_prep",
    )(p, cos_t, sin_t, qnw, wuq, kvnw, wukv, qhw, khw)


def _flash_kernel(q_ref, k_ref, vt_ref, g_ref, o_ref, s_ref):
    tq = FLASH_TQ
    hp = MLA_HEAD_PAD
    keep = (lax.broadcasted_iota(jnp.int32, (tq, tq), 0)
            <= lax.broadcasted_iota(jnp.int32, (tq, tq), 1))
    heads = range(MLA_HEADS)

    def attend(nb):
        n = nb * tq
        scores = [_dot_nt(k_ref[0:n, hp * h:hp * (h + 1)], q_ref[:, hp * h:hp * (h + 1)]) for h in heads]
        m = []
        for h in heads:
            blocks = [scores[h][tq * j:tq * (j + 1), :] for j in range(nb)]
            blocks[-1] = jnp.where(keep, blocks[-1], -jnp.inf)
            top = blocks[0]
            for j in range(nb):
                s_ref[h, tq * j:tq * (j + 1), :] = blocks[j]
                top = jnp.maximum(top, blocks[j])
            m.append(jnp.max(top, axis=0, keepdims=True))
        pt = [jnp.exp2(s_ref[h, 0:n, :] - m[h]) for h in heads]
        l = [jnp.sum(p, axis=0, keepdims=True) for p in pt]
        acc = [_dot(vt_ref[MLA_V * h:MLA_V * (h + 1), 0:n], pt[h].astype(BF16)) for h in heads]
        for h in heads:
            o = jnp.transpose(acc[h] / l[h])
            o_ref[:, MLA_V * h:MLA_V * (h + 1)] = (o * _silu(g_ref[:, MLA_V * h:MLA_V * (h + 1)])).astype(BF16)

    for nb in range(1, s_ref.shape[1] // tq + 1):
        pl.when(pl.program_id(1) == nb - 1)(functools.partial(attend, nb))


def _flash(q, k, vt, p, bsz, s):
    nq = s // FLASH_TQ
    hp = MLA_HEADS * MLA_HEAD_PAD
    hv = MLA_HEADS * MLA_V
    return pl.pallas_call(
        _flash_kernel,
        grid=(bsz, nq),
        in_specs=[pl.BlockSpec((FLASH_TQ, hp), lambda b, i: (b * nq + i, 0)),
                  pl.BlockSpec((s, hp), lambda b, i: (b, 0)),
                  pl.BlockSpec((hv, s), lambda b, i: (0, b)),
                  pl.BlockSpec((FLASH_TQ, GROUP_W), lambda b, i: (b * nq + i, 0))],
        out_specs=pl.BlockSpec((FLASH_TQ, hv), lambda b, i: (b * nq + i, 0)),
        out_shape=jax.ShapeDtypeStruct((bsz * s, hv), BF16),
        scratch_shapes=[pltpu.VMEM((MLA_HEADS, s, FLASH_TQ), F32)],
        compiler_params=_params("arbitrary", "arbitrary"),
        name="mla_flash",
    )(q, k, vt, p)


def _regroup_kernel(wt_ref, ssd_ref, s5_ref, gla_ref, mla_ref):
    offs = [0]
    for wd in IN_WIDTHS:
        offs.append(offs[-1] + wd)
    (z, xbc, dt, s5u, s5g, gq, gk, gv, gg, glr, cq, ckv, kpe, mg) = [
        (a, b) for a, b in zip(offs[:-1], offs[1:])]

    def put(dst, at, seg):
        dst[at:at + seg[1] - seg[0], :] = wt_ref[seg[0]:seg[1], :].astype(BF16)

    put(ssd_ref, 0, z)
    put(ssd_ref, 512, xbc)
    put(s5_ref, 0, s5u)
    put(s5_ref, 512, s5g)
    put(gla_ref, 0, gq)
    put(gla_ref, 256, gk)
    put(gla_ref, 512, gv)
    put(gla_ref, 1024, gg)
    put(mla_ref, 0, mg)
    put(mla_ref, 512, cq)
    put(mla_ref, 896, ckv)
    half = MLA_ROPE // 2
    zeros = lambda n: jnp.zeros((n, wt_ref.shape[1]), F32)
    shared = jnp.concatenate([wt_ref[kpe[0]:kpe[0] + half, :], wt_ref[dt[0]:dt[1], :], zeros(24),
                              wt_ref[kpe[0] + half:kpe[1], :], wt_ref[glr[0]:glr[1], :], zeros(16)], axis=0)
    mla_ref[1024:1152, :] = shared.astype(BF16)


def _pad_rows(x, n):
    return jnp.concatenate([x, jnp.zeros((n - x.shape[0], x.shape[1]), x.dtype)], axis=0)


def _regroup_w_in(w_in, tc=256):
    wt = jnp.swapaxes(w_in, 1, 2)
    depth, n, k = wt.shape
    widths = (W_SSD, W_S5, W_GLA, W_MLA)
    return pl.pallas_call(
        _regroup_kernel,
        grid=(depth, k // tc),
        in_specs=[pl.BlockSpec((None, n, tc), lambda l, i: (l, 0, i))],
        out_specs=[pl.BlockSpec((None, w, tc), lambda l, i: (l, 0, i)) for w in widths],
        out_shape=[jax.ShapeDtypeStruct((depth, w, k), BF16) for w in widths],
        compiler_params=_params("arbitrary", "arbitrary"),
        name="regroup_w_in",
    )(wt)


def _pad_last(w, n):
    return jnp.pad(w, [(0, 0)] * (w.ndim - 1) + [(0, n - w.shape[-1])])


def kernel(x, positions, norm_w, w_in, w_out, ssd_conv_w, ssd_conv_b, ssd_dt_bias, ssd_a_log, ssd_d, ssd_norm_w, s5_a_re, s5_a_im, s5_log_dt, s5_b_re, s5_b_im, s5_c_re, s5_c_im, s5_d, s5_glu_w, s5_glu_b, gla_gate_w2, gla_gate_b, gla_norm_w, mla_q_norm_w, mla_w_uq, mla_kv_norm_w, mla_w_ukv, mla_q_head_norm_w, mla_k_head_norm_w):
    bsz, s, d = x.shape
    depth = w_in.shape[0]
    t = bsz * s
    h = x.reshape(t, d)
    pos = positions.reshape(t, 1)
    row = lambda v: v[:, None, :]

    inv_freq = ROPE_THETA ** (-jnp.arange(0, MLA_ROPE, 2, dtype=F32) / MLA_ROPE)
    invf = _spread_rope(jnp.concatenate([inv_freq, inv_freq]))[None, :]
    sign = _spread_rope(jnp.concatenate([-jnp.ones((32,), F32), jnp.ones((32,), F32)]))[None, :]

    w_ssd, w_s5, w_gla, w_mla = _regroup_w_in(w_in)
    w_out_b = w_out.astype(BF16)
    norm_w3 = row(norm_w)

    ssd_cb, ssd_nw = row(ssd_conv_b), row(ssd_norm_w)
    at_dt = lambda v: row(jnp.pad(v, ((0, 0), (DT_LANE, 128 - DT_LANE - SSD_HEADS))))
    ssd_dtb, ssd_alog = at_dt(ssd_dt_bias), at_dt(ssd_a_log)
    ssd_dsk = row(jnp.repeat(ssd_d, SSD_HEAD_DIM, axis=1))

    s5_are = s5_a_re.reshape(depth, 1, S5_NS)
    s5_aim = s5_a_im.reshape(depth, 1, S5_NS)
    s5_ldt = row(jnp.repeat(s5_log_dt, S5_STATE, axis=1))
    s5_bre, s5_bim = s5_b_re.transpose(0, 1, 3, 2), s5_b_im.transpose(0, 1, 3, 2)
    s5_cre, s5_cim = s5_c_re.transpose(0, 1, 3, 2), s5_c_im.transpose(0, 1, 3, 2)
    s5_dsk, s5_gw, s5_gb = row(s5_d), s5_glu_w.astype(BF16), row(s5_glu_b)

    gla_w2 = jnp.pad(gla_gate_w2, ((0, 0), (GLR_LANE, 128 - GLR_LANE - gla_gate_w2.shape[1]), (0, 0))).astype(BF16)
    gla_gb, gla_nw = row(gla_gate_b), row(gla_norm_w)

    head_pad = lambda w: jnp.concatenate([w[..., :MLA_NOPE], _spread_rope(w[..., MLA_NOPE:])], axis=-1)
    wuq = head_pad(mla_w_uq.reshape(depth, MLA_Q_RANK, MLA_HEADS, MLA_NOPE + MLA_ROPE))
    wuq = wuq.reshape(depth, MLA_Q_RANK, MLA_HEADS * MLA_HEAD_PAD).astype(BF16)
    wukv = mla_w_ukv.astype(BF16)
    qnw, kvnw = row(mla_q_norm_w), row(mla_kv_norm_w)
    qhw, khw = row(head_pad(mla_q_head_norm_w)), row(head_pad(mla_k_head_norm_w))

    cos_t, sin_t = _rope_tables(pos, invf, sign)
    u = _rmsnorm_bf16(h, norm_w3, 0)
    for l in range(depth):
        p_ssd = _in_proj(u, w_ssd, l, "in_proj_ssd")
        p_s5 = _in_proj(u, w_s5, l, "in_proj_s5")
        p_gla = _in_proj(u, w_gla, l, "in_proj_gla")
        p_mla = _in_proj(u, w_mla, l, "in_proj_mla")

        y_a = _ssd_branch(l, p_ssd, p_mla, ssd_conv_w, ssd_cb, ssd_dtb, ssd_alog, ssd_dsk, ssd_nw, bsz, s)

        bb, cc, tab = _s5_prep(l, s5_are, s5_aim, s5_ldt, s5_bre, s5_bim, s5_cre, s5_cim)
        y_b = _s5_branch(l, p_s5, bb, cc, tab, s5_dsk, s5_gw, s5_gb, bsz, s)

        y_c = _gla_branch(l, p_gla, p_mla, gla_w2, gla_gb, gla_nw, bsz, s)

        q, k, vt = _mla_prep(l, p_mla, cos_t, sin_t, qnw, wuq, kvnw, wukv, qhw, khw)
        y_d = _flash(q, k, vt, p_mla, bsz, s)

        h, u = _out_proj((y_a, y_b, y_c, y_d), w_out_b, h, norm_w3, l, (l + 1) % depth)
    return h.reshape(bsz, s, d)
```

```python
import functools
import math

import jax
import jax.numpy as jnp
from jax import lax
from jax.experimental import pallas as pl
from jax.experimental.pallas import tpu as pltpu

F32 = jnp.float32
BF16 = jnp.bfloat16
NORM_EPS = 1e-6
HI = lax.Precision.HIGHEST

D_MODEL = 2048
GROUP_W = 512
SSD_HEADS = 8
SSD_HEAD_DIM = 64
SSD_STATE = 64
SSD_CHUNK = 128
SSD_CHUNKS_PER_STEP = 2
SSD_XBC = 768
SSD_CONV = 4
S5_GROUPS = 32
S5_CH = 16
S5_STATE = 64
S5_NS = S5_GROUPS * S5_STATE
S5_HALF = S5_NS // 2
S5_TSTEP = 128
S5_SCAN_LANES = 512
GLA_HEADS = 4
GLA_DK = 64
GLA_DV = 128
GLA_CHUNK = 16
GLA_TILE = 128
GLA_TILES_PER_STEP = 4
GLA_GATE_NORM = 16.0
MLA_HEADS = 4
MLA_NOPE = 128
MLA_ROPE = 64
MLA_V = 128
MLA_Q_RANK = 384
MLA_KV_RANK = 128
MLA_HEAD_PAD = 256
ROPE_THETA = 10000.0
FLASH_TQ = 256

IN_WIDTHS = (512, 768, 8, 512, 512, 256, 256, 512, 512, 16, 384, 128, 64, 512)
W_SSD = 512 + 768
W_S5 = 1024
W_GLA = 256 + 256 + 512 + 512
W_MLA = 512 + 384 + 128 + 128
SHARED_TILE = W_MLA // 128 - 1
DT_LANE = 32
GLR_LANE = 96

VMEM_LIMIT_BYTES = 56 * 1024 * 1024


def _params(*sem):
    return pltpu.CompilerParams(dimension_semantics=sem, vmem_limit_bytes=VMEM_LIMIT_BYTES)


def _layer(shape, l):
    return pl.BlockSpec((None,) + tuple(shape), lambda *_: (l,) + (0,) * len(shape))


def _silu(x):
    return x / (1.0 + jnp.exp(-x))


def _softplus(x):
    return jnp.maximum(x, 0.0) + jnp.log(1.0 + jnp.exp(-jnp.abs(x)))


def _dot(a, b):
    return jnp.dot(a, b, preferred_element_type=F32)


def _dot_nt(a, b):
    return lax.dot_general(a, b, (((1,), (1,)), ((), ())), preferred_element_type=F32)


def _dot_tn(a, b):
    return lax.dot_general(a, b, (((0,), (0,)), ((), ())), preferred_element_type=F32)


def _dot_01(m, x):
    hi = x.astype(BF16)
    rest = x - hi.astype(F32)
    mid = rest.astype(BF16)
    lo = (rest - mid.astype(F32)).astype(BF16)
    return _dot(m, hi) + _dot(m, mid) + _dot(m, lo)


def _dot_r01(x, m):
    hi = x.astype(BF16)
    rest = x - hi.astype(F32)
    mid = rest.astype(BF16)
    lo = (rest - mid.astype(F32)).astype(BF16)
    return _dot(hi, m) + _dot(mid, m) + _dot(lo, m)


def _rms_kernel(x_ref, w_ref, o_ref):
    x = x_ref[...]
    ms = jnp.mean(x * x, axis=-1, keepdims=True)
    o_ref[...] = (x * lax.rsqrt(ms + NORM_EPS) * w_ref[...]).astype(BF16)


def _rmsnorm_bf16(x, w, l, tm=512):
    t, d = x.shape
    return pl.pallas_call(
        _rms_kernel,
        grid=(t // tm,),
        in_specs=[pl.BlockSpec((tm, d), lambda i: (i, 0)), _layer((1, d), l)],
        out_specs=pl.BlockSpec((tm, d), lambda i: (i, 0)),
        out_shape=jax.ShapeDtypeStruct((t, d), BF16),
        compiler_params=_params("arbitrary"),
        name="rmsnorm_in",
    )(x, w)


def _mm_kernel(x_ref, wt_ref, o_ref):
    o_ref[...] = _dot_nt(x_ref[...], wt_ref[...])


def _in_proj(u, wt, l, name, tm=1024):
    t, k = u.shape
    n = wt.shape[1]
    return pl.pallas_call(
        _mm_kernel,
        grid=(t // tm,),
        in_specs=[pl.BlockSpec((tm, k), lambda i: (i, 0)), _layer((n, k), l)],
        out_specs=pl.BlockSpec((tm, n), lambda i: (i, 0)),
        out_shape=jax.ShapeDtypeStruct((t, n), F32),
        compiler_params=_params("arbitrary"),
        name=name,
    )(u, wt)


def _out_proj_kernel(ya_ref, yb_ref, yc_ref, yd_ref, w_ref, h_ref, nw_ref, ho_ref, uo_ref, wb_ref):
    @pl.when(pl.program_id(0) == 0)
    def _():
        for i in range(w_ref.shape[0] // GROUP_W):
            wb_ref[GROUP_W * i:GROUP_W * (i + 1), :] = w_ref[GROUP_W * i:GROUP_W * (i + 1), :].astype(BF16)

    sub = 256
    for r in range(h_ref.shape[0] // sub):
        rows = slice(sub * r, sub * (r + 1))
        acc = h_ref[rows, :]
        for i, y_ref in enumerate((ya_ref, yb_ref, yc_ref, yd_ref)):
            acc = acc + _dot(y_ref[rows, :], wb_ref[GROUP_W * i:GROUP_W * (i + 1), :])
        ho_ref[rows, :] = acc
        ms = jnp.mean(acc * acc, axis=-1, keepdims=True)
        uo_ref[rows, :] = (acc * lax.rsqrt(ms + NORM_EPS) * nw_ref[...]).astype(BF16)


def _out_proj(ys, w, h, nw, l, l_next, tm=512):
    t, d = h.shape
    yspec = pl.BlockSpec((tm, GROUP_W), lambda i: (i, 0))
    return pl.pallas_call(
        _out_proj_kernel,
        grid=(t // tm,),
        in_specs=[yspec, yspec, yspec, yspec,
                  pl.BlockSpec((None, d, d), lambda i: (l, 0, 0), pipeline_mode=pl.Buffered(1)),
                  pl.BlockSpec((tm, d), lambda i: (i, 0)),
                  _layer((1, d), l_next)],
        out_specs=[pl.BlockSpec((tm, d), lambda i: (i, 0)), pl.BlockSpec((tm, d), lambda i: (i, 0))],
        out_shape=[jax.ShapeDtypeStruct((t, d), F32), jax.ShapeDtypeStruct((t, d), BF16)],
        scratch_shapes=[pltpu.VMEM((d, d), BF16)],
        compiler_params=_params("arbitrary"),
        name="out_proj",
    )(*ys, w, h, nw)


def _ssd_kernel(p_ref, sh_ref, cw_ref, cb_ref, dtb_ref, alog_ref, d_ref, nw_ref, o_ref, cbuf, st_ref):
    L = SSD_CHUNK

    @pl.when(pl.program_id(1) == 0)
    def _():
        cbuf[0:8, :] = jnp.zeros((8, SSD_XBC), F32)
        st_ref[...] = jnp.zeros_like(st_ref)

    row = lax.broadcasted_iota(jnp.int32, (L, L), 0)
    col = lax.broadcasted_iota(jnp.int32, (L, L), 1)
    causal = (col <= row).astype(BF16)
    expand = (lax.broadcasted_iota(jnp.int32, (128, 512), 1) // SSD_HEAD_DIM + DT_LANE
              == lax.broadcasted_iota(jnp.int32, (128, 512), 0)).astype(BF16)
    lane = lax.broadcasted_iota(jnp.int32, (L, 128), 1)
    rowi = lax.broadcasted_iota(jnp.int32, (L, 128), 0)
    low = lane < 64
    low8 = lax.broadcasted_iota(jnp.int32, (8, 128), 1) < 64
    low64 = lax.broadcasted_iota(jnp.int32, (64, 128), 1) < 64
    keep = [rowi >= (lane % 64) + 64 * jh for jh in range(2)]
    blockdiag = (lax.broadcasted_iota(jnp.int32, (128, 128), 0) // 64
                 == lax.broadcasted_iota(jnp.int32, (128, 128), 1) // 64)

    def both_halves(x, g):
        r = pltpu.roll(x, 64, 1)
        return jnp.where(low, x, r) if g == 0 else jnp.where(low, r, x)

    def chunk(rows):
        z = p_ref[rows, 0:512]
        cbuf[8:8 + L, :] = p_ref[rows, 512:512 + SSD_XBC]
        ext = cbuf[...]
        acc = cb_ref[...] + ext[8:8 + L, :] * cw_ref[SSD_CONV - 1:SSD_CONV, :]
        for k in range(SSD_CONV - 1):
            acc = acc + pltpu.roll(ext, SSD_CONV - 1 - k, 0)[8:8 + L, :] * cw_ref[k:k + 1, :]
        cbuf[0:8, :] = ext[L:L + 8, :]
        xbc = _silu(acc)
        xs = xbc[:, 0:512]
        bm = xbc[:, 512:640]
        cm = xbc[:, 640:768]

        dt_c = _softplus(sh_ref[rows, :] + dtb_ref[...])
        cs_c = _dot_01(causal, dt_c * -jnp.exp(alog_ref[...]))
        dt = _dot_r01(dt_c, expand)
        cs = _dot_r01(cs_c, expand)
        cs_t = jnp.transpose(cs_c)
        cs_last = cs[L - 1:L, :]
        grow = jnp.exp(cs)
        tail = jnp.exp(cs_last - cs)
        total = jnp.exp(cs_last)
        xdt = (xs * dt).astype(BF16)
        bmb = bm.astype(BF16)

        groups, pairs, halves = range(2), range(4), range(2)
        cmask = [jnp.where(low if g == 0 else jnp.logical_not(low), cm, 0.0).astype(BF16) for g in groups]
        bhalf = [jnp.concatenate([bmb[64 * jh:64 * jh + 64, :]] * 2, axis=0) for jh in halves]
        gdup = [[_dot_nt(cmask[g], bhalf[jh]) for jh in halves] for g in groups]
        cdup = [both_halves(cm, g) for g in groups]
        bdup = [both_halves(bm, g) for g in groups]
        lanes = [slice(128 * pair, 128 * (pair + 1)) for pair in pairs]
        crow = []
        for pair in pairs:
            h0 = DT_LANE + 2 * pair
            r0 = jnp.broadcast_to(cs_t[h0:h0 + 1, :], (8, L))
            r1 = jnp.broadcast_to(cs_t[h0 + 1:h0 + 2, :], (8, L))
            crow.append([jnp.where(low8, r0, pltpu.roll(r1, 64, 1))[0:1, :],
                         jnp.where(low8, pltpu.roll(r0, 64, 1), r1)[0:1, :]])
        lmat = [[jnp.exp(jnp.where(keep[jh], cs[:, lanes[pair]] - crow[pair][jh], -jnp.inf)) for jh in halves]
                for pair in pairs]
        xbd = []
        for pair in pairs:
            xp = xdt[:, lanes[pair]]
            per_half = []
            for jh in halves:
                xj = xp[64 * jh:64 * jh + 64, :]
                zero = jnp.zeros_like(xj)
                per_half.append(jnp.concatenate([jnp.where(low64, xj, zero), jnp.where(low64, zero, xj)], axis=0))
            xbd.append(per_half)
        sts = [st_ref[pair] for pair in pairs]
        ys = []
        for pair in pairs:
            g = pair // 2
            y = _dot_nt((cdup[g] * grow[:, lanes[pair]]).astype(BF16), sts[pair].astype(BF16))
            for jh in halves:
                y = y + _dot((gdup[g][jh] * lmat[pair][jh]).astype(BF16), xbd[pair][jh])
            ys.append(y)
        for pair in pairs:
            upd = _dot_tn(xdt[:, lanes[pair]], (bdup[pair // 2] * tail[:, lanes[pair]]).astype(BF16))
            st_ref[pair] = total[:, lanes[pair]] * sts[pair] + jnp.where(blockdiag, upd, 0.0)
        y = jnp.concatenate(ys, axis=1) + d_ref[...] * xs
        y = y * _silu(z)
        ms = jnp.mean(y * y, axis=-1, keepdims=True)
        o_ref[rows, :] = (y * lax.rsqrt(ms + NORM_EPS) * nw_ref[...]).astype(BF16)

    for c in range(p_ref.shape[0] // L):
        chunk(slice(L * c, L * (c + 1)))


def _ssd_branch(l, p, shared, cw, cb, dtb, alog, dsk, nw, bsz, s):
    rows = SSD_CHUNK * SSD_CHUNKS_PER_STEP
    nc = s // rows
    return pl.pallas_call(
        _ssd_kernel,
        grid=(bsz, nc),
        in_specs=[pl.BlockSpec((rows, W_SSD), lambda b, c: (b * nc + c, 0)),
                  pl.BlockSpec((rows, 128), lambda b, c: (b * nc + c, SHARED_TILE)),
                  _layer((SSD_CONV, SSD_XBC), l), _layer((1, SSD_XBC), l),
                  _layer((1, 128), l), _layer((1, 128), l), _layer((1, 512), l), _layer((1, 512), l)],
        out_specs=pl.BlockSpec((rows, GROUP_W), lambda b, c: (b * nc + c, 0)),
        out_shape=jax.ShapeDtypeStruct((bsz * s, GROUP_W), BF16),
        scratch_shapes=[pltpu.VMEM((SSD_CHUNK + 8, SSD_XBC), F32),
                        pltpu.VMEM((SSD_HEADS // 2, 2 * SSD_HEAD_DIM, 2 * SSD_STATE), F32)],
        compiler_params=_params("arbitrary", "arbitrary"),
        name="ssd_branch",
    )(p, shared, cw, cb, dtb, alog, dsk, nw)


def _s5_prep_kernel(are_ref, aim_ref, ldt_ref, bre_ref, bim_ref, cre_ref, cim_ref, bb_ref, cc_ref, tab_ref):
    are = are_ref[...]
    aim = aim_ref[...]
    delta = jnp.exp(ldt_ref[...])
    mag = jnp.exp(are * delta)
    ar = mag * jnp.cos(aim * delta)
    ai = mag * jnp.sin(aim * delta)
    den = are * are + aim * aim
    coef_re = ((ar - 1.0) * are + ai * aim) / den
    coef_im = (ai * are - (ar - 1.0) * aim) / den

    bb_ref[...] = jnp.zeros_like(bb_ref)
    cc_ref[...] = jnp.zeros_like(cc_ref)
    for g in range(S5_GROUPS):
        kb, gl = divmod(g, 16)
        st = slice(S5_STATE * g, S5_STATE * (g + 1))
        rows = slice(S5_CH * gl, S5_CH * (gl + 1))
        re = slice(S5_STATE * gl, S5_STATE * (gl + 1))
        im = slice(S5_HALF + S5_STATE * gl, S5_HALF + S5_STATE * (gl + 1))
        bre = bre_ref[g]
        bim = bim_ref[g]
        bb_ref[kb, rows, re] = (coef_re[:, st] * bre - coef_im[:, st] * bim).astype(BF16)
        bb_ref[kb, rows, im] = (coef_re[:, st] * bim + coef_im[:, st] * bre).astype(BF16)
        cc_ref[kb, re, rows] = cre_ref[g].astype(BF16)
        cc_ref[kb, im, rows] = (-cim_ref[g]).astype(BF16)

    odd = lax.broadcasted_iota(jnp.int32, (8, S5_HALF), 0) % 2 == 1
    tab_ref[0] = jnp.where(odd, ar[:, S5_HALF:], ar[:, :S5_HALF])
    tab_ref[1] = jnp.where(odd, ai[:, S5_HALF:], ai[:, :S5_HALF])


def _s5_prep(l, are, aim, ldt, bre, bim, cre, cim):
    out3 = lambda shape: pl.BlockSpec(shape, lambda i: (0, 0, 0))
    return pl.pallas_call(
        _s5_prep_kernel,
        grid=(1,),
        in_specs=[_layer((1, S5_NS), l), _layer((1, S5_NS), l), _layer((1, S5_NS), l),
                  _layer((S5_GROUPS, S5_CH, S5_STATE), l), _layer((S5_GROUPS, S5_CH, S5_STATE), l),
                  _layer((S5_GROUPS, S5_STATE, S5_CH), l), _layer((S5_GROUPS, S5_STATE, S5_CH), l)],
        out_specs=[out3((2, 256, 2 * S5_HALF)), out3((2, 2 * S5_HALF, 256)), out3((2, 8, S5_HALF))],
        out_shape=[jax.ShapeDtypeStruct((2, 256, 2 * S5_HALF), BF16),
                   jax.ShapeDtypeStruct((2, 2 * S5_HALF, 256), BF16),
                   jax.ShapeDtypeStruct((2, 8, S5_HALF), F32)],
        compiler_params=_params("arbitrary"),
        name="s5_prep",
    )(are, aim, ldt, bre, bim, cre, cim)


def _s5_kernel(p_ref, bb_ref, cc_ref, tab_ref, d_ref, gw_ref, gb_ref, o_ref, pbuf, hbuf, ybuf, carry):
    nb = p_ref.shape[0]
    rt = S5_TSTEP
    n = nb * rt

    @pl.when(pl.program_id(0) == 0)
    def _():
        carry[...] = jnp.zeros_like(carry)

    def cat(ref, blocks, rows):
        return jnp.concatenate([ref[c, rows, :] for c in blocks], axis=1)

    for b in range(nb):
        for c in range(W_S5 // 128):
            pbuf[c, pl.ds(b, rt, stride=nb), :] = p_ref[b, :, 128 * c:128 * (c + 1)]
    u = cat(pbuf, range(0, 4), slice(None))
    ub = u.astype(BF16)
    nblk = 2 * S5_HALF // 128
    for j in range(2):
        hj = _dot(ub[:, 256 * j:256 * (j + 1)], bb_ref[j])
        for c in range(nblk):
            hbuf[c, pl.ds(j, n, stride=2), :] = hj[:, 128 * c:128 * (c + 1)]

    wb = S5_SCAN_LANES // 128
    for lc in range(S5_HALF // S5_SCAN_LANES):
        re_blocks = list(range(wb * lc, wb * (lc + 1)))
        im_blocks = [c + nblk // 2 for c in re_blocks]
        lanes = slice(S5_SCAN_LANES * lc, S5_SCAN_LANES * (lc + 1))
        ar = tab_ref[0, :, lanes]
        ai = tab_ref[1, :, lanes]

        def body(t, c, re_blocks=re_blocks, im_blocks=im_blocks, ar=ar, ai=ai):
            hr, hi = c
            rows = pl.ds(pl.multiple_of(t * 8, 8), 8)
            hr, hi = (ar * hr - ai * hi + cat(hbuf, re_blocks, rows),
                      ar * hi + ai * hr + cat(hbuf, im_blocks, rows))
            for k in range(wb):
                hbuf[re_blocks[k], rows, :] = hr[:, 128 * k:128 * (k + 1)]
                hbuf[im_blocks[k], rows, :] = hi[:, 128 * k:128 * (k + 1)]
            return hr, hi

        re = slice(S5_SCAN_LANES * lc, S5_SCAN_LANES * (lc + 1))
        im = slice(S5_HALF + re.start, S5_HALF + re.stop)
        hr, hi = lax.fori_loop(0, rt, body, (carry[:, re], carry[:, im]), unroll=4)
        carry[:, re] = hr
        carry[:, im] = hi

    ys = [_dot(cat(hbuf, range(nblk), pl.ds(j, n, stride=2)).astype(BF16), cc_ref[j]) for j in range(2)]
    y = jnp.concatenate(ys, axis=1) + d_ref[...] * u
    y = 0.5 * y * (1.0 + jnp.tanh(math.sqrt(2.0 / math.pi) * (y + 0.044715 * (y * y * y))))
    glu = _dot(y.astype(BF16), gw_ref[...]) + gb_ref[...]
    y = y / (1.0 + jnp.exp(-glu))
    y = y * _silu(cat(pbuf, range(4, 8), slice(None)))
    for c in range(GROUP_W // 128):
        ybuf[c] = y[:, 128 * c:128 * (c + 1)]
    for b in range(nb):
        o_ref[b] = cat(ybuf, range(GROUP_W // 128), pl.ds(b, rt, stride=nb)).astype(BF16)


def _s5_branch(l, p, bb, cc, tab, dsk, gw, gb, bsz, s):
    assert 2 * bsz == 8, "scan rows (batch, state block) must fill the 8 sublanes"
    const = lambda shape: pl.BlockSpec(shape, lambda t: tuple(0 for _ in shape))
    n = bsz * S5_TSTEP
    out = pl.pallas_call(
        _s5_kernel,
        grid=(s // S5_TSTEP,),
        in_specs=[pl.BlockSpec((bsz, S5_TSTEP, W_S5), lambda t: (0, t, 0)),
                  const((2, 256, 2 * S5_HALF)), const((2, 2 * S5_HALF, 256)), const((2, 8, S5_HALF)),
                  _layer((1, GROUP_W), l), _layer((GROUP_W, GROUP_W), l), _layer((1, GROUP_W), l)],
        out_specs=pl.BlockSpec((bsz, S5_TSTEP, GROUP_W), lambda t: (0, t, 0)),
        out_shape=jax.ShapeDtypeStruct((bsz, s, GROUP_W), BF16),
        scratch_shapes=[pltpu.VMEM((W_S5 // 128, n, 128), F32),
                        pltpu.VMEM((2 * S5_HALF // 128, 2 * n, 128), F32),
                        pltpu.VMEM((GROUP_W // 128, n, 128), F32),
                        pltpu.VMEM((8, 2 * S5_HALF), F32)],
        compiler_params=_params("arbitrary"),
        name="s5_branch",
    )(p.reshape(bsz, s, W_S5), bb, cc, tab, dsk, gw, gb)
    return out.reshape(bsz * s, GROUP_W)


def _gla_kernel(p_ref, sh_ref, w2_ref, gb_ref, nw_ref, o_ref, st_ref):
    T = GLA_TILE
    NCH = T // GLA_CHUNK

    @pl.when(pl.program_id(1) == 0)
    def _():
        st_ref[...] = jnp.zeros_like(st_ref)

    row = lax.broadcasted_iota(jnp.int32, (T, T), 0)
    col = lax.broadcasted_iota(jnp.int32, (T, T), 1)
    same = (row // GLA_CHUNK) == (col // GLA_CHUNK)
    intra = jnp.logical_and(same, col <= row)
    same_b, intra_b = same.astype(BF16), intra.astype(BF16)
    wide = (T, NCH * 128)
    blockmask = (lax.broadcasted_iota(jnp.int32, wide, 0) // GLA_CHUNK
                 == lax.broadcasted_iota(jnp.int32, wide, 1) // 128)
    wide_head = (lax.broadcasted_iota(jnp.int32, wide, 1) % 128) // GLA_DK
    lane_head = lax.broadcasted_iota(jnp.int32, (T, 128), 1) // GLA_DK
    zero = jnp.zeros((), BF16)
    scale = GLA_DK ** -0.5
    states = [st_ref[pair] for pair in range(GLA_HEADS // 2)]

    subs = range(p_ref.shape[0] // T)
    pairs = range(GLA_HEADS // 2)
    rows = [slice(T * sub, T * (sub + 1)) for sub in subs]
    xs = [_dot(sh_ref[rows[s], :].astype(BF16), w2_ref[...]) + gb_ref[...] for s in subs]
    gs = [-_softplus(-x) * (1.0 / GLA_GATE_NORM) for x in xs]
    bs = [_dot_01(intra_b, g) for g in gs]
    blasts = [_dot_01(same_b, g) for g in gs]
    q_mid, k_mid, q_dec, k_dec = [], [], [], []
    for s in subs:
        q = p_ref[rows[s], 0:256] * scale
        k = p_ref[rows[s], 256:512]
        half = 0.5 * blasts[s]
        q_mid.append((q * jnp.exp(bs[s] - half)).astype(BF16))
        k_mid.append((k * jnp.exp(half - bs[s])).astype(BF16))
        q_dec.append((q * jnp.exp(bs[s])).astype(BF16))
        k_dec.append((k * jnp.exp(blasts[s] - bs[s])).astype(BF16))

    lanes = [slice(128 * pair, 128 * (pair + 1)) for pair in pairs]
    vs = [[p_ref[rows[s], 512 + GLA_DV * h:512 + GLA_DV * (h + 1)].astype(BF16) for h in range(GLA_HEADS)]
          for s in subs]
    ds = {}
    for s in subs:
        for pair in pairs:
            kcat = jnp.where(blockmask, jnp.tile(k_dec[s][:, lanes[pair]], (1, NCH)), zero)
            ds[s, pair] = sum(_dot_tn(vs[s][2 * pair + hh], jnp.where(wide_head == hh, kcat, zero))
                              for hh in range(2))
    st_all = {}
    for s in subs:
        for pair in pairs:
            st = states[pair]
            sts = []
            for c in range(NCH):
                sts.append(st)
                dec = jnp.exp(blasts[s][GLA_CHUNK * c:GLA_CHUNK * c + 1, lanes[pair]])
                st = dec * st + ds[s, pair][:, 128 * c:128 * (c + 1)]
            states[pair] = st
            st_all[s, pair] = jnp.concatenate(sts, axis=1).astype(BF16)
    for pair in pairs:
        st_ref[pair] = states[pair]

    for s in subs:
        for pair in pairs:
            qcat = jnp.where(blockmask, jnp.tile(q_dec[s][:, lanes[pair]], (1, NCH)), zero)
            for hh in range(2):
                h = 2 * pair + hh
                attn = _dot_nt(jnp.where(lane_head == hh, q_mid[s][:, lanes[pair]], zero), k_mid[s][:, lanes[pair]])
                o = _dot(jnp.where(intra, attn, 0.0).astype(BF16), vs[s][h])
                o = o + _dot_nt(jnp.where(wide_head == hh, qcat, zero), st_all[s, pair])
                ms = jnp.mean(o * o, axis=-1, keepdims=True)
                o = o * lax.rsqrt(ms + NORM_EPS) * nw_ref[...]
                gate = p_ref[rows[s], 1024 + GLA_DV * h:1024 + GLA_DV * (h + 1)]
                o_ref[rows[s], GLA_DV * h:GLA_DV * (h + 1)] = (o * _silu(gate)).astype(BF16)


def _gla_branch(l, p, shared, w2, gb, nw, bsz, s):
    rows = GLA_TILE * GLA_TILES_PER_STEP
    nt = s // rows
    return pl.pallas_call(
        _gla_kernel,
        grid=(bsz, nt),
        in_specs=[pl.BlockSpec((rows, W_GLA), lambda b, t: (b * nt + t, 0)),
                  pl.BlockSpec((rows, 128), lambda b, t: (b * nt + t, SHARED_TILE)),
                  _layer((128, 256), l), _layer((1, 256), l), _layer((1, GLA_DV), l)],
        out_specs=pl.BlockSpec((rows, GROUP_W), lambda b, t: (b * nt + t, 0)),
        out_shape=jax.ShapeDtypeStruct((bsz * s, GROUP_W), BF16),
        scratch_shapes=[pltpu.VMEM((GLA_HEADS // 2, GLA_DV, 2 * GLA_DK), F32)],
        compiler_params=_params("arbitrary", "arbitrary"),
        name="gla_branch",
    )(p, shared, w2, gb, nw)


def _rope_swap(t):
    return pltpu.roll(t, 64, 1)


def _spread_rope(w):
    half = MLA_ROPE // 2
    zeros = jnp.zeros(w.shape[:-1] + (64 - half,), w.dtype)
    return jnp.concatenate([w[..., :half], zeros, w[..., half:], zeros], axis=-1)


def _rope_kernel(pos_ref, invf_ref, sign_ref, cos_ref, sin_ref):
    ang = pos_ref[...].astype(F32) * invf_ref[...]
    cos_ref[...] = jnp.cos(ang)
    sin_ref[...] = jnp.sin(ang) * sign_ref[...]


def _rope_tables(pos, invf, sign, tm=512):
    t = pos.shape[0]
    const = pl.BlockSpec((1, 128), lambda i: (0, 0))
    tile = pl.BlockSpec((tm, 128), lambda i: (i, 0))
    return pl.pallas_call(
        _rope_kernel,
        grid=(t // tm,),
        in_specs=[pl.BlockSpec((tm, 1), lambda i: (i, 0)), const, const],
        out_specs=[tile, tile],
        out_shape=[jax.ShapeDtypeStruct((t, 128), F32)] * 2,
        compiler_params=_params("arbitrary"),
        name="rope_tables",
    )(pos, invf, sign)


def _mla_prep_kernel(p_ref, cos_ref, sin_ref, qnw_ref, wuq_ref, kvnw_ref, wukv_ref,
                     qhw_ref, khw_ref, q_ref, k_ref, vt_ref):
    inv_d = 1.0 / (MLA_NOPE + MLA_ROPE)
    scale = (MLA_NOPE + MLA_ROPE) ** -0.5 * math.log2(math.e)
    qhw = qhw_ref[...]
    khw = khw_ref[...]
    sub = 128
    lane = lax.broadcasted_iota(jnp.int32, (sub, 128), 1)
    rope_lanes = (lane % 64) < MLA_ROPE // 2
    for r0 in range(0, p_ref.shape[0], sub):
        rows = slice(r0, r0 + sub)
        cq = p_ref[rows, 512:896]
        ckv = p_ref[rows, 896:1024]
        kpe = jnp.where(rope_lanes, p_ref[rows, 1024:1152], 0.0)
        ms = jnp.mean(cq * cq, axis=-1, keepdims=True)
        qn = (cq * lax.rsqrt(ms + NORM_EPS) * qnw_ref[...]).astype(BF16)
        ms = jnp.mean(ckv * ckv, axis=-1, keepdims=True)
        kvn = (ckv * lax.rsqrt(ms + NORM_EPS) * kvnw_ref[...]).astype(BF16)
        cos_t = cos_ref[rows, :]
        sin_t = sin_ref[rows, :]
        kpe_ss = jnp.sum(kpe * kpe, axis=-1, keepdims=True)
        heads = range(MLA_HEADS)
        hp = MLA_HEAD_PAD
        qs = [_dot(qn, wuq_ref[:, hp * h:hp * (h + 1)]) for h in heads]
        kvs = [_dot(kvn, wukv_ref[:, hp * h:hp * (h + 1)]) for h in heads]
        q_ss = [jnp.sum(q * q, axis=-1, keepdims=True) for q in qs]
        k_ss = [jnp.sum(kv[:, 0:128] * kv[:, 0:128], axis=-1, keepdims=True) for kv in kvs]
        q_rs = [lax.rsqrt(ss * inv_d + NORM_EPS) * scale for ss in q_ss]
        k_rs = [lax.rsqrt((ss + kpe_ss) * inv_d + NORM_EPS) for ss in k_ss]
        q_rot = [qs[h][:, 128:256] * q_rs[h] * qhw[:, 128:256] for h in heads]
        k_rot = [kpe * k_rs[h] * khw[:, 128:256] for h in heads]
        q_swap = [_rope_swap(t) for t in q_rot]
        k_swap = [_rope_swap(t) for t in k_rot]
        for h in heads:
            lo = hp * h
            q_ref[rows, lo:lo + 128] = (qs[h][:, 0:128] * q_rs[h] * qhw[:, 0:128]).astype(BF16)
            q_ref[rows, lo + 128:lo + 256] = (q_rot[h] * cos_t + q_swap[h] * sin_t).astype(BF16)
            k_ref[rows, lo:lo + 128] = (kvs[h][:, 0:128] * k_rs[h] * khw[:, 0:128]).astype(BF16)
            k_ref[rows, lo + 128:lo + 256] = (k_rot[h] * cos_t + k_swap[h] * sin_t).astype(BF16)
            vt_ref[MLA_V * h:MLA_V * (h + 1), rows] = jnp.transpose(kvs[h][:, 128:256]).astype(BF16)


def _mla_prep(l, p, cos_t, sin_t, qnw, wuq, kvnw, wukv, qhw, khw, tm=512):
    t = p.shape[0]
    hp = MLA_HEADS * MLA_HEAD_PAD
    return pl.pallas_call(
        _mla_prep_kernel,
        grid=(t // tm,),
        in_specs=[pl.BlockSpec((tm, W_MLA), lambda i: (i, 0)),
                  pl.BlockSpec((tm, 128), lambda i: (i, 0)), pl.BlockSpec((tm, 128), lambda i: (i, 0)),
                  _layer((1, MLA_Q_RANK), l), _layer((MLA_Q_RANK, hp), l),
                  _layer((1, MLA_KV_RANK), l), _layer((MLA_KV_RANK, hp), l),
                  _layer((1, MLA_HEAD_PAD), l), _layer((1, MLA_HEAD_PAD), l)],
        out_specs=[pl.BlockSpec((tm, hp), lambda i: (i, 0)),
                   pl.BlockSpec((tm, hp), lambda i: (i, 0)),
                   pl.BlockSpec((MLA_HEADS * MLA_V, tm), lambda i: (0, i))],
        out_shape=[jax.ShapeDtypeStruct((t, hp), BF16),
                   jax.ShapeDtypeStruct((t, hp), BF16),
                   jax.ShapeDtypeStruct((MLA_HEADS * MLA_V, t), BF16)],
        compiler_params=_params("arbitrary"),
        name="mla_prep",
    )(p, cos_t, sin_t, qnw, wuq, kvnw, wukv, qhw, khw)


def _flash_kernel(q_ref, k_ref, vt_ref, g_ref, o_ref, s_ref):
    tq = FLASH_TQ
    hp = MLA_HEAD_PAD
    keep = (lax.broadcasted_iota(jnp.int32, (tq, tq), 0)
            <= lax.broadcasted_iota(jnp.int32, (tq, tq), 1))
    heads = range(MLA_HEADS)

    def attend(nb):
        n = nb * tq
        scores = [_dot_nt(k_ref[0:n, hp * h:hp * (h + 1)], q_ref[:, hp * h:hp * (h + 1)]) for h in heads]
        m = []
        for h in heads:
            blocks = [scores[h][tq * j:tq * (j + 1), :] for j in range(nb)]
            blocks[-1] = jnp.where(keep, blocks[-1], -jnp.inf)
            top = blocks[0]
            for j in range(nb):
                s_ref[h, tq * j:tq * (j + 1), :] = blocks[j]
                top = jnp.maximum(top, blocks[j])
            m.append(jnp.max(top, axis=0, keepdims=True))
        pt = [jnp.exp2(s_ref[h, 0:n, :] - m[h]) for h in heads]
        l = [jnp.sum(p, axis=0, keepdims=True) for p in pt]
        acc = [_dot(vt_ref[MLA_V * h:MLA_V * (h + 1), 0:n], pt[h].astype(BF16)) for h in heads]
        for h in heads:
            o = jnp.transpose(acc[h] / l[h])
            o_ref[:, MLA_V * h:MLA_V * (h + 1)] = (o * _silu(g_ref[:, MLA_V * h:MLA_V * (h + 1)])).astype(BF16)

    for nb in range(1, s_ref.shape[1] // tq + 1):
        pl.when(pl.program_id(1) == nb - 1)(functools.partial(attend, nb))


def _flash(q, k, vt, p, bsz, s):
    nq = s // FLASH_TQ
    hp = MLA_HEADS * MLA_HEAD_PAD
    hv = MLA_HEADS * MLA_V
    return pl.pallas_call(
        _flash_kernel,
        grid=(bsz, nq),
        in_specs=[pl.BlockSpec((FLASH_TQ, hp), lambda b, i: (b * nq + i, 0)),
                  pl.BlockSpec((s, hp), lambda b, i: (b, 0)),
                  pl.BlockSpec((hv, s), lambda b, i: (0, b)),
                  pl.BlockSpec((FLASH_TQ, GROUP_W), lambda b, i: (b * nq + i, 0))],
        out_specs=pl.BlockSpec((FLASH_TQ, hv), lambda b, i: (b * nq + i, 0)),
        out_shape=jax.ShapeDtypeStruct((bsz * s, hv), BF16),
        scratch_shapes=[pltpu.VMEM((MLA_HEADS, s, FLASH_TQ), F32)],
        compiler_params=_params("arbitrary", "arbitrary"),
        name="mla_flash",
    )(q, k, vt, p)


def _regroup_kernel(wt_ref, ssd_ref, s5_ref, gla_ref, mla_ref):
    offs = [0]
    for wd in IN_WIDTHS:
        offs.append(offs[-1] + wd)
    (z, xbc, dt, s5u, s5g, gq, gk, gv, gg, glr, cq, ckv, kpe, mg) = [
        (a, b) for a, b in zip(offs[:-1], offs[1:])]

    def put(dst, at, seg):
        dst[at:at + seg[1] - seg[0], :] = wt_ref[seg[0]:seg[1], :].astype(BF16)

    put(ssd_ref, 0, z)
    put(ssd_ref, 512, xbc)
    put(s5_ref, 0, s5u)
    put(s5_ref, 512, s5g)
    put(gla_ref, 0, gq)
    put(gla_ref, 256, gk)
    put(gla_ref, 512, gv)
    put(gla_ref, 1024, gg)
    put(mla_ref, 0, mg)
    put(mla_ref, 512, cq)
    put(mla_ref, 896, ckv)
    half = MLA_ROPE // 2
    zeros = lambda n: jnp.zeros((n, wt_ref.shape[1]), F32)
    shared = jnp.concatenate([wt_ref[kpe[0]:kpe[0] + half, :], wt_ref[dt[0]:dt[1], :], zeros(24),
                              wt_ref[kpe[0] + half:kpe[1], :], wt_ref[glr[0]:glr[1], :], zeros(16)], axis=0)
    mla_ref[1024:1152, :] = shared.astype(BF16)


def _pad_rows(x, n):
    return jnp.concatenate([x, jnp.zeros((n - x.shape[0], x.shape[1]), x.dtype)], axis=0)


def _regroup_w_in(w_in, tc=256):
    wt = jnp.swapaxes(w_in, 1, 2)
    depth, n, k = wt.shape
    widths = (W_SSD, W_S5, W_GLA, W_MLA)
    return pl.pallas_call(
        _regroup_kernel,
        grid=(depth, k // tc),
        in_specs=[pl.BlockSpec((None, n, tc), lambda l, i: (l, 0, i))],
        out_specs=[pl.BlockSpec((None, w, tc), lambda l, i: (l, 0, i)) for w in widths],
        out_shape=[jax.ShapeDtypeStruct((depth, w, k), BF16) for w in widths],
        compiler_params=_params("arbitrary", "arbitrary"),
        name="regroup_w_in",
    )(wt)


def _pad_last(w, n):
    return jnp.pad(w, [(0, 0)] * (w.ndim - 1) + [(0, n - w.shape[-1])])


def kernel(x, positions, norm_w, w_in, w_out, ssd_conv_w, ssd_conv_b, ssd_dt_bias, ssd_a_log, ssd_d, ssd_norm_w, s5_a_re, s5_a_im, s5_log_dt, s5_b_re, s5_b_im, s5_c_re, s5_c_im, s5_d, s5_glu_w, s5_glu_b, gla_gate_w2, gla_gate_b, gla_norm_w, mla_q_norm_w, mla_w_uq, mla_kv_norm_w, mla_w_ukv, mla_q_head_norm_w, mla_k_head_norm_w):
    bsz, s, d = x.shape
    depth = w_in.shape[0]
    t = bsz * s
    h = x.reshape(t, d)
    pos = positions.reshape(t, 1)
    row = lambda v: v[:, None, :]

    inv_freq = ROPE_THETA ** (-jnp.arange(0, MLA_ROPE, 2, dtype=F32) / MLA_ROPE)
    invf = _spread_rope(jnp.concatenate([inv_freq, inv_freq]))[None, :]
    sign = _spread_rope(jnp.concatenate([-jnp.ones((32,), F32), jnp.ones((32,), F32)]))[None, :]

    w_ssd, w_s5, w_gla, w_mla = _regroup_w_in(w_in)
    norm_w3 = row(norm_w)

    ssd_cb, ssd_nw = row(ssd_conv_b), row(ssd_norm_w)
    at_dt = lambda v: row(jnp.pad(v, ((0, 0), (DT_LANE, 128 - DT_LANE - SSD_HEADS))))
    ssd_dtb, ssd_alog = at_dt(ssd_dt_bias), at_dt(ssd_a_log)
    ssd_dsk = row(jnp.repeat(ssd_d, SSD_HEAD_DIM, axis=1))

    s5_are = s5_a_re.reshape(depth, 1, S5_NS)
    s5_aim = s5_a_im.reshape(depth, 1, S5_NS)
    s5_ldt = row(jnp.repeat(s5_log_dt, S5_STATE, axis=1))
    s5_bre, s5_bim = s5_b_re.transpose(0, 1, 3, 2), s5_b_im.transpose(0, 1, 3, 2)
    s5_cre, s5_cim = s5_c_re.transpose(0, 1, 3, 2), s5_c_im.transpose(0, 1, 3, 2)
    s5_dsk, s5_gw, s5_gb = row(s5_d), s5_glu_w.astype(BF16), row(s5_glu_b)

    gla_w2 = jnp.pad(gla_gate_w2, ((0, 0), (GLR_LANE, 128 - GLR_LANE - gla_gate_w2.shape[1]), (0, 0))).astype(BF16)
    gla_gb, gla_nw = row(gla_gate_b), row(gla_norm_w)

    head_pad = lambda w: jnp.concatenate([w[..., :MLA_NOPE], _spread_rope(w[..., MLA_NOPE:])], axis=-1)
    wuq = head_pad(mla_w_uq.reshape(depth, MLA_Q_RANK, MLA_HEADS, MLA_NOPE + MLA_ROPE))
    wuq = wuq.reshape(depth, MLA_Q_RANK, MLA_HEADS * MLA_HEAD_PAD).astype(BF16)
    wukv = mla_w_ukv.astype(BF16)
    qnw, kvnw = row(mla_q_norm_w), row(mla_kv_norm_w)
    qhw, khw = row(head_pad(mla_q_head_norm_w)), row(head_pad(mla_k_head_norm_w))

    cos_t, sin_t = _rope_tables(pos, invf, sign)
    u = _rmsnorm_bf16(h, norm_w3, 0)
    for l in range(depth):
        p_ssd = _in_proj(u, w_ssd, l, "in_proj_ssd")
        p_s5 = _in_proj(u, w_s5, l, "in_proj_s5")
        p_gla = _in_proj(u, w_gla, l, "in_proj_gla")
        p_mla = _in_proj(u, w_mla, l, "in_proj_mla")

        y_a = _ssd_branch(l, p_ssd, p_mla, ssd_conv_w, ssd_cb, ssd_dtb, ssd_alog, ssd_dsk, ssd_nw, bsz, s)

        bb, cc, tab = _s5_prep(l, s5_are, s5_aim, s5_ldt, s5_bre, s5_bim, s5_cre, s5_cim)
        y_b = _s5_branch(l, p_s5, bb, cc, tab, s5_dsk, s5_gw, s5_gb, bsz, s)

        y_c = _gla_branch(l, p_gla, p_mla, gla_w2, gla_gb, gla_nw, bsz, s)

        q, k, vt = _mla_prep(l, p_mla, cos_t, sin_t, qnw, wuq, kvnw, wukv, qhw, khw)
        y_d = _flash(q, k, vt, p_mla, bsz, s)

        h, u = _out_proj((y_a, y_b, y_c, y_d), w_out, h, norm_w3, l, (l + 1) % depth)
    return h.reshape(bsz, s, d)
```

```python
import functools
import math

import jax
import jax.numpy as jnp
from jax import lax
from jax.experimental import pallas as pl
from jax.experimental.pallas import tpu as pltpu

F32 = jnp.float32
BF16 = jnp.bfloat16
NORM_EPS = 1e-6
HI = lax.Precision.HIGHEST

D_MODEL = 2048
GROUP_W = 512
SSD_HEADS = 8
SSD_HEAD_DIM = 64
SSD_STATE = 64
SSD_CHUNK = 128
SSD_CHUNKS_PER_STEP = 2
SSD_XBC = 768
SSD_CONV = 4
S5_GROUPS = 32
S5_CH = 16
S5_STATE = 64
S5_NS = S5_GROUPS * S5_STATE
S5_HALF = S5_NS // 2
S5_TSTEP = 128
S5_SCAN_LANES = 512
GLA_HEADS = 4
GLA_DK = 64
GLA_DV = 128
GLA_CHUNK = 16
GLA_TILE = 128
GLA_TILES_PER_STEP = 4
GLA_GATE_NORM = 16.0
MLA_HEADS = 4
MLA_NOPE = 128
MLA_ROPE = 64
MLA_V = 128
MLA_Q_RANK = 384
MLA_KV_RANK = 128
MLA_HEAD_PAD = 256
ROPE_THETA = 10000.0
FLASH_TQ = 256

IN_WIDTHS = (512, 768, 8, 512, 512, 256, 256, 512, 512, 16, 384, 128, 64, 512)
W_SSD = 512 + 768
W_S5 = 1024
W_GLA = 256 + 256 + 512 + 512
W_MLA = 512 + 384 + 128 + 128
SHARED_TILE = W_MLA // 128 - 1
DT_LANE = 32
GLR_LANE = 96

VMEM_LIMIT_BYTES = 56 * 1024 * 1024


def _params(*sem):
    return pltpu.CompilerParams(dimension_semantics=sem, vmem_limit_bytes=VMEM_LIMIT_BYTES)


def _layer(shape, l):
    return pl.BlockSpec((None,) + tuple(shape), lambda *_: (l,) + (0,) * len(shape))


def _silu(x):
    return x / (1.0 + jnp.exp(-x))


def _softplus(x):
    return jnp.maximum(x, 0.0) + jnp.log(1.0 + jnp.exp(-jnp.abs(x)))


def _dot(a, b):
    return jnp.dot(a, b, preferred_element_type=F32)


def _dot_nt(a, b):
    return lax.dot_general(a, b, (((1,), (1,)), ((), ())), preferred_element_type=F32)


def _dot_tn(a, b):
    return lax.dot_general(a, b, (((0,), (0,)), ((), ())), preferred_element_type=F32)


def _dot_01(m, x):
    hi = x.astype(BF16)
    rest = x - hi.astype(F32)
    mid = rest.astype(BF16)
    lo = (rest - mid.astype(F32)).astype(BF16)
    return _dot(m, hi) + _dot(m, mid) + _dot(m, lo)


def _dot_r01(x, m):
    hi = x.astype(BF16)
    rest = x - hi.astype(F32)
    mid = rest.astype(BF16)
    lo = (rest - mid.astype(F32)).astype(BF16)
    return _dot(hi, m) + _dot(mid, m) + _dot(lo, m)


def _rms_kernel(x_ref, w_ref, o_ref):
    x = x_ref[...]
    ms = jnp.mean(x * x, axis=-1, keepdims=True)
    o_ref[...] = (x * lax.rsqrt(ms + NORM_EPS) * w_ref[...]).astype(BF16)


def _rmsnorm_bf16(x, w, l, tm=512):
    t, d = x.shape
    return pl.pallas_call(
        _rms_kernel,
        grid=(t // tm,),
        in_specs=[pl.BlockSpec((tm, d), lambda i: (i, 0)), _layer((1, d), l)],
        out_specs=pl.BlockSpec((tm, d), lambda i: (i, 0)),
        out_shape=jax.ShapeDtypeStruct((t, d), BF16),
        compiler_params=_params("arbitrary"),
        name="rmsnorm_in",
    )(x, w)


def _mm_kernel(x_ref, wt_ref, o_ref):
    o_ref[...] = _dot_nt(x_ref[...], wt_ref[...])


def _in_proj(u, wt, l, name, tm=1024):
    t, k = u.shape
    n = wt.shape[1]
    return pl.pallas_call(
        _mm_kernel,
        grid=(t // tm,),
        in_specs=[pl.BlockSpec((tm, k), lambda i: (i, 0)), _layer((n, k), l)],
        out_specs=pl.BlockSpec((tm, n), lambda i: (i, 0)),
        out_shape=jax.ShapeDtypeStruct((t, n), F32),
        compiler_params=_params("arbitrary"),
        name=name,
    )(u, wt)


def _out_proj_kernel(ya_ref, yb_ref, yc_ref, yd_ref, w_ref, h_ref, nw_ref, ho_ref, uo_ref, wb_ref):
    @pl.when(pl.program_id(0) == 0)
    def _():
        for i in range(w_ref.shape[0] // GROUP_W):
            wb_ref[GROUP_W * i:GROUP_W * (i + 1), :] = w_ref[GROUP_W * i:GROUP_W * (i + 1), :].astype(BF16)

    sub = 256
    for r in range(h_ref.shape[0] // sub):
        rows = slice(sub * r, sub * (r + 1))
        acc = h_ref[rows, :]
        for i, y_ref in enumerate((ya_ref, yb_ref, yc_ref, yd_ref)):
            acc = acc + _dot(y_ref[rows, :], wb_ref[GROUP_W * i:GROUP_W * (i + 1), :])
        ho_ref[rows, :] = acc
        ms = jnp.mean(acc * acc, axis=-1, keepdims=True)
        uo_ref[rows, :] = (acc * lax.rsqrt(ms + NORM_EPS) * nw_ref[...]).astype(BF16)


def _out_proj(ys, w, h, nw, l, l_next, tm=512):
    t, d = h.shape
    yspec = pl.BlockSpec((tm, GROUP_W), lambda i: (i, 0))
    return pl.pallas_call(
        _out_proj_kernel,
        grid=(t // tm,),
        in_specs=[yspec, yspec, yspec, yspec,
                  pl.BlockSpec((None, d, d), lambda i: (l, 0, 0), pipeline_mode=pl.Buffered(1)),
                  pl.BlockSpec((tm, d), lambda i: (i, 0)),
                  _layer((1, d), l_next)],
        out_specs=[pl.BlockSpec((tm, d), lambda i: (i, 0)), pl.BlockSpec((tm, d), lambda i: (i, 0))],
        out_shape=[jax.ShapeDtypeStruct((t, d), F32), jax.ShapeDtypeStruct((t, d), BF16)],
        scratch_shapes=[pltpu.VMEM((d, d), BF16)],
        compiler_params=_params("arbitrary"),
        name="out_proj",
    )(*ys, w, h, nw)


def _ssd_kernel(p_ref, sh_ref, cw_ref, cb_ref, dtb_ref, alog_ref, d_ref, nw_ref, o_ref, cbuf, st_ref):
    L = SSD_CHUNK

    @pl.when(pl.program_id(1) == 0)
    def _():
        cbuf[0:8, :] = jnp.zeros((8, SSD_XBC), F32)
        st_ref[...] = jnp.zeros_like(st_ref)

    row = lax.broadcasted_iota(jnp.int32, (L, L), 0)
    col = lax.broadcasted_iota(jnp.int32, (L, L), 1)
    causal = (col <= row).astype(BF16)
    expand = (lax.broadcasted_iota(jnp.int32, (128, 512), 1) // SSD_HEAD_DIM + DT_LANE
              == lax.broadcasted_iota(jnp.int32, (128, 512), 0)).astype(BF16)
    lane = lax.broadcasted_iota(jnp.int32, (L, 128), 1)
    rowi = lax.broadcasted_iota(jnp.int32, (L, 128), 0)
    low = lane < 64
    low8 = lax.broadcasted_iota(jnp.int32, (8, 128), 1) < 64
    low64 = lax.broadcasted_iota(jnp.int32, (64, 128), 1) < 64
    keep = [rowi >= (lane % 64) + 64 * jh for jh in range(2)]
    blockdiag = (lax.broadcasted_iota(jnp.int32, (128, 128), 0) // 64
                 == lax.broadcasted_iota(jnp.int32, (128, 128), 1) // 64)

    def both_halves(x, g):
        r = pltpu.roll(x, 64, 1)
        return jnp.where(low, x, r) if g == 0 else jnp.where(low, r, x)

    def chunk(rows):
        z = p_ref[rows, 0:512]
        cbuf[8:8 + L, :] = p_ref[rows, 512:512 + SSD_XBC]
        ext = cbuf[...]
        acc = cb_ref[...] + ext[8:8 + L, :] * cw_ref[SSD_CONV - 1:SSD_CONV, :]
        for k in range(SSD_CONV - 1):
            acc = acc + pltpu.roll(ext, SSD_CONV - 1 - k, 0)[8:8 + L, :] * cw_ref[k:k + 1, :]
        cbuf[0:8, :] = ext[L:L + 8, :]
        xbc = _silu(acc)
        xs = xbc[:, 0:512]
        bm = xbc[:, 512:640]
        cm = xbc[:, 640:768]

        dt_c = _softplus(sh_ref[rows, :] + dtb_ref[...])
        cs_c = _dot_01(causal, dt_c * -jnp.exp(alog_ref[...]))
        dt = _dot_r01(dt_c, expand)
        cs = _dot_r01(cs_c, expand)
        cs_t = jnp.transpose(cs_c)
        cs_last = cs[L - 1:L, :]
        grow = jnp.exp(cs)
        tail = jnp.exp(cs_last - cs)
        total = jnp.exp(cs_last)
        xdt = (xs * dt).astype(BF16)
        bmb = bm.astype(BF16)

        groups, pairs, halves = range(2), range(4), range(2)
        cmask = [jnp.where(low if g == 0 else jnp.logical_not(low), cm, 0.0).astype(BF16) for g in groups]
        bhalf = [jnp.concatenate([bmb[64 * jh:64 * jh + 64, :]] * 2, axis=0) for jh in halves]
        gdup = [[_dot_nt(cmask[g], bhalf[jh]) for jh in halves] for g in groups]
        cdup = [both_halves(cm, g) for g in groups]
        bdup = [both_halves(bm, g) for g in groups]
        lanes = [slice(128 * pair, 128 * (pair + 1)) for pair in pairs]
        crow = []
        for pair in pairs:
            h0 = DT_LANE + 2 * pair
            r0 = jnp.broadcast_to(cs_t[h0:h0 + 1, :], (8, L))
            r1 = jnp.broadcast_to(cs_t[h0 + 1:h0 + 2, :], (8, L))
            crow.append([jnp.where(low8, r0, pltpu.roll(r1, 64, 1))[0:1, :],
                         jnp.where(low8, pltpu.roll(r0, 64, 1), r1)[0:1, :]])
        lmat = [[jnp.exp(jnp.where(keep[jh], cs[:, lanes[pair]] - crow[pair][jh], -jnp.inf)) for jh in halves]
                for pair in pairs]
        xbd = []
        for pair in pairs:
            xp = xdt[:, lanes[pair]]
            per_half = []
            for jh in halves:
                xj = xp[64 * jh:64 * jh + 64, :]
                zero = jnp.zeros_like(xj)
                per_half.append(jnp.concatenate([jnp.where(low64, xj, zero), jnp.where(low64, zero, xj)], axis=0))
            xbd.append(per_half)
        sts = [st_ref[pair] for pair in pairs]
        ys = []
        for pair in pairs:
            g = pair // 2
            y = _dot_nt((cdup[g] * grow[:, lanes[pair]]).astype(BF16), sts[pair].astype(BF16))
            for jh in halves:
                y = y + _dot((gdup[g][jh] * lmat[pair][jh]).astype(BF16), xbd[pair][jh])
            ys.append(y)
        for pair in pairs:
            upd = _dot_tn(xdt[:, lanes[pair]], (bdup[pair // 2] * tail[:, lanes[pair]]).astype(BF16))
            st_ref[pair] = total[:, lanes[pair]] * sts[pair] + jnp.where(blockdiag, upd, 0.0)
        y = jnp.concatenate(ys, axis=1) + d_ref[...] * xs
        y = y * _silu(z)
        ms = jnp.mean(y * y, axis=-1, keepdims=True)
        o_ref[rows, :] = (y * lax.rsqrt(ms + NORM_EPS) * nw_ref[...]).astype(BF16)

    for c in range(p_ref.shape[0] // L):
        chunk(slice(L * c, L * (c + 1)))


def _ssd_branch(l, p, shared, cw, cb, dtb, alog, dsk, nw, bsz, s):
    rows = SSD_CHUNK * SSD_CHUNKS_PER_STEP
    nc = s // rows
    return pl.pallas_call(
        _ssd_kernel,
        grid=(bsz, nc),
        in_specs=[pl.BlockSpec((rows, W_SSD), lambda b, c: (b * nc + c, 0)),
                  pl.BlockSpec((rows, 128), lambda b, c: (b * nc + c, SHARED_TILE)),
                  _layer((SSD_CONV, SSD_XBC), l), _layer((1, SSD_XBC), l),
                  _layer((1, 128), l), _layer((1, 128), l), _layer((1, 512), l), _layer((1, 512), l)],
        out_specs=pl.BlockSpec((rows, GROUP_W), lambda b, c: (b * nc + c, 0)),
        out_shape=jax.ShapeDtypeStruct((bsz * s, GROUP_W), BF16),
        scratch_shapes=[pltpu.VMEM((SSD_CHUNK + 8, SSD_XBC), F32),
                        pltpu.VMEM((SSD_HEADS // 2, 2 * SSD_HEAD_DIM, 2 * SSD_STATE), F32)],
        compiler_params=_params("arbitrary", "arbitrary"),
        name="ssd_branch",
    )(p, shared, cw, cb, dtb, alog, dsk, nw)


def _s5_prep_kernel(are_ref, aim_ref, ldt_ref, bre_ref, bim_ref, cre_ref, cim_ref, bb_ref, cc_ref, tab_ref):
    are = are_ref[...]
    aim = aim_ref[...]
    delta = jnp.exp(ldt_ref[...])
    mag = jnp.exp(are * delta)
    ar = mag * jnp.cos(aim * delta)
    ai = mag * jnp.sin(aim * delta)
    den = are * are + aim * aim
    coef_re = ((ar - 1.0) * are + ai * aim) / den
    coef_im = (ai * are - (ar - 1.0) * aim) / den

    bb_ref[...] = jnp.zeros_like(bb_ref)
    cc_ref[...] = jnp.zeros_like(cc_ref)
    for g in range(S5_GROUPS):
        kb, gl = divmod(g, 16)
        st = slice(S5_STATE * g, S5_STATE * (g + 1))
        rows = slice(S5_CH * gl, S5_CH * (gl + 1))
        re = slice(S5_STATE * gl, S5_STATE * (gl + 1))
        im = slice(S5_HALF + S5_STATE * gl, S5_HALF + S5_STATE * (gl + 1))
        bre = bre_ref[g]
        bim = bim_ref[g]
        bb_ref[kb, rows, re] = (coef_re[:, st] * bre - coef_im[:, st] * bim).astype(BF16)
        bb_ref[kb, rows, im] = (coef_re[:, st] * bim + coef_im[:, st] * bre).astype(BF16)
        cc_ref[kb, re, rows] = cre_ref[g].astype(BF16)
        cc_ref[kb, im, rows] = (-cim_ref[g]).astype(BF16)

    odd = lax.broadcasted_iota(jnp.int32, (8, S5_HALF), 0) % 2 == 1
    tab_ref[0] = jnp.where(odd, ar[:, S5_HALF:], ar[:, :S5_HALF])
    tab_ref[1] = jnp.where(odd, ai[:, S5_HALF:], ai[:, :S5_HALF])


def _s5_prep(l, are, aim, ldt, bre, bim, cre, cim):
    out3 = lambda shape: pl.BlockSpec(shape, lambda i: (0, 0, 0))
    return pl.pallas_call(
        _s5_prep_kernel,
        grid=(1,),
        in_specs=[_layer((1, S5_NS), l), _layer((1, S5_NS), l), _layer((1, S5_NS), l),
                  _layer((S5_GROUPS, S5_CH, S5_STATE), l), _layer((S5_GROUPS, S5_CH, S5_STATE), l),
                  _layer((S5_GROUPS, S5_STATE, S5_CH), l), _layer((S5_GROUPS, S5_STATE, S5_CH), l)],
        out_specs=[out3((2, 256, 2 * S5_HALF)), out3((2, 2 * S5_HALF, 256)), out3((2, 8, S5_HALF))],
        out_shape=[jax.ShapeDtypeStruct((2, 256, 2 * S5_HALF), BF16),
                   jax.ShapeDtypeStruct((2, 2 * S5_HALF, 256), BF16),
                   jax.ShapeDtypeStruct((2, 8, S5_HALF), F32)],
        compiler_params=_params("arbitrary"),
        name="s5_prep",
    )(are, aim, ldt, bre, bim, cre, cim)


def _s5_kernel(p_ref, bb_ref, cc_ref, tab_ref, d_ref, gw_ref, gb_ref, o_ref, pbuf, hbuf, ybuf, carry):
    nb = p_ref.shape[0]
    rt = S5_TSTEP
    n = nb * rt

    @pl.when(pl.program_id(0) == 0)
    def _():
        carry[...] = jnp.zeros_like(carry)

    def cat(ref, blocks, rows):
        return jnp.concatenate([ref[c, rows, :] for c in blocks], axis=1)

    for b in range(nb):
        for c in range(W_S5 // 128):
            pbuf[c, pl.ds(b, rt, stride=nb), :] = p_ref[b, :, 128 * c:128 * (c + 1)]
    u = cat(pbuf, range(0, 4), slice(None))
    ub = u.astype(BF16)
    nblk = 2 * S5_HALF // 128
    for j in range(2):
        hj = _dot(ub[:, 256 * j:256 * (j + 1)], bb_ref[j])
        for c in range(nblk):
            hbuf[c, pl.ds(j, n, stride=2), :] = hj[:, 128 * c:128 * (c + 1)]

    wb = S5_SCAN_LANES // 128
    for lc in range(S5_HALF // S5_SCAN_LANES):
        re_blocks = list(range(wb * lc, wb * (lc + 1)))
        im_blocks = [c + nblk // 2 for c in re_blocks]
        lanes = slice(S5_SCAN_LANES * lc, S5_SCAN_LANES * (lc + 1))
        ar = tab_ref[0, :, lanes]
        ai = tab_ref[1, :, lanes]

        def body(t, c, re_blocks=re_blocks, im_blocks=im_blocks, ar=ar, ai=ai):
            hr, hi = c
            rows = pl.ds(pl.multiple_of(t * 8, 8), 8)
            hr, hi = (ar * hr - ai * hi + cat(hbuf, re_blocks, rows),
                      ar * hi + ai * hr + cat(hbuf, im_blocks, rows))
            for k in range(wb):
                hbuf[re_blocks[k], rows, :] = hr[:, 128 * k:128 * (k + 1)]
                hbuf[im_blocks[k], rows, :] = hi[:, 128 * k:128 * (k + 1)]
            return hr, hi

        re = slice(S5_SCAN_LANES * lc, S5_SCAN_LANES * (lc + 1))
        im = slice(S5_HALF + re.start, S5_HALF + re.stop)
        hr, hi = lax.fori_loop(0, rt, body, (carry[:, re], carry[:, im]), unroll=4)
        carry[:, re] = hr
        carry[:, im] = hi

    ys = [_dot(cat(hbuf, range(nblk), pl.ds(j, n, stride=2)).astype(BF16), cc_ref[j]) for j in range(2)]
    y = jnp.concatenate(ys, axis=1) + d_ref[...] * u
    y = 0.5 * y * (1.0 + jnp.tanh(math.sqrt(2.0 / math.pi) * (y + 0.044715 * (y * y * y))))
    glu = _dot(y.astype(BF16), gw_ref[...]) + gb_ref[...]
    y = y / (1.0 + jnp.exp(-glu))
    y = y * _silu(cat(pbuf, range(4, 8), slice(None)))
    for c in range(GROUP_W // 128):
        ybuf[c] = y[:, 128 * c:128 * (c + 1)]
    for b in range(nb):
        o_ref[b] = cat(ybuf, range(GROUP_W // 128), pl.ds(b, rt, stride=nb)).astype(BF16)


def _s5_branch(l, p, bb, cc, tab, dsk, gw, gb, bsz, s):
    assert 2 * bsz == 8, "scan rows (batch, state block) must fill the 8 sublanes"
    const = lambda shape: pl.BlockSpec(shape, lambda t: tuple(0 for _ in shape))
    n = bsz * S5_TSTEP
    out = pl.pallas_call(
        _s5_kernel,
        grid=(s // S5_TSTEP,),
        in_specs=[pl.BlockSpec((bsz, S5_TSTEP, W_S5), lambda t: (0, t, 0)),
                  const((2, 256, 2 * S5_HALF)), const((2, 2 * S5_HALF, 256)), const((2, 8, S5_HALF)),
                  _layer((1, GROUP_W), l), _layer((GROUP_W, GROUP_W), l), _layer((1, GROUP_W), l)],
        out_specs=pl.BlockSpec((bsz, S5_TSTEP, GROUP_W), lambda t: (0, t, 0)),
        out_shape=jax.ShapeDtypeStruct((bsz, s, GROUP_W), BF16),
        scratch_shapes=[pltpu.VMEM((W_S5 // 128, n, 128), F32),
                        pltpu.VMEM((2 * S5_HALF // 128, 2 * n, 128), F32),
                        pltpu.VMEM((GROUP_W // 128, n, 128), F32),
                        pltpu.VMEM((8, 2 * S5_HALF), F32)],
        compiler_params=_params("arbitrary"),
        name="s5_branch",
    )(p.reshape(bsz, s, W_S5), bb, cc, tab, dsk, gw, gb)
    return out.reshape(bsz * s, GROUP_W)


def _gla_kernel(p_ref, sh_ref, w2_ref, gb_ref, nw_ref, o_ref, st_ref):
    T = GLA_TILE
    NCH = T // GLA_CHUNK

    @pl.when(pl.program_id(1) == 0)
    def _():
        st_ref[...] = jnp.zeros_like(st_ref)

    row = lax.broadcasted_iota(jnp.int32, (T, T), 0)
    col = lax.broadcasted_iota(jnp.int32, (T, T), 1)
    same = (row // GLA_CHUNK) == (col // GLA_CHUNK)
    intra = jnp.logical_and(same, col <= row)
    same_b, intra_b = same.astype(BF16), intra.astype(BF16)
    wide = (T, NCH * 128)
    blockmask = (lax.broadcasted_iota(jnp.int32, wide, 0) // GLA_CHUNK
                 == lax.broadcasted_iota(jnp.int32, wide, 1) // 128)
    wide_head = (lax.broadcasted_iota(jnp.int32, wide, 1) % 128) // GLA_DK
    lane_head = lax.broadcasted_iota(jnp.int32, (T, 128), 1) // GLA_DK
    zero = jnp.zeros((), BF16)
    scale = GLA_DK ** -0.5
    states = [st_ref[pair] for pair in range(GLA_HEADS // 2)]

    subs = range(p_ref.shape[0] // T)
    pairs = range(GLA_HEADS // 2)
    rows = [slice(T * sub, T * (sub + 1)) for sub in subs]
    xs = [_dot(sh_ref[rows[s], :].astype(BF16), w2_ref[...]) + gb_ref[...] for s in subs]
    gs = [-_softplus(-x) * (1.0 / GLA_GATE_NORM) for x in xs]
    bs = [_dot_01(intra_b, g) for g in gs]
    blasts = [_dot_01(same_b, g) for g in gs]
    q_mid, k_mid, q_dec, k_dec = [], [], [], []
    for s in subs:
        q = p_ref[rows[s], 0:256] * scale
        k = p_ref[rows[s], 256:512]
        half = 0.5 * blasts[s]
        q_mid.append((q * jnp.exp(bs[s] - half)).astype(BF16))
        k_mid.append((k * jnp.exp(half - bs[s])).astype(BF16))
        q_dec.append((q * jnp.exp(bs[s])).astype(BF16))
        k_dec.append((k * jnp.exp(blasts[s] - bs[s])).astype(BF16))

    lanes = [slice(128 * pair, 128 * (pair + 1)) for pair in pairs]
    vs = [[p_ref[rows[s], 512 + GLA_DV * h:512 + GLA_DV * (h + 1)].astype(BF16) for h in range(GLA_HEADS)]
          for s in subs]
    ds = {}
    for s in subs:
        for pair in pairs:
            kcat = jnp.where(blockmask, jnp.tile(k_dec[s][:, lanes[pair]], (1, NCH)), zero)
            ds[s, pair] = sum(_dot_tn(vs[s][2 * pair + hh], jnp.where(wide_head == hh, kcat, zero))
                              for hh in range(2))
    st_all = {}
    for s in subs:
        for pair in pairs:
            st = states[pair]
            sts = []
            for c in range(NCH):
                sts.append(st)
                dec = jnp.exp(blasts[s][GLA_CHUNK * c:GLA_CHUNK * c + 1, lanes[pair]])
                st = dec * st + ds[s, pair][:, 128 * c:128 * (c + 1)]
            states[pair] = st
            st_all[s, pair] = jnp.concatenate(sts, axis=1).astype(BF16)
    for pair in pairs:
        st_ref[pair] = states[pair]

    combos = [(s, pair, hh) for s in subs for pair in pairs for hh in range(2)]
    qcat = {(s, pair): jnp.where(blockmask, jnp.tile(q_dec[s][:, lanes[pair]], (1, NCH)), zero)
            for s in subs for pair in pairs}
    attn = {(s, pair, hh): _dot_nt(jnp.where(lane_head == hh, q_mid[s][:, lanes[pair]], zero),
                                   k_mid[s][:, lanes[pair]]) for s, pair, hh in combos}
    inter = {(s, pair, hh): _dot_nt(jnp.where(wide_head == hh, qcat[s, pair], zero), st_all[s, pair])
             for s, pair, hh in combos}
    outs = {(s, pair, hh): inter[s, pair, hh] + _dot(jnp.where(intra, attn[s, pair, hh], 0.0).astype(BF16),
                                                      vs[s][2 * pair + hh]) for s, pair, hh in combos}
    for s, pair, hh in combos:
        h = 2 * pair + hh
        o = outs[s, pair, hh]
        ms = jnp.mean(o * o, axis=-1, keepdims=True)
        o = o * lax.rsqrt(ms + NORM_EPS) * nw_ref[...]
        gate = p_ref[rows[s], 1024 + GLA_DV * h:1024 + GLA_DV * (h + 1)]
        o_ref[rows[s], GLA_DV * h:GLA_DV * (h + 1)] = (o * _silu(gate)).astype(BF16)


def _gla_branch(l, p, shared, w2, gb, nw, bsz, s):
    rows = GLA_TILE * GLA_TILES_PER_STEP
    nt = s // rows
    return pl.pallas_call(
        _gla_kernel,
        grid=(bsz, nt),
        in_specs=[pl.BlockSpec((rows, W_GLA), lambda b, t: (b * nt + t, 0)),
                  pl.BlockSpec((rows, 128), lambda b, t: (b * nt + t, SHARED_TILE)),
                  _layer((128, 256), l), _layer((1, 256), l), _layer((1, GLA_DV), l)],
        out_specs=pl.BlockSpec((rows, GROUP_W), lambda b, t: (b * nt + t, 0)),
        out_shape=jax.ShapeDtypeStruct((bsz * s, GROUP_W), BF16),
        scratch_shapes=[pltpu.VMEM((GLA_HEADS // 2, GLA_DV, 2 * GLA_DK), F32)],
        compiler_params=_params("arbitrary", "arbitrary"),
        name="gla_branch",
    )(p, shared, w2, gb, nw)


def _rope_swap(t):
    return pltpu.roll(t, 64, 1)


def _spread_rope(w):
    half = MLA_ROPE // 2
    zeros = jnp.zeros(w.shape[:-1] + (64 - half,), w.dtype)
    return jnp.concatenate([w[..., :half], zeros, w[..., half:], zeros], axis=-1)


def _rope_kernel(pos_ref, invf_ref, sign_ref, cos_ref, sin_ref):
    ang = pos_ref[...].astype(F32) * invf_ref[...]
    cos_ref[...] = jnp.cos(ang)
    sin_ref[...] = jnp.sin(ang) * sign_ref[...]


def _rope_tables(pos, invf, sign, tm=512):
    t = pos.shape[0]
    const = pl.BlockSpec((1, 128), lambda i: (0, 0))
    tile = pl.BlockSpec((tm, 128), lambda i: (i, 0))
    return pl.pallas_call(
        _rope_kernel,
        grid=(t // tm,),
        in_specs=[pl.BlockSpec((tm, 1), lambda i: (i, 0)), const, const],
        out_specs=[tile, tile],
        out_shape=[jax.ShapeDtypeStruct((t, 128), F32)] * 2,
        compiler_params=_params("arbitrary"),
        name="rope_tables",
    )(pos, invf, sign)


def _mla_prep_kernel(p_ref, cos_ref, sin_ref, qnw_ref, wuq_ref, kvnw_ref, wukv_ref,
                     qhw_ref, khw_ref, q_ref, k_ref, vt_ref):
    inv_d = 1.0 / (MLA_NOPE + MLA_ROPE)
    scale = (MLA_NOPE + MLA_ROPE) ** -0.5 * math.log2(math.e)
    qhw = qhw_ref[...]
    khw = khw_ref[...]
    sub = 256
    lane = lax.broadcasted_iota(jnp.int32, (sub, 128), 1)
    rope_lanes = (lane % 64) < MLA_ROPE // 2
    for r0 in range(0, p_ref.shape[0], sub):
        rows = slice(r0, r0 + sub)
        cq = p_ref[rows, 512:896]
        ckv = p_ref[rows, 896:1024]
        kpe = jnp.where(rope_lanes, p_ref[rows, 1024:1152], 0.0)
        ms = jnp.mean(cq * cq, axis=-1, keepdims=True)
        qn = (cq * lax.rsqrt(ms + NORM_EPS) * qnw_ref[...]).astype(BF16)
        ms = jnp.mean(ckv * ckv, axis=-1, keepdims=True)
        kvn = (ckv * lax.rsqrt(ms + NORM_EPS) * kvnw_ref[...]).astype(BF16)
        cos_t = cos_ref[rows, :]
        sin_t = sin_ref[rows, :]
        kpe_ss = jnp.sum(kpe * kpe, axis=-1, keepdims=True)
        heads = range(MLA_HEADS)
        hp = MLA_HEAD_PAD
        qs = [_dot(qn, wuq_ref[:, hp * h:hp * (h + 1)]) for h in heads]
        kvs = [_dot(kvn, wukv_ref[:, hp * h:hp * (h + 1)]) for h in heads]
        q_ss = [jnp.sum(q * q, axis=-1, keepdims=True) for q in qs]
        k_ss = [jnp.sum(kv[:, 0:128] * kv[:, 0:128], axis=-1, keepdims=True) for kv in kvs]
        q_rs = [lax.rsqrt(ss * inv_d + NORM_EPS) * scale for ss in q_ss]
        k_rs = [lax.rsqrt((ss + kpe_ss) * inv_d + NORM_EPS) for ss in k_ss]
        q_rot = [qs[h][:, 128:256] * q_rs[h] * qhw[:, 128:256] for h in heads]
        k_rot = [kpe * k_rs[h] * khw[:, 128:256] for h in heads]
        q_swap = [_rope_swap(t) for t in q_rot]
        k_swap = [_rope_swap(t) for t in k_rot]
        for h in heads:
            lo = hp * h
            q_ref[rows, lo:lo + 128] = (qs[h][:, 0:128] * q_rs[h] * qhw[:, 0:128]).astype(BF16)
            q_ref[rows, lo + 128:lo + 256] = (q_rot[h] * cos_t + q_swap[h] * sin_t).astype(BF16)
            k_ref[rows, lo:lo + 128] = (kvs[h][:, 0:128] * k_rs[h] * khw[:, 0:128]).astype(BF16)
            k_ref[rows, lo + 128:lo + 256] = (k_rot[h] * cos_t + k_swap[h] * sin_t).astype(BF16)
            vt_ref[MLA_V * h:MLA_V * (h + 1), rows] = jnp.transpose(kvs[h][:, 128:256]).astype(BF16)


def _mla_prep(l, p, cos_t, sin_t, qnw, wuq, kvnw, wukv, qhw, khw, tm=512):
    t = p.shape[0]
    hp = MLA_HEADS * MLA_HEAD_PAD
    return pl.pallas_call(
        _mla_prep_kernel,
        grid=(t // tm,),
        in_specs=[pl.BlockSpec((tm, W_MLA), lambda i: (i, 0)),
                  pl.BlockSpec((tm, 128), lambda i: (i, 0)), pl.BlockSpec((tm, 128), lambda i: (i, 0)),
                  _layer((1, MLA_Q_RANK), l), _layer((MLA_Q_RANK, hp), l),
                  _layer((1, MLA_KV_RANK), l), _layer((MLA_KV_RANK, hp), l),
                  _layer((1, MLA_HEAD_PAD), l), _layer((1, MLA_HEAD_PAD), l)],
        out_specs=[pl.BlockSpec((tm, hp), lambda i: (i, 0)),
                   pl.BlockSpec((tm, hp), lambda i: (i, 0)),
                   pl.BlockSpec((MLA_HEADS * MLA_V, tm), lambda i: (0, i))],
        out_shape=[jax.ShapeDtypeStruct((t, hp), BF16),
                   jax.ShapeDtypeStruct((t, hp), BF16),
                   jax.ShapeDtypeStruct((MLA_HEADS * MLA_V, t), BF16)],
        compiler_params=_params("arbitrary"),
        name="mla_prep",
    )(p, cos_t, sin_t, qnw, wuq, kvnw, wukv, qhw, khw)


def _flash_kernel(q_ref, k_ref, vt_ref, g_ref, o_ref, s_ref):
    tq = FLASH_TQ
    hp = MLA_HEAD_PAD
    keep = (lax.broadcasted_iota(jnp.int32, (tq, tq), 0)
            <= lax.broadcasted_iota(jnp.int32, (tq, tq), 1))
    heads = range(MLA_HEADS)

    def attend(nb):
        n = nb * tq
        scores = [_dot_nt(k_ref[0:n, hp * h:hp * (h + 1)], q_ref[:, hp * h:hp * (h + 1)]) for h in heads]
        m = []
        for h in heads:
            blocks = [scores[h][tq * j:tq * (j + 1), :] for j in range(nb)]
            blocks[-1] = jnp.where(keep, blocks[-1], -jnp.inf)
            top = blocks[0]
            for j in range(nb):
                s_ref[h, tq * j:tq * (j + 1), :] = blocks[j]
                top = jnp.maximum(top, blocks[j])
            m.append(jnp.max(top, axis=0, keepdims=True))
        pt = [jnp.exp2(s_ref[h, 0:n, :] - m[h]) for h in heads]
        l = [jnp.sum(p, axis=0, keepdims=True) for p in pt]
        acc = [_dot(vt_ref[MLA_V * h:MLA_V * (h + 1), 0:n], pt[h].astype(BF16)) for h in heads]
        for h in heads:
            o = jnp.transpose(acc[h] / l[h])
            o_ref[:, MLA_V * h:MLA_V * (h + 1)] = (o * _silu(g_ref[:, MLA_V * h:MLA_V * (h + 1)])).astype(BF16)

    for nb in range(1, s_ref.shape[1] // tq + 1):
        pl.when(pl.program_id(1) == nb - 1)(functools.partial(attend, nb))


def _flash(q, k, vt, p, bsz, s):
    nq = s // FLASH_TQ
    hp = MLA_HEADS * MLA_HEAD_PAD
    hv = MLA_HEADS * MLA_V
    return pl.pallas_call(
        _flash_kernel,
        grid=(bsz, nq),
        in_specs=[pl.BlockSpec((FLASH_TQ, hp), lambda b, i: (b * nq + i, 0)),
                  pl.BlockSpec((s, hp), lambda b, i: (b, 0)),
                  pl.BlockSpec((hv, s), lambda b, i: (0, b)),
                  pl.BlockSpec((FLASH_TQ, GROUP_W), lambda b, i: (b * nq + i, 0))],
        out_specs=pl.BlockSpec((FLASH_TQ, hv), lambda b, i: (b * nq + i, 0)),
        out_shape=jax.ShapeDtypeStruct((bsz * s, hv), BF16),
        scratch_shapes=[pltpu.VMEM((MLA_HEADS, s, FLASH_TQ), F32)],
        compiler_params=_params("arbitrary", "arbitrary"),
        name="mla_flash",
    )(q, k, vt, p)


def _regroup_kernel(wt_ref, ssd_ref, s5_ref, gla_ref, mla_ref):
    offs = [0]
    for wd in IN_WIDTHS:
        offs.append(offs[-1] + wd)
    (z, xbc, dt, s5u, s5g, gq, gk, gv, gg, glr, cq, ckv, kpe, mg) = [
        (a, b) for a, b in zip(offs[:-1], offs[1:])]

    def put(dst, at, seg):
        dst[at:at + seg[1] - seg[0], :] = wt_ref[seg[0]:seg[1], :].astype(BF16)

    put(ssd_ref, 0, z)
    put(ssd_ref, 512, xbc)
    put(s5_ref, 0, s5u)
    put(s5_ref, 512, s5g)
    put(gla_ref, 0, gq)
    put(gla_ref, 256, gk)
    put(gla_ref, 512, gv)
    put(gla_ref, 1024, gg)
    put(mla_ref, 0, mg)
    put(mla_ref, 512, cq)
    put(mla_ref, 896, ckv)
    half = MLA_ROPE // 2
    zeros = lambda n: jnp.zeros((n, wt_ref.shape[1]), F32)
    shared = jnp.concatenate([wt_ref[kpe[0]:kpe[0] + half, :], wt_ref[dt[0]:dt[1], :], zeros(24),
                              wt_ref[kpe[0] + half:kpe[1], :], wt_ref[glr[0]:glr[1], :], zeros(16)], axis=0)
    mla_ref[1024:1152, :] = shared.astype(BF16)


def _pad_rows(x, n):
    return jnp.concatenate([x, jnp.zeros((n - x.shape[0], x.shape[1]), x.dtype)], axis=0)


def _regroup_w_in(w_in, tc=256):
    wt = jnp.swapaxes(w_in, 1, 2)
    depth, n, k = wt.shape
    widths = (W_SSD, W_S5, W_GLA, W_MLA)
    return pl.pallas_call(
        _regroup_kernel,
        grid=(depth, k // tc),
        in_specs=[pl.BlockSpec((None, n, tc), lambda l, i: (l, 0, i))],
        out_specs=[pl.BlockSpec((None, w, tc), lambda l, i: (l, 0, i)) for w in widths],
        out_shape=[jax.ShapeDtypeStruct((depth, w, k), BF16) for w in widths],
        compiler_params=_params("arbitrary", "arbitrary"),
        name="regroup_w_in",
    )(wt)


def _pad_last(w, n):
    return jnp.pad(w, [(0, 0)] * (w.ndim - 1) + [(0, n - w.shape[-1])])


def kernel(x, positions, norm_w, w_in, w_out, ssd_conv_w, ssd_conv_b, ssd_dt_bias, ssd_a_log, ssd_d, ssd_norm_w, s5_a_re, s5_a_im, s5_log_dt, s5_b_re, s5_b_im, s5_c_re, s5_c_im, s5_d, s5_glu_w, s5_glu_b, gla_gate_w2, gla_gate_b, gla_norm_w, mla_q_norm_w, mla_w_uq, mla_kv_norm_w, mla_w_ukv, mla_q_head_norm_w, mla_k_head_norm_w):
    bsz, s, d = x.shape
    depth = w_in.shape[0]
    t = bsz * s
    h = x.reshape(t, d)
    pos = positions.reshape(t, 1)
    row = lambda v: v[:, None, :]

    inv_freq = ROPE_THETA ** (-jnp.arange(0, MLA_ROPE, 2, dtype=F32) / MLA_ROPE)
    invf = _spread_rope(jnp.concatenate([inv_freq, inv_freq]))[None, :]
    sign = _spread_rope(jnp.concatenate([-jnp.ones((32,), F32), jnp.ones((32,), F32)]))[None, :]

    w_ssd, w_s5, w_gla, w_mla = _regroup_w_in(w_in)
    norm_w3 = row(norm_w)

    ssd_cb, ssd_nw = row(ssd_conv_b), row(ssd_norm_w)
    at_dt = lambda v: row(jnp.pad(v, ((0, 0), (DT_LANE, 128 - DT_LANE - SSD_HEADS))))
    ssd_dtb, ssd_alog = at_dt(ssd_dt_bias), at_dt(ssd_a_log)
    ssd_dsk = row(jnp.repeat(ssd_d, SSD_HEAD_DIM, axis=1))

    s5_are = s5_a_re.reshape(depth, 1, S5_NS)
    s5_aim = s5_a_im.reshape(depth, 1, S5_NS)
    s5_ldt = row(jnp.repeat(s5_log_dt, S5_STATE, axis=1))
    s5_bre, s5_bim = s5_b_re.transpose(0, 1, 3, 2), s5_b_im.transpose(0, 1, 3, 2)
    s5_cre, s5_cim = s5_c_re.transpose(0, 1, 3, 2), s5_c_im.transpose(0, 1, 3, 2)
    s5_dsk, s5_gw, s5_gb = row(s5_d), s5_glu_w.astype(BF16), row(s5_glu_b)

    gla_w2 = jnp.pad(gla_gate_w2, ((0, 0), (GLR_LANE, 128 - GLR_LANE - gla_gate_w2.shape[1]), (0, 0))).astype(BF16)
    gla_gb, gla_nw = row(gla_gate_b), row(gla_norm_w)

    head_pad = lambda w: jnp.concatenate([w[..., :MLA_NOPE], _spread_rope(w[..., MLA_NOPE:])], axis=-1)
    wuq = head_pad(mla_w_uq.reshape(depth, MLA_Q_RANK, MLA_HEADS, MLA_NOPE + MLA_ROPE))
    wuq = wuq.reshape(depth, MLA_Q_RANK, MLA_HEADS * MLA_HEAD_PAD).astype(BF16)
    wukv = mla_w_ukv.astype(BF16)
    qnw, kvnw = row(mla_q_norm_w), row(mla_kv_norm_w)
    qhw, khw = row(head_pad(mla_q_head_norm_w)), row(head_pad(mla_k_head_norm_w))

    cos_t, sin_t = _rope_tables(pos, invf, sign)
    u = _rmsnorm_bf16(h, norm_w3, 0)
    for l in range(depth):
        p_ssd = _in_proj(u, w_ssd, l, "in_proj_ssd")
        p_s5 = _in_proj(u, w_s5, l, "in_proj_s5")
        p_gla = _in_proj(u, w_gla, l, "in_proj_gla")
        p_mla = _in_proj(u, w_mla, l, "in_proj_mla")

        y_a = _ssd_branch(l, p_ssd, p_mla, ssd_conv_w, ssd_cb, ssd_dtb, ssd_alog, ssd_dsk, ssd_nw, bsz, s)

        bb, cc, tab = _s5_prep(l, s5_are, s5_aim, s5_ldt, s5_bre, s5_bim, s5_cre, s5_cim)
        y_b = _s5_branch(l, p_s5, bb, cc, tab, s5_dsk, s5_gw, s5_gb, bsz, s)

        y_c = _gla_branch(l, p_gla, p_mla, gla_w2, gla_gb, gla_nw, bsz, s)

        q, k, vt = _mla_prep(l, p_mla, cos_t, sin_t, qnw, wuq, kvnw, wukv, qhw, khw)
        y_d = _flash(q, k, vt, p_mla, bsz, s)

        h, u = _out_proj((y_a, y_b, y_c, y_d), w_out, h, norm_w3, l, (l + 1) % depth)
    return h.reshape(bsz, s, d)
```

```python
import functools
import math

import jax
import jax.numpy as jnp
from jax import lax
from jax.experimental import pallas as pl
from jax.experimental.pallas import tpu as pltpu

F32 = jnp.float32
BF16 = jnp.bfloat16
NORM_EPS = 1e-6
HI = lax.Precision.HIGHEST

D_MODEL = 2048
GROUP_W = 512
SSD_HEADS = 8
SSD_HEAD_DIM = 64
SSD_STATE = 64
SSD_CHUNK = 128
SSD_CHUNKS_PER_STEP = 4
SSD_XBC = 768
SSD_CONV = 4
S5_GROUPS = 32
S5_CH = 16
S5_STATE = 64
S5_NS = S5_GROUPS * S5_STATE
S5_HALF = S5_NS // 2
S5_TSTEP = 128
S5_SCAN_LANES = 512
GLA_HEADS = 4
GLA_DK = 64
GLA_DV = 128
GLA_CHUNK = 16
GLA_TILE = 128
GLA_TILES_PER_STEP = 4
GLA_GATE_NORM = 16.0
MLA_HEADS = 4
MLA_NOPE = 128
MLA_ROPE = 64
MLA_V = 128
MLA_Q_RANK = 384
MLA_KV_RANK = 128
MLA_HEAD_PAD = 256
ROPE_THETA = 10000.0
FLASH_TQ = 256

IN_WIDTHS = (512, 768, 8, 512, 512, 256, 256, 512, 512, 16, 384, 128, 64, 512)
W_SSD = 512 + 768
W_S5 = 1024
W_GLA = 256 + 256 + 512 + 512
W_MLA = 512 + 384 + 128 + 128
SHARED_TILE = W_MLA // 128 - 1
DT_LANE = 32
GLR_LANE = 96

VMEM_LIMIT_BYTES = 56 * 1024 * 1024


def _params(*sem):
    return pltpu.CompilerParams(dimension_semantics=sem, vmem_limit_bytes=VMEM_LIMIT_BYTES)


def _layer(shape, l):
    return pl.BlockSpec((None,) + tuple(shape), lambda *_: (l,) + (0,) * len(shape))


def _silu(x):
    return x / (1.0 + jnp.exp(-x))


def _softplus(x):
    return jnp.maximum(x, 0.0) + jnp.log(1.0 + jnp.exp(-jnp.abs(x)))


def _dot(a, b):
    return jnp.dot(a, b, preferred_element_type=F32)


def _dot_nt(a, b):
    return lax.dot_general(a, b, (((1,), (1,)), ((), ())), preferred_element_type=F32)


def _dot_tn(a, b):
    return lax.dot_general(a, b, (((0,), (0,)), ((), ())), preferred_element_type=F32)


def _dot_01(m, x):
    hi = x.astype(BF16)
    rest = x - hi.astype(F32)
    mid = rest.astype(BF16)
    lo = (rest - mid.astype(F32)).astype(BF16)
    return _dot(m, hi) + _dot(m, mid) + _dot(m, lo)


def _dot_r01(x, m):
    hi = x.astype(BF16)
    rest = x - hi.astype(F32)
    mid = rest.astype(BF16)
    lo = (rest - mid.astype(F32)).astype(BF16)
    return _dot(hi, m) + _dot(mid, m) + _dot(lo, m)


def _rms_kernel(x_ref, w_ref, o_ref):
    x = x_ref[...]
    ms = jnp.mean(x * x, axis=-1, keepdims=True)
    o_ref[...] = (x * lax.rsqrt(ms + NORM_EPS) * w_ref[...]).astype(BF16)


def _rmsnorm_bf16(x, w, l, tm=512):
    t, d = x.shape
    return pl.pallas_call(
        _rms_kernel,
        grid=(t // tm,),
        in_specs=[pl.BlockSpec((tm, d), lambda i: (i, 0)), _layer((1, d), l)],
        out_specs=pl.BlockSpec((tm, d), lambda i: (i, 0)),
        out_shape=jax.ShapeDtypeStruct((t, d), BF16),
        compiler_params=_params("arbitrary"),
        name="rmsnorm_in",
    )(x, w)


def _mm_kernel(x_ref, wt_ref, o_ref):
    o_ref[...] = _dot_nt(x_ref[...], wt_ref[...])


def _in_proj(u, wt, l, name, tm=1024):
    t, k = u.shape
    n = wt.shape[1]
    return pl.pallas_call(
        _mm_kernel,
        grid=(t // tm,),
        in_specs=[pl.BlockSpec((tm, k), lambda i: (i, 0)), _layer((n, k), l)],
        out_specs=pl.BlockSpec((tm, n), lambda i: (i, 0)),
        out_shape=jax.ShapeDtypeStruct((t, n), F32),
        compiler_params=_params("arbitrary"),
        name=name,
    )(u, wt)


def _out_proj_kernel(ya_ref, yb_ref, yc_ref, yd_ref, w_ref, h_ref, nw_ref, ho_ref, uo_ref, wb_ref):
    @pl.when(pl.program_id(0) == 0)
    def _():
        for i in range(w_ref.shape[0] // GROUP_W):
            wb_ref[GROUP_W * i:GROUP_W * (i + 1), :] = w_ref[GROUP_W * i:GROUP_W * (i + 1), :].astype(BF16)

    sub = 256
    for r in range(h_ref.shape[0] // sub):
        rows = slice(sub * r, sub * (r + 1))
        acc = h_ref[rows, :]
        for i, y_ref in enumerate((ya_ref, yb_ref, yc_ref, yd_ref)):
            acc = acc + _dot(y_ref[rows, :], wb_ref[GROUP_W * i:GROUP_W * (i + 1), :])
        ho_ref[rows, :] = acc
        ms = jnp.mean(acc * acc, axis=-1, keepdims=True)
        uo_ref[rows, :] = (acc * lax.rsqrt(ms + NORM_EPS) * nw_ref[...]).astype(BF16)


def _out_proj(ys, w, h, nw, l, l_next, tm=512):
    t, d = h.shape
    yspec = pl.BlockSpec((tm, GROUP_W), lambda i: (i, 0))
    return pl.pallas_call(
        _out_proj_kernel,
        grid=(t // tm,),
        in_specs=[yspec, yspec, yspec, yspec,
                  pl.BlockSpec((None, d, d), lambda i: (l, 0, 0), pipeline_mode=pl.Buffered(1)),
                  pl.BlockSpec((tm, d), lambda i: (i, 0)),
                  _layer((1, d), l_next)],
        out_specs=[pl.BlockSpec((tm, d), lambda i: (i, 0)), pl.BlockSpec((tm, d), lambda i: (i, 0))],
        out_shape=[jax.ShapeDtypeStruct((t, d), F32), jax.ShapeDtypeStruct((t, d), BF16)],
        scratch_shapes=[pltpu.VMEM((d, d), BF16)],
        compiler_params=_params("arbitrary"),
        name="out_proj",
    )(*ys, w, h, nw)


def _ssd_kernel(u_ref, w_ref, sh_ref, cw_ref, cb_ref, dtb_ref, alog_ref, d_ref, nw_ref, o_ref, p_ref, cbuf, st_ref):
    L = SSD_CHUNK

    @pl.when(pl.program_id(1) == 0)
    def _():
        cbuf[0:8, :] = jnp.zeros((8, SSD_XBC), F32)
        st_ref[...] = jnp.zeros_like(st_ref)

    p_ref[...] = _dot_nt(u_ref[...], w_ref[...])
    row = lax.broadcasted_iota(jnp.int32, (L, L), 0)
    col = lax.broadcasted_iota(jnp.int32, (L, L), 1)
    causal = (col <= row).astype(BF16)
    expand = (lax.broadcasted_iota(jnp.int32, (128, 512), 1) // SSD_HEAD_DIM + DT_LANE
              == lax.broadcasted_iota(jnp.int32, (128, 512), 0)).astype(BF16)
    lane = lax.broadcasted_iota(jnp.int32, (L, 128), 1)
    rowi = lax.broadcasted_iota(jnp.int32, (L, 128), 0)
    low = lane < 64
    low8 = lax.broadcasted_iota(jnp.int32, (8, 128), 1) < 64
    low64 = lax.broadcasted_iota(jnp.int32, (64, 128), 1) < 64
    keep = [rowi >= (lane % 64) + 64 * jh for jh in range(2)]
    blockdiag = (lax.broadcasted_iota(jnp.int32, (128, 128), 0) // 64
                 == lax.broadcasted_iota(jnp.int32, (128, 128), 1) // 64)

    def both_halves(x, g):
        r = pltpu.roll(x, 64, 1)
        return jnp.where(low, x, r) if g == 0 else jnp.where(low, r, x)

    def chunk(rows):
        z = p_ref[rows, 0:512]
        cbuf[8:8 + L, :] = p_ref[rows, 512:512 + SSD_XBC]
        ext = cbuf[...]
        acc = cb_ref[...] + ext[8:8 + L, :] * cw_ref[SSD_CONV - 1:SSD_CONV, :]
        for k in range(SSD_CONV - 1):
            acc = acc + pltpu.roll(ext, SSD_CONV - 1 - k, 0)[8:8 + L, :] * cw_ref[k:k + 1, :]
        cbuf[0:8, :] = ext[L:L + 8, :]
        xbc = _silu(acc)
        xs = xbc[:, 0:512]
        bm = xbc[:, 512:640]
        cm = xbc[:, 640:768]

        dt_c = _softplus(sh_ref[rows, :] + dtb_ref[...])
        cs_c = _dot_01(causal, dt_c * -jnp.exp(alog_ref[...]))
        dt = _dot_r01(dt_c, expand)
        cs = _dot_r01(cs_c, expand)
        cs_t = jnp.transpose(cs_c)
        cs_last = cs[L - 1:L, :]
        grow = jnp.exp(cs)
        tail = jnp.exp(cs_last - cs)
        total = jnp.exp(cs_last)
        xdt = (xs * dt).astype(BF16)
        bmb = bm.astype(BF16)

        groups, pairs, halves = range(2), range(4), range(2)
        cmask = [jnp.where(low if g == 0 else jnp.logical_not(low), cm, 0.0).astype(BF16) for g in groups]
        bhalf = [jnp.concatenate([bmb[64 * jh:64 * jh + 64, :]] * 2, axis=0) for jh in halves]
        gdup = [[_dot_nt(cmask[g], bhalf[jh]) for jh in halves] for g in groups]
        cdup = [both_halves(cm, g) for g in groups]
        bdup = [both_halves(bm, g) for g in groups]
        lanes = [slice(128 * pair, 128 * (pair + 1)) for pair in pairs]
        crow = []
        for pair in pairs:
            h0 = DT_LANE + 2 * pair
            r0 = jnp.broadcast_to(cs_t[h0:h0 + 1, :], (8, L))
            r1 = jnp.broadcast_to(cs_t[h0 + 1:h0 + 2, :], (8, L))
            crow.append([jnp.where(low8, r0, pltpu.roll(r1, 64, 1))[0:1, :],
                         jnp.where(low8, pltpu.roll(r0, 64, 1), r1)[0:1, :]])
        lmat = [[jnp.exp(jnp.where(keep[jh], cs[:, lanes[pair]] - crow[pair][jh], -jnp.inf)) for jh in halves]
                for pair in pairs]
        xbd = []
        for pair in pairs:
            xp = xdt[:, lanes[pair]]
            per_half = []
            for jh in halves:
                xj = xp[64 * jh:64 * jh + 64, :]
                zero = jnp.zeros_like(xj)
                per_half.append(jnp.concatenate([jnp.where(low64, xj, zero), jnp.where(low64, zero, xj)], axis=0))
            xbd.append(per_half)
        sts = [st_ref[pair] for pair in pairs]
        ys = []
        for pair in pairs:
            g = pair // 2
            y = _dot_nt((cdup[g] * grow[:, lanes[pair]]).astype(BF16), sts[pair].astype(BF16))
            for jh in halves:
                y = y + _dot((gdup[g][jh] * lmat[pair][jh]).astype(BF16), xbd[pair][jh])
            ys.append(y)
        for pair in pairs:
            upd = _dot_tn(xdt[:, lanes[pair]], (bdup[pair // 2] * tail[:, lanes[pair]]).astype(BF16))
            st_ref[pair] = total[:, lanes[pair]] * sts[pair] + jnp.where(blockdiag, upd, 0.0)
        y = jnp.concatenate(ys, axis=1) + d_ref[...] * xs
        y = y * _silu(z)
        ms = jnp.mean(y * y, axis=-1, keepdims=True)
        o_ref[rows, :] = (y * lax.rsqrt(ms + NORM_EPS) * nw_ref[...]).astype(BF16)

    for c in range(p_ref.shape[0] // L):
        chunk(slice(L * c, L * (c + 1)))


def _ssd_branch(l, u, wt, shared, cw, cb, dtb, alog, dsk, nw, bsz, s):
    rows = SSD_CHUNK * SSD_CHUNKS_PER_STEP
    nc = s // rows
    return pl.pallas_call(
        _ssd_kernel,
        grid=(bsz, nc),
        in_specs=[pl.BlockSpec((rows, D_MODEL), lambda b, c: (b * nc + c, 0)),
                  _layer((W_SSD, D_MODEL), l),
                  pl.BlockSpec((rows, 128), lambda b, c: (b * nc + c, SHARED_TILE)),
                  _layer((SSD_CONV, SSD_XBC), l), _layer((1, SSD_XBC), l),
                  _layer((1, 128), l), _layer((1, 128), l), _layer((1, 512), l), _layer((1, 512), l)],
        out_specs=pl.BlockSpec((rows, GROUP_W), lambda b, c: (b * nc + c, 0)),
        out_shape=jax.ShapeDtypeStruct((bsz * s, GROUP_W), BF16),
        scratch_shapes=[pltpu.VMEM((rows, W_SSD), F32),
                        pltpu.VMEM((SSD_CHUNK + 8, SSD_XBC), F32),
                        pltpu.VMEM((SSD_HEADS // 2, 2 * SSD_HEAD_DIM, 2 * SSD_STATE), F32)],
        compiler_params=_params("arbitrary", "arbitrary"),
        name="ssd_branch",
    )(u, wt, shared, cw, cb, dtb, alog, dsk, nw)


def _s5_prep_kernel(are_ref, aim_ref, ldt_ref, bre_ref, bim_ref, cre_ref, cim_ref, bb_ref, cc_ref, tab_ref):
    are = are_ref[...]
    aim = aim_ref[...]
    delta = jnp.exp(ldt_ref[...])
    mag = jnp.exp(are * delta)
    ar = mag * jnp.cos(aim * delta)
    ai = mag * jnp.sin(aim * delta)
    den = are * are + aim * aim
    coef_re = ((ar - 1.0) * are + ai * aim) / den
    coef_im = (ai * are - (ar - 1.0) * aim) / den

    bb_ref[...] = jnp.zeros_like(bb_ref)
    cc_ref[...] = jnp.zeros_like(cc_ref)
    for g in range(S5_GROUPS):
        kb, gl = divmod(g, 16)
        st = slice(S5_STATE * g, S5_STATE * (g + 1))
        rows = slice(S5_CH * gl, S5_CH * (gl + 1))
        re = slice(S5_STATE * gl, S5_STATE * (gl + 1))
        im = slice(S5_HALF + S5_STATE * gl, S5_HALF + S5_STATE * (gl + 1))
        bre = bre_ref[g]
        bim = bim_ref[g]
        bb_ref[kb, rows, re] = (coef_re[:, st] * bre - coef_im[:, st] * bim).astype(BF16)
        bb_ref[kb, rows, im] = (coef_re[:, st] * bim + coef_im[:, st] * bre).astype(BF16)
        cc_ref[kb, re, rows] = cre_ref[g].astype(BF16)
        cc_ref[kb, im, rows] = (-cim_ref[g]).astype(BF16)

    odd = lax.broadcasted_iota(jnp.int32, (8, S5_HALF), 0) % 2 == 1
    tab_ref[0] = jnp.where(odd, ar[:, S5_HALF:], ar[:, :S5_HALF])
    tab_ref[1] = jnp.where(odd, ai[:, S5_HALF:], ai[:, :S5_HALF])


def _s5_prep(l, are, aim, ldt, bre, bim, cre, cim):
    out3 = lambda shape: pl.BlockSpec(shape, lambda i: (0, 0, 0))
    return pl.pallas_call(
        _s5_prep_kernel,
        grid=(1,),
        in_specs=[_layer((1, S5_NS), l), _layer((1, S5_NS), l), _layer((1, S5_NS), l),
                  _layer((S5_GROUPS, S5_CH, S5_STATE), l), _layer((S5_GROUPS, S5_CH, S5_STATE), l),
                  _layer((S5_GROUPS, S5_STATE, S5_CH), l), _layer((S5_GROUPS, S5_STATE, S5_CH), l)],
        out_specs=[out3((2, 256, 2 * S5_HALF)), out3((2, 2 * S5_HALF, 256)), out3((2, 8, S5_HALF))],
        out_shape=[jax.ShapeDtypeStruct((2, 256, 2 * S5_HALF), BF16),
                   jax.ShapeDtypeStruct((2, 2 * S5_HALF, 256), BF16),
                   jax.ShapeDtypeStruct((2, 8, S5_HALF), F32)],
        compiler_params=_params("arbitrary"),
        name="s5_prep",
    )(are, aim, ldt, bre, bim, cre, cim)


def _s5_kernel(u_ref, w_ref, bb_ref, cc_ref, tab_ref, d_ref, gw_ref, gb_ref, o_ref, pbuf, hbuf, ybuf, carry):
    nb = u_ref.shape[0]
    rt = S5_TSTEP
    n = nb * rt

    @pl.when(pl.program_id(0) == 0)
    def _():
        carry[...] = jnp.zeros_like(carry)

    def cat(ref, blocks, rows):
        return jnp.concatenate([ref[c, rows, :] for c in blocks], axis=1)

    proj = _dot_nt(u_ref[...].reshape(n, u_ref.shape[2]), w_ref[...])
    for b in range(nb):
        for c in range(W_S5 // 128):
            pbuf[c, pl.ds(b, rt, stride=nb), :] = proj[rt * b:rt * (b + 1), 128 * c:128 * (c + 1)]
    u = cat(pbuf, range(0, 4), slice(None))
    ub = u.astype(BF16)
    nblk = 2 * S5_HALF // 128
    for j in range(2):
        hj = _dot(ub[:, 256 * j:256 * (j + 1)], bb_ref[j])
        for c in range(nblk):
            hbuf[c, pl.ds(j, n, stride=2), :] = hj[:, 128 * c:128 * (c + 1)]

    wb = S5_SCAN_LANES // 128
    for lc in range(S5_HALF // S5_SCAN_LANES):
        re_blocks = list(range(wb * lc, wb * (lc + 1)))
        im_blocks = [c + nblk // 2 for c in re_blocks]
        lanes = slice(S5_SCAN_LANES * lc, S5_SCAN_LANES * (lc + 1))
        ar = tab_ref[0, :, lanes]
        ai = tab_ref[1, :, lanes]

        def body(t, c, re_blocks=re_blocks, im_blocks=im_blocks, ar=ar, ai=ai):
            hr, hi = c
            rows = pl.ds(pl.multiple_of(t * 8, 8), 8)
            hr, hi = (ar * hr - ai * hi + cat(hbuf, re_blocks, rows),
                      ar * hi + ai * hr + cat(hbuf, im_blocks, rows))
            for k in range(wb):
                hbuf[re_blocks[k], rows, :] = hr[:, 128 * k:128 * (k + 1)]
                hbuf[im_blocks[k], rows, :] = hi[:, 128 * k:128 * (k + 1)]
            return hr, hi

        re = slice(S5_SCAN_LANES * lc, S5_SCAN_LANES * (lc + 1))
        im = slice(S5_HALF + re.start, S5_HALF + re.stop)
        hr, hi = lax.fori_loop(0, rt, body, (carry[:, re], carry[:, im]), unroll=4)
        carry[:, re] = hr
        carry[:, im] = hi

    ys = [_dot(cat(hbuf, range(nblk), pl.ds(j, n, stride=2)).astype(BF16), cc_ref[j]) for j in range(2)]
    y = jnp.concatenate(ys, axis=1) + d_ref[...] * u
    y = 0.5 * y * (1.0 + jnp.tanh(math.sqrt(2.0 / math.pi) * (y + 0.044715 * (y * y * y))))
    glu = _dot(y.astype(BF16), gw_ref[...]) + gb_ref[...]
    y = y / (1.0 + jnp.exp(-glu))
    y = y * _silu(cat(pbuf, range(4, 8), slice(None)))
    for c in range(GROUP_W // 128):
        ybuf[c] = y[:, 128 * c:128 * (c + 1)]
    for b in range(nb):
        o_ref[b] = cat(ybuf, range(GROUP_W // 128), pl.ds(b, rt, stride=nb)).astype(BF16)


def _s5_branch(l, u, wt, bb, cc, tab, dsk, gw, gb, bsz, s):
    assert 2 * bsz == 8, "scan rows (batch, state block) must fill the 8 sublanes"
    const = lambda shape: pl.BlockSpec(shape, lambda t: tuple(0 for _ in shape))
    n = bsz * S5_TSTEP
    out = pl.pallas_call(
        _s5_kernel,
        grid=(s // S5_TSTEP,),
        in_specs=[pl.BlockSpec((bsz, S5_TSTEP, D_MODEL), lambda t: (0, t, 0)),
                  _layer((W_S5, D_MODEL), l),
                  const((2, 256, 2 * S5_HALF)), const((2, 2 * S5_HALF, 256)), const((2, 8, S5_HALF)),
                  _layer((1, GROUP_W), l), _layer((GROUP_W, GROUP_W), l), _layer((1, GROUP_W), l)],
        out_specs=pl.BlockSpec((bsz, S5_TSTEP, GROUP_W), lambda t: (0, t, 0)),
        out_shape=jax.ShapeDtypeStruct((bsz, s, GROUP_W), BF16),
        scratch_shapes=[pltpu.VMEM((W_S5 // 128, n, 128), F32),
                        pltpu.VMEM((2 * S5_HALF // 128, 2 * n, 128), F32),
                        pltpu.VMEM((GROUP_W // 128, n, 128), F32),
                        pltpu.VMEM((8, 2 * S5_HALF), F32)],
        compiler_params=_params("arbitrary"),
        name="s5_branch",
    )(u.reshape(bsz, s, D_MODEL), wt, bb, cc, tab, dsk, gw, gb)
    return out.reshape(bsz * s, GROUP_W)


def _gla_kernel(u_ref, w_ref, sh_ref, w2_ref, gb_ref, nw_ref, o_ref, p_ref, st_ref):
    T = GLA_TILE
    NCH = T // GLA_CHUNK

    @pl.when(pl.program_id(1) == 0)
    def _():
        st_ref[...] = jnp.zeros_like(st_ref)

    p_ref[...] = _dot_nt(u_ref[...], w_ref[...])
    row = lax.broadcasted_iota(jnp.int32, (T, T), 0)
    col = lax.broadcasted_iota(jnp.int32, (T, T), 1)
    same = (row // GLA_CHUNK) == (col // GLA_CHUNK)
    intra = jnp.logical_and(same, col <= row)
    same_b, intra_b = same.astype(BF16), intra.astype(BF16)
    wide = (T, NCH * 128)
    blockmask = (lax.broadcasted_iota(jnp.int32, wide, 0) // GLA_CHUNK
                 == lax.broadcasted_iota(jnp.int32, wide, 1) // 128)
    wide_head = (lax.broadcasted_iota(jnp.int32, wide, 1) % 128) // GLA_DK
    lane_head = lax.broadcasted_iota(jnp.int32, (T, 128), 1) // GLA_DK
    zero = jnp.zeros((), BF16)
    scale = GLA_DK ** -0.5
    states = [st_ref[pair] for pair in range(GLA_HEADS // 2)]

    subs = range(p_ref.shape[0] // T)
    pairs = range(GLA_HEADS // 2)
    rows = [slice(T * sub, T * (sub + 1)) for sub in subs]
    xs = [_dot(sh_ref[rows[s], :].astype(BF16), w2_ref[...]) + gb_ref[...] for s in subs]
    gs = [-_softplus(-x) * (1.0 / GLA_GATE_NORM) for x in xs]
    bs = [_dot_01(intra_b, g) for g in gs]
    blasts = [_dot_01(same_b, g) for g in gs]
    q_mid, k_mid, q_dec, k_dec = [], [], [], []
    for s in subs:
        q = p_ref[rows[s], 0:256] * scale
        k = p_ref[rows[s], 256:512]
        half = 0.5 * blasts[s]
        q_mid.append((q * jnp.exp(bs[s] - half)).astype(BF16))
        k_mid.append((k * jnp.exp(half - bs[s])).astype(BF16))
        q_dec.append((q * jnp.exp(bs[s])).astype(BF16))
        k_dec.append((k * jnp.exp(blasts[s] - bs[s])).astype(BF16))

    lanes = [slice(128 * pair, 128 * (pair + 1)) for pair in pairs]
    vs = [[p_ref[rows[s], 512 + GLA_DV * h:512 + GLA_DV * (h + 1)].astype(BF16) for h in range(GLA_HEADS)]
          for s in subs]
    ds = {}
    for s in subs:
        for pair in pairs:
            kcat = jnp.where(blockmask, jnp.tile(k_dec[s][:, lanes[pair]], (1, NCH)), zero)
            ds[s, pair] = sum(_dot_tn(vs[s][2 * pair + hh], jnp.where(wide_head == hh, kcat, zero))
                              for hh in range(2))
    st_all = {}
    for s in subs:
        for pair in pairs:
            st = states[pair]
            sts = []
            for c in range(NCH):
                sts.append(st)
                dec = jnp.exp(blasts[s][GLA_CHUNK * c:GLA_CHUNK * c + 1, lanes[pair]])
                st = dec * st + ds[s, pair][:, 128 * c:128 * (c + 1)]
            states[pair] = st
            st_all[s, pair] = jnp.concatenate(sts, axis=1).astype(BF16)
    for pair in pairs:
        st_ref[pair] = states[pair]

    combos = [(s, pair, hh) for s in subs for pair in pairs for hh in range(2)]
    qcat = {(s, pair): jnp.where(blockmask, jnp.tile(q_dec[s][:, lanes[pair]], (1, NCH)), zero)
            for s in subs for pair in pairs}
    attn = {(s, pair, hh): _dot_nt(jnp.where(lane_head == hh, q_mid[s][:, lanes[pair]], zero),
                                   k_mid[s][:, lanes[pair]]) for s, pair, hh in combos}
    inter = {(s, pair, hh): _dot_nt(jnp.where(wide_head == hh, qcat[s, pair], zero), st_all[s, pair])
             for s, pair, hh in combos}
    outs = {(s, pair, hh): inter[s, pair, hh] + _dot(jnp.where(intra, attn[s, pair, hh], 0.0).astype(BF16),
                                                      vs[s][2 * pair + hh]) for s, pair, hh in combos}
    for s, pair, hh in combos:
        h = 2 * pair + hh
        o = outs[s, pair, hh]
        ms = jnp.mean(o * o, axis=-1, keepdims=True)
        o = o * lax.rsqrt(ms + NORM_EPS) * nw_ref[...]
        gate = p_ref[rows[s], 1024 + GLA_DV * h:1024 + GLA_DV * (h + 1)]
        o_ref[rows[s], GLA_DV * h:GLA_DV * (h + 1)] = (o * _silu(gate)).astype(BF16)


def _gla_branch(l, u, wt, shared, w2, gb, nw, bsz, s):
    rows = GLA_TILE * GLA_TILES_PER_STEP
    nt = s // rows
    return pl.pallas_call(
        _gla_kernel,
        grid=(bsz, nt),
        in_specs=[pl.BlockSpec((rows, D_MODEL), lambda b, t: (b * nt + t, 0)),
                  _layer((W_GLA, D_MODEL), l),
                  pl.BlockSpec((rows, 128), lambda b, t: (b * nt + t, SHARED_TILE)),
                  _layer((128, 256), l), _layer((1, 256), l), _layer((1, GLA_DV), l)],
        out_specs=pl.BlockSpec((rows, GROUP_W), lambda b, t: (b * nt + t, 0)),
        out_shape=jax.ShapeDtypeStruct((bsz * s, GROUP_W), BF16),
        scratch_shapes=[pltpu.VMEM((rows, W_GLA), F32), pltpu.VMEM((GLA_HEADS // 2, GLA_DV, 2 * GLA_DK), F32)],
        compiler_params=_params("arbitrary", "arbitrary"),
        name="gla_branch",
    )(u, wt, shared, w2, gb, nw)


def _rope_swap(t):
    return pltpu.roll(t, 64, 1)


def _spread_rope(w):
    half = MLA_ROPE // 2
    zeros = jnp.zeros(w.shape[:-1] + (64 - half,), w.dtype)
    return jnp.concatenate([w[..., :half], zeros, w[..., half:], zeros], axis=-1)


def _rope_kernel(pos_ref, invf_ref, sign_ref, cos_ref, sin_ref):
    ang = pos_ref[...].astype(F32) * invf_ref[...]
    cos_ref[...] = jnp.cos(ang)
    sin_ref[...] = jnp.sin(ang) * sign_ref[...]


def _rope_tables(pos, invf, sign, tm=512):
    t = pos.shape[0]
    const = pl.BlockSpec((1, 128), lambda i: (0, 0))
    tile = pl.BlockSpec((tm, 128), lambda i: (i, 0))
    return pl.pallas_call(
        _rope_kernel,
        grid=(t // tm,),
        in_specs=[pl.BlockSpec((tm, 1), lambda i: (i, 0)), const, const],
        out_specs=[tile, tile],
        out_shape=[jax.ShapeDtypeStruct((t, 128), F32)] * 2,
        compiler_params=_params("arbitrary"),
        name="rope_tables",
    )(pos, invf, sign)


def _mla_prep_kernel(p_ref, cos_ref, sin_ref, qnw_ref, wuq_ref, kvnw_ref, wukv_ref,
                     qhw_ref, khw_ref, q_ref, k_ref, vt_ref):
    inv_d = 1.0 / (MLA_NOPE + MLA_ROPE)
    scale = (MLA_NOPE + MLA_ROPE) ** -0.5 * math.log2(math.e)
    qhw = qhw_ref[...]
    khw = khw_ref[...]
    sub = 256
    lane = lax.broadcasted_iota(jnp.int32, (sub, 128), 1)
    rope_lanes = (lane % 64) < MLA_ROPE // 2
    for r0 in range(0, p_ref.shape[0], sub):
        rows = slice(r0, r0 + sub)
        cq = p_ref[rows, 512:896]
        ckv = p_ref[rows, 896:1024]
        kpe = jnp.where(rope_lanes, p_ref[rows, 1024:1152], 0.0)
        ms = jnp.mean(cq * cq, axis=-1, keepdims=True)
        qn = (cq * lax.rsqrt(ms + NORM_EPS) * qnw_ref[...]).astype(BF16)
        ms = jnp.mean(ckv * ckv, axis=-1, keepdims=True)
        kvn = (ckv * lax.rsqrt(ms + NORM_EPS) * kvnw_ref[...]).astype(BF16)
        cos_t = cos_ref[rows, :]
        sin_t = sin_ref[rows, :]
        kpe_ss = jnp.sum(kpe * kpe, axis=-1, keepdims=True)
        heads = range(MLA_HEADS)
        hp = MLA_HEAD_PAD
        qs = [_dot(qn, wuq_ref[:, hp * h:hp * (h + 1)]) for h in heads]
        kvs = [_dot(kvn, wukv_ref[:, hp * h:hp * (h + 1)]) for h in heads]
        q_ss = [jnp.sum(q * q, axis=-1, keepdims=True) for q in qs]
        k_ss = [jnp.sum(kv[:, 0:128] * kv[:, 0:128], axis=-1, keepdims=True) for kv in kvs]
        q_rs = [lax.rsqrt(ss * inv_d + NORM_EPS) * scale for ss in q_ss]
        k_rs = [lax.rsqrt((ss + kpe_ss) * inv_d + NORM_EPS) for ss in k_ss]
        q_rot = [qs[h][:, 128:256] * q_rs[h] * qhw[:, 128:256] for h in heads]
        k_rot = [kpe * k_rs[h] * khw[:, 128:256] for h in heads]
        q_swap = [_rope_swap(t) for t in q_rot]
        k_swap = [_rope_swap(t) for t in k_rot]
        for h in heads:
            lo = hp * h
            q_ref[rows, lo:lo + 128] = (qs[h][:, 0:128] * q_rs[h] * qhw[:, 0:128]).astype(BF16)
            q_ref[rows, lo + 128:lo + 256] = (q_rot[h] * cos_t + q_swap[h] * sin_t).astype(BF16)
            k_ref[rows, lo:lo + 128] = (kvs[h][:, 0:128] * k_rs[h] * khw[:, 0:128]).astype(BF16)
            k_ref[rows, lo + 128:lo + 256] = (k_rot[h] * cos_t + k_swap[h] * sin_t).astype(BF16)
            vt_ref[MLA_V * h:MLA_V * (h + 1), rows] = jnp.transpose(kvs[h][:, 128:256]).astype(BF16)


def _mla_prep(l, p, cos_t, sin_t, qnw, wuq, kvnw, wukv, qhw, khw, tm=512):
    t = p.shape[0]
    hp = MLA_HEADS * MLA_HEAD_PAD
    return pl.pallas_call(
        _mla_prep_kernel,
        grid=(t // tm,),
        in_specs=[pl.BlockSpec((tm, W_MLA), lambda i: (i, 0)),
                  pl.BlockSpec((tm, 128), lambda i: (i, 0)), pl.BlockSpec((tm, 128), lambda i: (i, 0)),
                  _layer((1, MLA_Q_RANK), l), _layer((MLA_Q_RANK, hp), l),
                  _layer((1, MLA_KV_RANK), l), _layer((MLA_KV_RANK, hp), l),
                  _layer((1, MLA_HEAD_PAD), l), _layer((1, MLA_HEAD_PAD), l)],
        out_specs=[pl.BlockSpec((tm, hp), lambda i: (i, 0)),
                   pl.BlockSpec((tm, hp), lambda i: (i, 0)),
                   pl.BlockSpec((MLA_HEADS * MLA_V, tm), lambda i: (0, i))],
        out_shape=[jax.ShapeDtypeStruct((t, hp), BF16),
                   jax.ShapeDtypeStruct((t, hp), BF16),
                   jax.ShapeDtypeStruct((MLA_HEADS * MLA_V, t), BF16)],
        compiler_params=_params("arbitrary"),
        name="mla_prep",
    )(p, cos_t, sin_t, qnw, wuq, kvnw, wukv, qhw, khw)


def _flash_kernel(q_ref, k_ref, vt_ref, g_ref, o_ref, s_ref):
    tq = FLASH_TQ
    hp = MLA_HEAD_PAD
    keep = (lax.broadcasted_iota(jnp.int32, (tq, tq), 0)
            <= lax.broadcasted_iota(jnp.int32, (tq, tq), 1))
    heads = range(MLA_HEADS)

    def attend(nb):
        n = nb * tq
        scores = [_dot_nt(k_ref[0:n, hp * h:hp * (h + 1)], q_ref[:, hp * h:hp * (h + 1)]) for h in heads]
        m = []
        for h in heads:
            blocks = [scores[h][tq * j:tq * (j + 1), :] for j in range(nb)]
            blocks[-1] = jnp.where(keep, blocks[-1], -jnp.inf)
            top = blocks[0]
            for j in range(nb):
                s_ref[h, tq * j:tq * (j + 1), :] = blocks[j]
                top = jnp.maximum(top, blocks[j])
            m.append(jnp.max(top, axis=0, keepdims=True))
        pt = [jnp.exp2(s_ref[h, 0:n, :] - m[h]) for h in heads]
        l = [jnp.sum(p, axis=0, keepdims=True) for p in pt]
        acc = [_dot(vt_ref[MLA_V * h:MLA_V * (h + 1), 0:n], pt[h].astype(BF16)) for h in heads]
        for h in heads:
            o = jnp.transpose(acc[h] / l[h])
            o_ref[:, MLA_V * h:MLA_V * (h + 1)] = (o * _silu(g_ref[:, MLA_V * h:MLA_V * (h + 1)])).astype(BF16)

    for nb in range(1, s_ref.shape[1] // tq + 1):
        pl.when(pl.program_id(1) == nb - 1)(functools.partial(attend, nb))


def _flash(q, k, vt, p, bsz, s):
    nq = s // FLASH_TQ
    hp = MLA_HEADS * MLA_HEAD_PAD
    hv = MLA_HEADS * MLA_V
    return pl.pallas_call(
        _flash_kernel,
        grid=(bsz, nq),
        in_specs=[pl.BlockSpec((FLASH_TQ, hp), lambda b, i: (b * nq + i, 0)),
                  pl.BlockSpec((s, hp), lambda b, i: (b, 0)),
                  pl.BlockSpec((hv, s), lambda b, i: (0, b)),
                  pl.BlockSpec((FLASH_TQ, GROUP_W), lambda b, i: (b * nq + i, 0))],
        out_specs=pl.BlockSpec((FLASH_TQ, hv), lambda b, i: (b * nq + i, 0)),
        out_shape=jax.ShapeDtypeStruct((bsz * s, hv), BF16),
        scratch_shapes=[pltpu.VMEM((MLA_HEADS, s, FLASH_TQ), F32)],
        compiler_params=_params("arbitrary", "arbitrary"),
        name="mla_flash",
    )(q, k, vt, p)


def _regroup_kernel(wt_ref, ssd_ref, s5_ref, gla_ref, mla_ref):
    offs = [0]
    for wd in IN_WIDTHS:
        offs.append(offs[-1] + wd)
    (z, xbc, dt, s5u, s5g, gq, gk, gv, gg, glr, cq, ckv, kpe, mg) = [
        (a, b) for a, b in zip(offs[:-1], offs[1:])]

    def put(dst, at, seg):
        dst[at:at + seg[1] - seg[0], :] = wt_ref[seg[0]:seg[1], :].astype(BF16)

    put(ssd_ref, 0, z)
    put(ssd_ref, 512, xbc)
    put(s5_ref, 0, s5u)
    put(s5_ref, 512, s5g)
    put(gla_ref, 0, gq)
    put(gla_ref, 256, gk)
    put(gla_ref, 512, gv)
    put(gla_ref, 1024, gg)
    put(mla_ref, 0, mg)
    put(mla_ref, 512, cq)
    put(mla_ref, 896, ckv)
    half = MLA_ROPE // 2
    zeros = lambda n: jnp.zeros((n, wt_ref.shape[1]), F32)
    shared = jnp.concatenate([wt_ref[kpe[0]:kpe[0] + half, :], wt_ref[dt[0]:dt[1], :], zeros(24),
                              wt_ref[kpe[0] + half:kpe[1], :], wt_ref[glr[0]:glr[1], :], zeros(16)], axis=0)
    mla_ref[1024:1152, :] = shared.astype(BF16)


def _pad_rows(x, n):
    return jnp.concatenate([x, jnp.zeros((n - x.shape[0], x.shape[1]), x.dtype)], axis=0)


def _regroup_w_in(w_in, tc=256):
    wt = jnp.swapaxes(w_in, 1, 2)
    depth, n, k = wt.shape
    widths = (W_SSD, W_S5, W_GLA, W_MLA)
    return pl.pallas_call(
        _regroup_kernel,
        grid=(depth, k // tc),
        in_specs=[pl.BlockSpec((None, n, tc), lambda l, i: (l, 0, i))],
        out_specs=[pl.BlockSpec((None, w, tc), lambda l, i: (l, 0, i)) for w in widths],
        out_shape=[jax.ShapeDtypeStruct((depth, w, k), BF16) for w in widths],
        compiler_params=_params("arbitrary", "arbitrary"),
        name="regroup_w_in",
    )(wt)


def _pad_last(w, n):
    return jnp.pad(w, [(0, 0)] * (w.ndim - 1) + [(0, n - w.shape[-1])])


def kernel(x, positions, norm_w, w_in, w_out, ssd_conv_w, ssd_conv_b, ssd_dt_bias, ssd_a_log, ssd_d, ssd_norm_w, s5_a_re, s5_a_im, s5_log_dt, s5_b_re, s5_b_im, s5_c_re, s5_c_im, s5_d, s5_glu_w, s5_glu_b, gla_gate_w2, gla_gate_b, gla_norm_w, mla_q_norm_w, mla_w_uq, mla_kv_norm_w, mla_w_ukv, mla_q_head_norm_w, mla_k_head_norm_w):
    bsz, s, d = x.shape
    depth = w_in.shape[0]
    t = bsz * s
    h = x.reshape(t, d)
    pos = positions.reshape(t, 1)
    row = lambda v: v[:, None, :]

    inv_freq = ROPE_THETA ** (-jnp.arange(0, MLA_ROPE, 2, dtype=F32) / MLA_ROPE)
    invf = _spread_rope(jnp.concatenate([inv_freq, inv_freq]))[None, :]
    sign = _spread_rope(jnp.concatenate([-jnp.ones((32,), F32), jnp.ones((32,), F32)]))[None, :]

    w_ssd, w_s5, w_gla, w_mla = _regroup_w_in(w_in)
    norm_w3 = row(norm_w)

    ssd_cb, ssd_nw = row(ssd_conv_b), row(ssd_norm_w)
    at_dt = lambda v: row(jnp.pad(v, ((0, 0), (DT_LANE, 128 - DT_LANE - SSD_HEADS))))
    ssd_dtb, ssd_alog = at_dt(ssd_dt_bias), at_dt(ssd_a_log)
    ssd_dsk = row(jnp.repeat(ssd_d, SSD_HEAD_DIM, axis=1))

    s5_are = s5_a_re.reshape(depth, 1, S5_NS)
    s5_aim = s5_a_im.reshape(depth, 1, S5_NS)
    s5_ldt = row(jnp.repeat(s5_log_dt, S5_STATE, axis=1))
    s5_bre, s5_bim = s5_b_re.transpose(0, 1, 3, 2), s5_b_im.transpose(0, 1, 3, 2)
    s5_cre, s5_cim = s5_c_re.transpose(0, 1, 3, 2), s5_c_im.transpose(0, 1, 3, 2)
    s5_dsk, s5_gw, s5_gb = row(s5_d), s5_glu_w.astype(BF16), row(s5_glu_b)

    gla_w2 = jnp.pad(gla_gate_w2, ((0, 0), (GLR_LANE, 128 - GLR_LANE - gla_gate_w2.shape[1]), (0, 0))).astype(BF16)
    gla_gb, gla_nw = row(gla_gate_b), row(gla_norm_w)

    head_pad = lambda w: jnp.concatenate([w[..., :MLA_NOPE], _spread_rope(w[..., MLA_NOPE:])], axis=-1)
    wuq = head_pad(mla_w_uq.reshape(depth, MLA_Q_RANK, MLA_HEADS, MLA_NOPE + MLA_ROPE))
    wuq = wuq.reshape(depth, MLA_Q_RANK, MLA_HEADS * MLA_HEAD_PAD).astype(BF16)
    wukv = mla_w_ukv.astype(BF16)
    qnw, kvnw = row(mla_q_norm_w), row(mla_kv_norm_w)
    qhw, khw = row(head_pad(mla_q_head_norm_w)), row(head_pad(mla_k_head_norm_w))

    cos_t, sin_t = _rope_tables(pos, invf, sign)
    u = _rmsnorm_bf16(h, norm_w3, 0)
    for l in range(depth):
        p_mla = _in_proj(u, w_mla, l, "in_proj_mla")

        y_a = _ssd_branch(l, u, w_ssd, p_mla, ssd_conv_w, ssd_cb, ssd_dtb, ssd_alog, ssd_dsk, ssd_nw, bsz, s)

        bb, cc, tab = _s5_prep(l, s5_are, s5_aim, s5_ldt, s5_bre, s5_bim, s5_cre, s5_cim)
        y_b = _s5_branch(l, u, w_s5, bb, cc, tab, s5_dsk, s5_gw, s5_gb, bsz, s)

        y_c = _gla_branch(l, u, w_gla, p_mla, gla_w2, gla_gb, gla_nw, bsz, s)

        q, k, vt = _mla_prep(l, p_mla, cos_t, sin_t, qnw, wuq, kvnw, wukv, qhw, khw)
        y_d = _flash(q, k, vt, p_mla, bsz, s)

        h, u = _out_proj((y_a, y_b, y_c, y_d), w_out, h, norm_w3, l, (l + 1) % depth)
    return h.reshape(bsz, s, d)
```

```python
import functools
import math

import jax
import jax.numpy as jnp
from jax import lax
from jax.experimental import pallas as pl
from jax.experimental.pallas import tpu as pltpu

F32 = jnp.float32
BF16 = jnp.bfloat16
NORM_EPS = 1e-6
HI = lax.Precision.HIGHEST

D_MODEL = 2048
GROUP_W = 512
SSD_HEADS = 8
SSD_HEAD_DIM = 64
SSD_STATE = 64
SSD_CHUNK = 128
SSD_CHUNKS_PER_STEP = 4
SSD_XBC = 768
SSD_CONV = 4
S5_GROUPS = 32
S5_CH = 16
S5_STATE = 64
S5_NS = S5_GROUPS * S5_STATE
S5_HALF = S5_NS // 2
S5_TSTEP = 128
S5_SCAN_LANES = 512
GLA_HEADS = 4
GLA_DK = 64
GLA_DV = 128
GLA_CHUNK = 16
GLA_TILE = 128
GLA_TILES_PER_STEP = 4
GLA_GATE_NORM = 16.0
MLA_HEADS = 4
MLA_NOPE = 128
MLA_ROPE = 64
MLA_V = 128
MLA_Q_RANK = 384
MLA_KV_RANK = 128
MLA_HEAD_PAD = 256
ROPE_THETA = 10000.0
FLASH_TQ = 256

IN_WIDTHS = (512, 768, 8, 512, 512, 256, 256, 512, 512, 16, 384, 128, 64, 512)
W_SSD = 512 + 768
W_S5 = 1024
W_GLA = 256 + 256 + 512 + 512
W_MLA = 512 + 384 + 128 + 128
DT_LANE = 32
GLR_LANE = 96

VMEM_LIMIT_BYTES = 56 * 1024 * 1024


def _params(*sem):
    return pltpu.CompilerParams(dimension_semantics=sem, vmem_limit_bytes=VMEM_LIMIT_BYTES)


def _layer(shape, l):
    return pl.BlockSpec((None,) + tuple(shape), lambda *_: (l,) + (0,) * len(shape))


def _silu(x):
    return x / (1.0 + jnp.exp(-x))


def _softplus(x):
    return jnp.maximum(x, 0.0) + jnp.log(1.0 + jnp.exp(-jnp.abs(x)))


def _dot(a, b):
    return jnp.dot(a, b, preferred_element_type=F32)


def _dot_nt(a, b):
    return lax.dot_general(a, b, (((1,), (1,)), ((), ())), preferred_element_type=F32)


def _dot_tn(a, b):
    return lax.dot_general(a, b, (((0,), (0,)), ((), ())), preferred_element_type=F32)


def _dot_01(m, x):
    hi = x.astype(BF16)
    rest = x - hi.astype(F32)
    mid = rest.astype(BF16)
    lo = (rest - mid.astype(F32)).astype(BF16)
    return _dot(m, hi) + _dot(m, mid) + _dot(m, lo)


def _dot_r01(x, m):
    hi = x.astype(BF16)
    rest = x - hi.astype(F32)
    mid = rest.astype(BF16)
    lo = (rest - mid.astype(F32)).astype(BF16)
    return _dot(hi, m) + _dot(mid, m) + _dot(lo, m)


def _rms_kernel(x_ref, w_ref, o_ref):
    x = x_ref[...]
    ms = jnp.mean(x * x, axis=-1, keepdims=True)
    o_ref[...] = (x * lax.rsqrt(ms + NORM_EPS) * w_ref[...]).astype(BF16)


def _rmsnorm_bf16(x, w, l, tm=512):
    t, d = x.shape
    return pl.pallas_call(
        _rms_kernel,
        grid=(t // tm,),
        in_specs=[pl.BlockSpec((tm, d), lambda i: (i, 0)), _layer((1, d), l)],
        out_specs=pl.BlockSpec((tm, d), lambda i: (i, 0)),
        out_shape=jax.ShapeDtypeStruct((t, d), BF16),
        compiler_params=_params("arbitrary"),
        name="rmsnorm_in",
    )(x, w)


def _out_proj_kernel(ya_ref, yb_ref, yc_ref, yd_ref, w_ref, h_ref, nw_ref, ho_ref, uo_ref, wb_ref):
    @pl.when(pl.program_id(0) == 0)
    def _():
        for i in range(w_ref.shape[0] // GROUP_W):
            wb_ref[GROUP_W * i:GROUP_W * (i + 1), :] = w_ref[GROUP_W * i:GROUP_W * (i + 1), :].astype(BF16)

    sub = 256
    for r in range(h_ref.shape[0] // sub):
        rows = slice(sub * r, sub * (r + 1))
        acc = h_ref[rows, :]
        for i, y_ref in enumerate((ya_ref, yb_ref, yc_ref, yd_ref)):
            acc = acc + _dot(y_ref[rows, :], wb_ref[GROUP_W * i:GROUP_W * (i + 1), :])
        ho_ref[rows, :] = acc
        ms = jnp.mean(acc * acc, axis=-1, keepdims=True)
        uo_ref[rows, :] = (acc * lax.rsqrt(ms + NORM_EPS) * nw_ref[...]).astype(BF16)


def _out_proj(ys, w, h, nw, l, l_next, tm=512):
    t, d = h.shape
    yspec = pl.BlockSpec((tm, GROUP_W), lambda i: (i, 0))
    return pl.pallas_call(
        _out_proj_kernel,
        grid=(t // tm,),
        in_specs=[yspec, yspec, yspec, yspec,
                  pl.BlockSpec((None, d, d), lambda i: (l, 0, 0), pipeline_mode=pl.Buffered(1)),
                  pl.BlockSpec((tm, d), lambda i: (i, 0)),
                  _layer((1, d), l_next)],
        out_specs=[pl.BlockSpec((tm, d), lambda i: (i, 0)), pl.BlockSpec((tm, d), lambda i: (i, 0))],
        out_shape=[jax.ShapeDtypeStruct((t, d), F32), jax.ShapeDtypeStruct((t, d), BF16)],
        scratch_shapes=[pltpu.VMEM((d, d), BF16)],
        compiler_params=_params("arbitrary"),
        name="out_proj",
    )(*ys, w, h, nw)


def _ssd_kernel(u_ref, w_ref, sh_ref, cw_ref, cb_ref, dtb_ref, alog_ref, d_ref, nw_ref, o_ref, p_ref, cbuf, st_ref):
    L = SSD_CHUNK

    @pl.when(pl.program_id(1) == 0)
    def _():
        cbuf[0:8, :] = jnp.zeros((8, SSD_XBC), F32)
        st_ref[...] = jnp.zeros_like(st_ref)

    p_ref[...] = _dot_nt(u_ref[...], w_ref[...])
    row = lax.broadcasted_iota(jnp.int32, (L, L), 0)
    col = lax.broadcasted_iota(jnp.int32, (L, L), 1)
    causal = (col <= row).astype(BF16)
    expand = (lax.broadcasted_iota(jnp.int32, (128, 512), 1) // SSD_HEAD_DIM + DT_LANE
              == lax.broadcasted_iota(jnp.int32, (128, 512), 0)).astype(BF16)
    lane = lax.broadcasted_iota(jnp.int32, (L, 128), 1)
    rowi = lax.broadcasted_iota(jnp.int32, (L, 128), 0)
    low = lane < 64
    low8 = lax.broadcasted_iota(jnp.int32, (8, 128), 1) < 64
    low64 = lax.broadcasted_iota(jnp.int32, (64, 128), 1) < 64
    keep = [rowi >= (lane % 64) + 64 * jh for jh in range(2)]
    blockdiag = (lax.broadcasted_iota(jnp.int32, (128, 128), 0) // 64
                 == lax.broadcasted_iota(jnp.int32, (128, 128), 1) // 64)

    def both_halves(x, g):
        r = pltpu.roll(x, 64, 1)
        return jnp.where(low, x, r) if g == 0 else jnp.where(low, r, x)

    def chunk(rows):
        z = p_ref[rows, 0:512]
        cbuf[8:8 + L, :] = p_ref[rows, 512:512 + SSD_XBC]
        ext = cbuf[...]
        acc = cb_ref[...] + ext[8:8 + L, :] * cw_ref[SSD_CONV - 1:SSD_CONV, :]
        for k in range(SSD_CONV - 1):
            acc = acc + pltpu.roll(ext, SSD_CONV - 1 - k, 0)[8:8 + L, :] * cw_ref[k:k + 1, :]
        cbuf[0:8, :] = ext[L:L + 8, :]
        xbc = _silu(acc)
        xs = xbc[:, 0:512]
        bm = xbc[:, 512:640]
        cm = xbc[:, 640:768]

        dt_c = _softplus(sh_ref[rows, :] + dtb_ref[...])
        cs_c = _dot_01(causal, dt_c * -jnp.exp(alog_ref[...]))
        dt = _dot_r01(dt_c, expand)
        cs = _dot_r01(cs_c, expand)
        cs_t = jnp.transpose(cs_c)
        cs_last = cs[L - 1:L, :]
        grow = jnp.exp(cs)
        tail = jnp.exp(cs_last - cs)
        total = jnp.exp(cs_last)
        xdt = (xs * dt).astype(BF16)
        bmb = bm.astype(BF16)

        groups, pairs, halves = range(2), range(4), range(2)
        cmask = [jnp.where(low if g == 0 else jnp.logical_not(low), cm, 0.0).astype(BF16) for g in groups]
        bhalf = [jnp.concatenate([bmb[64 * jh:64 * jh + 64, :]] * 2, axis=0) for jh in halves]
        gdup = [[_dot_nt(cmask[g], bhalf[jh]) for jh in halves] for g in groups]
        cdup = [both_halves(cm, g) for g in groups]
        bdup = [both_halves(bm, g) for g in groups]
        lanes = [slice(128 * pair, 128 * (pair + 1)) for pair in pairs]
        crow = []
        for pair in pairs:
            h0 = DT_LANE + 2 * pair
            r0 = jnp.broadcast_to(cs_t[h0:h0 + 1, :], (8, L))
            r1 = jnp.broadcast_to(cs_t[h0 + 1:h0 + 2, :], (8, L))
            crow.append([jnp.where(low8, r0, pltpu.roll(r1, 64, 1))[0:1, :],
                         jnp.where(low8, pltpu.roll(r0, 64, 1), r1)[0:1, :]])
        lmat = [[jnp.exp(jnp.where(keep[jh], cs[:, lanes[pair]] - crow[pair][jh], -jnp.inf)) for jh in halves]
                for pair in pairs]
        xbd = []
        for pair in pairs:
            xp = xdt[:, lanes[pair]]
            per_half = []
            for jh in halves:
                xj = xp[64 * jh:64 * jh + 64, :]
                zero = jnp.zeros_like(xj)
                per_half.append(jnp.concatenate([jnp.where(low64, xj, zero), jnp.where(low64, zero, xj)], axis=0))
            xbd.append(per_half)
        sts = [st_ref[pair] for pair in pairs]
        ys = []
        for pair in pairs:
            g = pair // 2
            y = _dot_nt((cdup[g] * grow[:, lanes[pair]]).astype(BF16), sts[pair].astype(BF16))
            for jh in halves:
                y = y + _dot((gdup[g][jh] * lmat[pair][jh]).astype(BF16), xbd[pair][jh])
            ys.append(y)
        for pair in pairs:
            upd = _dot_tn(xdt[:, lanes[pair]], (bdup[pair // 2] * tail[:, lanes[pair]]).astype(BF16))
            st_ref[pair] = total[:, lanes[pair]] * sts[pair] + jnp.where(blockdiag, upd, 0.0)
        y = jnp.concatenate(ys, axis=1) + d_ref[...] * xs
        y = y * _silu(z)
        ms = jnp.mean(y * y, axis=-1, keepdims=True)
        o_ref[rows, :] = (y * lax.rsqrt(ms + NORM_EPS) * nw_ref[...]).astype(BF16)

    for c in range(p_ref.shape[0] // L):
        chunk(slice(L * c, L * (c + 1)))


def _ssd_branch(l, u, wt, shared, cw, cb, dtb, alog, dsk, nw, bsz, s):
    rows = SSD_CHUNK * SSD_CHUNKS_PER_STEP
    nc = s // rows
    return pl.pallas_call(
        _ssd_kernel,
        grid=(bsz, nc),
        in_specs=[pl.BlockSpec((rows, D_MODEL), lambda b, c: (b * nc + c, 0)),
                  _layer((W_SSD, D_MODEL), l),
                  pl.BlockSpec((rows, 128), lambda b, c: (b * nc + c, 0)),
                  _layer((SSD_CONV, SSD_XBC), l), _layer((1, SSD_XBC), l),
                  _layer((1, 128), l), _layer((1, 128), l), _layer((1, 512), l), _layer((1, 512), l)],
        out_specs=pl.BlockSpec((rows, GROUP_W), lambda b, c: (b * nc + c, 0)),
        out_shape=jax.ShapeDtypeStruct((bsz * s, GROUP_W), BF16),
        scratch_shapes=[pltpu.VMEM((rows, W_SSD), F32),
                        pltpu.VMEM((SSD_CHUNK + 8, SSD_XBC), F32),
                        pltpu.VMEM((SSD_HEADS // 2, 2 * SSD_HEAD_DIM, 2 * SSD_STATE), F32)],
        compiler_params=_params("arbitrary", "arbitrary"),
        name="ssd_branch",
    )(u, wt, shared, cw, cb, dtb, alog, dsk, nw)


def _s5_prep_kernel(are_ref, aim_ref, ldt_ref, bre_ref, bim_ref, cre_ref, cim_ref, bb_ref, cc_ref, tab_ref):
    are = are_ref[...]
    aim = aim_ref[...]
    delta = jnp.exp(ldt_ref[...])
    mag = jnp.exp(are * delta)
    ar = mag * jnp.cos(aim * delta)
    ai = mag * jnp.sin(aim * delta)
    den = are * are + aim * aim
    coef_re = ((ar - 1.0) * are + ai * aim) / den
    coef_im = (ai * are - (ar - 1.0) * aim) / den

    bb_ref[...] = jnp.zeros_like(bb_ref)
    cc_ref[...] = jnp.zeros_like(cc_ref)
    for g in range(S5_GROUPS):
        kb, gl = divmod(g, 16)
        st = slice(S5_STATE * g, S5_STATE * (g + 1))
        rows = slice(S5_CH * gl, S5_CH * (gl + 1))
        re = slice(S5_STATE * gl, S5_STATE * (gl + 1))
        im = slice(S5_HALF + S5_STATE * gl, S5_HALF + S5_STATE * (gl + 1))
        bre = bre_ref[g]
        bim = bim_ref[g]
        bb_ref[kb, rows, re] = (coef_re[:, st] * bre - coef_im[:, st] * bim).astype(BF16)
        bb_ref[kb, rows, im] = (coef_re[:, st] * bim + coef_im[:, st] * bre).astype(BF16)
        cc_ref[kb, re, rows] = cre_ref[g].astype(BF16)
        cc_ref[kb, im, rows] = (-cim_ref[g]).astype(BF16)

    odd = lax.broadcasted_iota(jnp.int32, (8, S5_HALF), 0) % 2 == 1
    tab_ref[0] = jnp.where(odd, ar[:, S5_HALF:], ar[:, :S5_HALF])
    tab_ref[1] = jnp.where(odd, ai[:, S5_HALF:], ai[:, :S5_HALF])


def _s5_prep(l, are, aim, ldt, bre, bim, cre, cim):
    out3 = lambda shape: pl.BlockSpec(shape, lambda i: (0, 0, 0))
    return pl.pallas_call(
        _s5_prep_kernel,
        grid=(1,),
        in_specs=[_layer((1, S5_NS), l), _layer((1, S5_NS), l), _layer((1, S5_NS), l),
                  _layer((S5_GROUPS, S5_CH, S5_STATE), l), _layer((S5_GROUPS, S5_CH, S5_STATE), l),
                  _layer((S5_GROUPS, S5_STATE, S5_CH), l), _layer((S5_GROUPS, S5_STATE, S5_CH), l)],
        out_specs=[out3((2, 256, 2 * S5_HALF)), out3((2, 2 * S5_HALF, 256)), out3((2, 8, S5_HALF))],
        out_shape=[jax.ShapeDtypeStruct((2, 256, 2 * S5_HALF), BF16),
                   jax.ShapeDtypeStruct((2, 2 * S5_HALF, 256), BF16),
                   jax.ShapeDtypeStruct((2, 8, S5_HALF), F32)],
        compiler_params=_params("arbitrary"),
        name="s5_prep",
    )(are, aim, ldt, bre, bim, cre, cim)


def _s5_kernel(u_ref, w_ref, bb_ref, cc_ref, tab_ref, d_ref, gw_ref, gb_ref, o_ref, pbuf, hbuf, ybuf, carry):
    nb = u_ref.shape[0]
    rt = S5_TSTEP
    n = nb * rt

    @pl.when(pl.program_id(0) == 0)
    def _():
        carry[...] = jnp.zeros_like(carry)

    def cat(ref, blocks, rows):
        return jnp.concatenate([ref[c, rows, :] for c in blocks], axis=1)

    proj = _dot_nt(u_ref[...].reshape(n, u_ref.shape[2]), w_ref[...])
    for b in range(nb):
        for c in range(W_S5 // 128):
            pbuf[c, pl.ds(b, rt, stride=nb), :] = proj[rt * b:rt * (b + 1), 128 * c:128 * (c + 1)]
    u = cat(pbuf, range(0, 4), slice(None))
    ub = u.astype(BF16)
    nblk = 2 * S5_HALF // 128
    for j in range(2):
        hj = _dot(ub[:, 256 * j:256 * (j + 1)], bb_ref[j])
        for c in range(nblk):
            hbuf[c, pl.ds(j, n, stride=2), :] = hj[:, 128 * c:128 * (c + 1)]

    wb = S5_SCAN_LANES // 128
    for lc in range(S5_HALF // S5_SCAN_LANES):
        re_blocks = list(range(wb * lc, wb * (lc + 1)))
        im_blocks = [c + nblk // 2 for c in re_blocks]
        lanes = slice(S5_SCAN_LANES * lc, S5_SCAN_LANES * (lc + 1))
        ar = tab_ref[0, :, lanes]
        ai = tab_ref[1, :, lanes]

        def body(t, c, re_blocks=re_blocks, im_blocks=im_blocks, ar=ar, ai=ai):
            hr, hi = c
            rows = pl.ds(pl.multiple_of(t * 8, 8), 8)
            hr, hi = (ar * hr - ai * hi + cat(hbuf, re_blocks, rows),
                      ar * hi + ai * hr + cat(hbuf, im_blocks, rows))
            for k in range(wb):
                hbuf[re_blocks[k], rows, :] = hr[:, 128 * k:128 * (k + 1)]
                hbuf[im_blocks[k], rows, :] = hi[:, 128 * k:128 * (k + 1)]
            return hr, hi

        re = slice(S5_SCAN_LANES * lc, S5_SCAN_LANES * (lc + 1))
        im = slice(S5_HALF + re.start, S5_HALF + re.stop)
        hr, hi = lax.fori_loop(0, rt, body, (carry[:, re], carry[:, im]), unroll=4)
        carry[:, re] = hr
        carry[:, im] = hi

    ys = [_dot(cat(hbuf, range(nblk), pl.ds(j, n, stride=2)).astype(BF16), cc_ref[j]) for j in range(2)]
    y = jnp.concatenate(ys, axis=1) + d_ref[...] * u
    y = 0.5 * y * (1.0 + jnp.tanh(math.sqrt(2.0 / math.pi) * (y + 0.044715 * (y * y * y))))
    glu = _dot(y.astype(BF16), gw_ref[...]) + gb_ref[...]
    y = y / (1.0 + jnp.exp(-glu))
    y = y * _silu(cat(pbuf, range(4, 8), slice(None)))
    for c in range(GROUP_W // 128):
        ybuf[c] = y[:, 128 * c:128 * (c + 1)]
    for b in range(nb):
        o_ref[b] = cat(ybuf, range(GROUP_W // 128), pl.ds(b, rt, stride=nb)).astype(BF16)


def _s5_branch(l, u, wt, bb, cc, tab, dsk, gw, gb, bsz, s):
    assert 2 * bsz == 8, "scan rows (batch, state block) must fill the 8 sublanes"
    const = lambda shape: pl.BlockSpec(shape, lambda t: tuple(0 for _ in shape))
    n = bsz * S5_TSTEP
    out = pl.pallas_call(
        _s5_kernel,
        grid=(s // S5_TSTEP,),
        in_specs=[pl.BlockSpec((bsz, S5_TSTEP, D_MODEL), lambda t: (0, t, 0)),
                  _layer((W_S5, D_MODEL), l),
                  const((2, 256, 2 * S5_HALF)), const((2, 2 * S5_HALF, 256)), const((2, 8, S5_HALF)),
                  _layer((1, GROUP_W), l), _layer((GROUP_W, GROUP_W), l), _layer((1, GROUP_W), l)],
        out_specs=pl.BlockSpec((bsz, S5_TSTEP, GROUP_W), lambda t: (0, t, 0)),
        out_shape=jax.ShapeDtypeStruct((bsz, s, GROUP_W), BF16),
        scratch_shapes=[pltpu.VMEM((W_S5 // 128, n, 128), F32),
                        pltpu.VMEM((2 * S5_HALF // 128, 2 * n, 128), F32),
                        pltpu.VMEM((GROUP_W // 128, n, 128), F32),
                        pltpu.VMEM((8, 2 * S5_HALF), F32)],
        compiler_params=_params("arbitrary"),
        name="s5_branch",
    )(u.reshape(bsz, s, D_MODEL), wt, bb, cc, tab, dsk, gw, gb)
    return out.reshape(bsz * s, GROUP_W)


def _gla_kernel(u_ref, w_ref, sh_ref, w2_ref, gb_ref, nw_ref, o_ref, p_ref, st_ref):
    T = GLA_TILE
    NCH = T // GLA_CHUNK

    @pl.when(pl.program_id(1) == 0)
    def _():
        st_ref[...] = jnp.zeros_like(st_ref)

    p_ref[...] = _dot_nt(u_ref[...], w_ref[...])
    row = lax.broadcasted_iota(jnp.int32, (T, T), 0)
    col = lax.broadcasted_iota(jnp.int32, (T, T), 1)
    same = (row // GLA_CHUNK) == (col // GLA_CHUNK)
    intra = jnp.logical_and(same, col <= row)
    same_b, intra_b = same.astype(BF16), intra.astype(BF16)
    wide = (T, NCH * 128)
    blockmask = (lax.broadcasted_iota(jnp.int32, wide, 0) // GLA_CHUNK
                 == lax.broadcasted_iota(jnp.int32, wide, 1) // 128)
    wide_head = (lax.broadcasted_iota(jnp.int32, wide, 1) % 128) // GLA_DK
    lane_head = lax.broadcasted_iota(jnp.int32, (T, 128), 1) // GLA_DK
    zero = jnp.zeros((), BF16)
    scale = GLA_DK ** -0.5
    states = [st_ref[pair] for pair in range(GLA_HEADS // 2)]

    subs = range(p_ref.shape[0] // T)
    pairs = range(GLA_HEADS // 2)
    rows = [slice(T * sub, T * (sub + 1)) for sub in subs]
    xs = [_dot(sh_ref[rows[s], :].astype(BF16), w2_ref[...]) + gb_ref[...] for s in subs]
    gs = [-_softplus(-x) * (1.0 / GLA_GATE_NORM) for x in xs]
    bs = [_dot_01(intra_b, g) for g in gs]
    blasts = [_dot_01(same_b, g) for g in gs]
    q_mid, k_mid, q_dec, k_dec = [], [], [], []
    for s in subs:
        q = p_ref[rows[s], 0:256] * scale
        k = p_ref[rows[s], 256:512]
        half = 0.5 * blasts[s]
        q_mid.append((q * jnp.exp(bs[s] - half)).astype(BF16))
        k_mid.append((k * jnp.exp(half - bs[s])).astype(BF16))
        q_dec.append((q * jnp.exp(bs[s])).astype(BF16))
        k_dec.append((k * jnp.exp(blasts[s] - bs[s])).astype(BF16))

    lanes = [slice(128 * pair, 128 * (pair + 1)) for pair in pairs]
    vs = [[p_ref[rows[s], 512 + GLA_DV * h:512 + GLA_DV * (h + 1)].astype(BF16) for h in range(GLA_HEADS)]
          for s in subs]
    ds = {}
    for s in subs:
        for pair in pairs:
            kcat = jnp.where(blockmask, jnp.tile(k_dec[s][:, lanes[pair]], (1, NCH)), zero)
            ds[s, pair] = sum(_dot_tn(vs[s][2 * pair + hh], jnp.where(wide_head == hh, kcat, zero))
                              for hh in range(2))
    st_all = {}
    for s in subs:
        for pair in pairs:
            st = states[pair]
            sts = []
            for c in range(NCH):
                sts.append(st)
                dec = jnp.exp(blasts[s][GLA_CHUNK * c:GLA_CHUNK * c + 1, lanes[pair]])
                st = dec * st + ds[s, pair][:, 128 * c:128 * (c + 1)]
            states[pair] = st
            st_all[s, pair] = jnp.concatenate(sts, axis=1).astype(BF16)
    for pair in pairs:
        st_ref[pair] = states[pair]

    combos = [(s, pair, hh) for s in subs for pair in pairs for hh in range(2)]
    qcat = {(s, pair): jnp.where(blockmask, jnp.tile(q_dec[s][:, lanes[pair]], (1, NCH)), zero)
            for s in subs for pair in pairs}
    attn = {(s, pair, hh): _dot_nt(jnp.where(lane_head == hh, q_mid[s][:, lanes[pair]], zero),
                                   k_mid[s][:, lanes[pair]]) for s, pair, hh in combos}
    inter = {(s, pair, hh): _dot_nt(jnp.where(wide_head == hh, qcat[s, pair], zero), st_all[s, pair])
             for s, pair, hh in combos}
    outs = {(s, pair, hh): inter[s, pair, hh] + _dot(jnp.where(intra, attn[s, pair, hh], 0.0).astype(BF16),
                                                      vs[s][2 * pair + hh]) for s, pair, hh in combos}
    for s, pair, hh in combos:
        h = 2 * pair + hh
        o = outs[s, pair, hh]
        ms = jnp.mean(o * o, axis=-1, keepdims=True)
        o = o * lax.rsqrt(ms + NORM_EPS) * nw_ref[...]
        gate = p_ref[rows[s], 1024 + GLA_DV * h:1024 + GLA_DV * (h + 1)]
        o_ref[rows[s], GLA_DV * h:GLA_DV * (h + 1)] = (o * _silu(gate)).astype(BF16)


def _gla_branch(l, u, wt, shared, w2, gb, nw, bsz, s):
    rows = GLA_TILE * GLA_TILES_PER_STEP
    nt = s // rows
    return pl.pallas_call(
        _gla_kernel,
        grid=(bsz, nt),
        in_specs=[pl.BlockSpec((rows, D_MODEL), lambda b, t: (b * nt + t, 0)),
                  _layer((W_GLA, D_MODEL), l),
                  pl.BlockSpec((rows, 128), lambda b, t: (b * nt + t, 0)),
                  _layer((128, 256), l), _layer((1, 256), l), _layer((1, GLA_DV), l)],
        out_specs=pl.BlockSpec((rows, GROUP_W), lambda b, t: (b * nt + t, 0)),
        out_shape=jax.ShapeDtypeStruct((bsz * s, GROUP_W), BF16),
        scratch_shapes=[pltpu.VMEM((rows, W_GLA), F32), pltpu.VMEM((GLA_HEADS // 2, GLA_DV, 2 * GLA_DK), F32)],
        compiler_params=_params("arbitrary", "arbitrary"),
        name="gla_branch",
    )(u, wt, shared, w2, gb, nw)


def _rope_swap(t):
    return pltpu.roll(t, 64, 1)


def _spread_rope(w):
    half = MLA_ROPE // 2
    zeros = jnp.zeros(w.shape[:-1] + (64 - half,), w.dtype)
    return jnp.concatenate([w[..., :half], zeros, w[..., half:], zeros], axis=-1)


def _rope_kernel(pos_ref, invf_ref, sign_ref, cos_ref, sin_ref):
    ang = pos_ref[...].astype(F32) * invf_ref[...]
    cos_ref[...] = jnp.cos(ang)
    sin_ref[...] = jnp.sin(ang) * sign_ref[...]


def _rope_tables(pos, invf, sign, tm=512):
    t = pos.shape[0]
    const = pl.BlockSpec((1, 128), lambda i: (0, 0))
    tile = pl.BlockSpec((tm, 128), lambda i: (i, 0))
    return pl.pallas_call(
        _rope_kernel,
        grid=(t // tm,),
        in_specs=[pl.BlockSpec((tm, 1), lambda i: (i, 0)), const, const],
        out_specs=[tile, tile],
        out_shape=[jax.ShapeDtypeStruct((t, 128), F32)] * 2,
        compiler_params=_params("arbitrary"),
        name="rope_tables",
    )(pos, invf, sign)


def _mla_prep_kernel(u_ref, w_ref, cos_ref, sin_ref, qnw_ref, wuq_ref, kvnw_ref, wukv_ref,
                     qhw_ref, khw_ref, q_ref, k_ref, vt_ref, gate_ref, narrow_ref):
    inv_d = 1.0 / (MLA_NOPE + MLA_ROPE)
    scale = (MLA_NOPE + MLA_ROPE) ** -0.5 * math.log2(math.e)
    qhw = qhw_ref[...]
    khw = khw_ref[...]
    sub = 256
    lane = lax.broadcasted_iota(jnp.int32, (sub, 128), 1)
    rope_lanes = (lane % 64) < MLA_ROPE // 2
    projs = [_dot_nt(u_ref[r0:r0 + sub, :], w_ref[...]) for r0 in range(0, u_ref.shape[0], sub)]
    for r0, proj in zip(range(0, u_ref.shape[0], sub), projs):
        rows = slice(r0, r0 + sub)
        gate_ref[rows, :] = proj[:, 0:GROUP_W]
        narrow_ref[rows, :] = proj[:, W_MLA - 128:W_MLA]
        cq = proj[:, 512:896]
        ckv = proj[:, 896:1024]
        kpe = jnp.where(rope_lanes, proj[:, 1024:1152], 0.0)
        ms = jnp.mean(cq * cq, axis=-1, keepdims=True)
        qn = (cq * lax.rsqrt(ms + NORM_EPS) * qnw_ref[...]).astype(BF16)
        ms = jnp.mean(ckv * ckv, axis=-1, keepdims=True)
        kvn = (ckv * lax.rsqrt(ms + NORM_EPS) * kvnw_ref[...]).astype(BF16)
        cos_t = cos_ref[rows, :]
        sin_t = sin_ref[rows, :]
        kpe_ss = jnp.sum(kpe * kpe, axis=-1, keepdims=True)
        heads = range(MLA_HEADS)
        hp = MLA_HEAD_PAD
        qs = [_dot(qn, wuq_ref[:, hp * h:hp * (h + 1)]) for h in heads]
        kvs = [_dot(kvn, wukv_ref[:, hp * h:hp * (h + 1)]) for h in heads]
        q_ss = [jnp.sum(q * q, axis=-1, keepdims=True) for q in qs]
        k_ss = [jnp.sum(kv[:, 0:128] * kv[:, 0:128], axis=-1, keepdims=True) for kv in kvs]
        q_rs = [lax.rsqrt(ss * inv_d + NORM_EPS) * scale for ss in q_ss]
        k_rs = [lax.rsqrt((ss + kpe_ss) * inv_d + NORM_EPS) for ss in k_ss]
        q_rot = [qs[h][:, 128:256] * q_rs[h] * qhw[:, 128:256] for h in heads]
        k_rot = [kpe * k_rs[h] * khw[:, 128:256] for h in heads]
        q_swap = [_rope_swap(t) for t in q_rot]
        k_swap = [_rope_swap(t) for t in k_rot]
        for h in heads:
            lo = hp * h
            q_ref[rows, lo:lo + 128] = (qs[h][:, 0:128] * q_rs[h] * qhw[:, 0:128]).astype(BF16)
            q_ref[rows, lo + 128:lo + 256] = (q_rot[h] * cos_t + q_swap[h] * sin_t).astype(BF16)
            k_ref[rows, lo:lo + 128] = (kvs[h][:, 0:128] * k_rs[h] * khw[:, 0:128]).astype(BF16)
            k_ref[rows, lo + 128:lo + 256] = (k_rot[h] * cos_t + k_swap[h] * sin_t).astype(BF16)
            vt_ref[MLA_V * h:MLA_V * (h + 1), rows] = jnp.transpose(kvs[h][:, 128:256]).astype(BF16)


def _mla_prep(l, u, wt, cos_t, sin_t, qnw, wuq, kvnw, wukv, qhw, khw, tm=512):
    t = u.shape[0]
    hp = MLA_HEADS * MLA_HEAD_PAD
    return pl.pallas_call(
        _mla_prep_kernel,
        grid=(t // tm,),
        in_specs=[pl.BlockSpec((tm, D_MODEL), lambda i: (i, 0)), _layer((W_MLA, D_MODEL), l),
                  pl.BlockSpec((tm, 128), lambda i: (i, 0)), pl.BlockSpec((tm, 128), lambda i: (i, 0)),
                  _layer((1, MLA_Q_RANK), l), _layer((MLA_Q_RANK, hp), l),
                  _layer((1, MLA_KV_RANK), l), _layer((MLA_KV_RANK, hp), l),
                  _layer((1, MLA_HEAD_PAD), l), _layer((1, MLA_HEAD_PAD), l)],
        out_specs=[pl.BlockSpec((tm, hp), lambda i: (i, 0)),
                   pl.BlockSpec((tm, hp), lambda i: (i, 0)),
                   pl.BlockSpec((MLA_HEADS * MLA_V, tm), lambda i: (0, i)),
                   pl.BlockSpec((tm, GROUP_W), lambda i: (i, 0)),
                   pl.BlockSpec((tm, 128), lambda i: (i, 0))],
        out_shape=[jax.ShapeDtypeStruct((t, hp), BF16),
                   jax.ShapeDtypeStruct((t, hp), BF16),
                   jax.ShapeDtypeStruct((MLA_HEADS * MLA_V, t), BF16),
                   jax.ShapeDtypeStruct((t, GROUP_W), F32),
                   jax.ShapeDtypeStruct((t, 128), F32)],
        compiler_params=_params("arbitrary"),
        name="mla_prep",
    )(u, wt, cos_t, sin_t, qnw, wuq, kvnw, wukv, qhw, khw)


def _flash_kernel(q_ref, k_ref, vt_ref, g_ref, o_ref, s_ref):
    tq = FLASH_TQ
    hp = MLA_HEAD_PAD
    keep = (lax.broadcasted_iota(jnp.int32, (tq, tq), 0)
            <= lax.broadcasted_iota(jnp.int32, (tq, tq), 1))
    heads = range(MLA_HEADS)

    def attend(nb):
        n = nb * tq
        scores = [_dot_nt(k_ref[0:n, hp * h:hp * (h + 1)], q_ref[:, hp * h:hp * (h + 1)]) for h in heads]
        m = []
        for h in heads:
            blocks = [scores[h][tq * j:tq * (j + 1), :] for j in range(nb)]
            blocks[-1] = jnp.where(keep, blocks[-1], -jnp.inf)
            top = blocks[0]
            for j in range(nb):
                s_ref[h, tq * j:tq * (j + 1), :] = blocks[j]
                top = jnp.maximum(top, blocks[j])
            m.append(jnp.max(top, axis=0, keepdims=True))
        pt = [jnp.exp2(s_ref[h, 0:n, :] - m[h]) for h in heads]
        l = [jnp.sum(p, axis=0, keepdims=True) for p in pt]
        acc = [_dot(vt_ref[MLA_V * h:MLA_V * (h + 1), 0:n], pt[h].astype(BF16)) for h in heads]
        for h in heads:
            o = jnp.transpose(acc[h] / l[h])
            o_ref[:, MLA_V * h:MLA_V * (h + 1)] = (o * _silu(g_ref[:, MLA_V * h:MLA_V * (h + 1)])).astype(BF16)

    for nb in range(1, s_ref.shape[1] // tq + 1):
        pl.when(pl.program_id(1) == nb - 1)(functools.partial(attend, nb))


def _flash(q, k, vt, gate, bsz, s):
    nq = s // FLASH_TQ
    hp = MLA_HEADS * MLA_HEAD_PAD
    hv = MLA_HEADS * MLA_V
    return pl.pallas_call(
        _flash_kernel,
        grid=(bsz, nq),
        in_specs=[pl.BlockSpec((FLASH_TQ, hp), lambda b, i: (b * nq + i, 0)),
                  pl.BlockSpec((s, hp), lambda b, i: (b, 0)),
                  pl.BlockSpec((hv, s), lambda b, i: (0, b)),
                  pl.BlockSpec((FLASH_TQ, GROUP_W), lambda b, i: (b * nq + i, 0))],
        out_specs=pl.BlockSpec((FLASH_TQ, hv), lambda b, i: (b * nq + i, 0)),
        out_shape=jax.ShapeDtypeStruct((bsz * s, hv), BF16),
        scratch_shapes=[pltpu.VMEM((MLA_HEADS, s, FLASH_TQ), F32)],
        compiler_params=_params("arbitrary", "arbitrary"),
        name="mla_flash",
    )(q, k, vt, gate)


def _regroup_kernel(wt_ref, ssd_ref, s5_ref, gla_ref, mla_ref):
    offs = [0]
    for wd in IN_WIDTHS:
        offs.append(offs[-1] + wd)
    (z, xbc, dt, s5u, s5g, gq, gk, gv, gg, glr, cq, ckv, kpe, mg) = [
        (a, b) for a, b in zip(offs[:-1], offs[1:])]

    def put(dst, at, seg):
        dst[at:at + seg[1] - seg[0], :] = wt_ref[seg[0]:seg[1], :].astype(BF16)

    put(ssd_ref, 0, z)
    put(ssd_ref, 512, xbc)
    put(s5_ref, 0, s5u)
    put(s5_ref, 512, s5g)
    put(gla_ref, 0, gq)
    put(gla_ref, 256, gk)
    put(gla_ref, 512, gv)
    put(gla_ref, 1024, gg)
    put(mla_ref, 0, mg)
    put(mla_ref, 512, cq)
    put(mla_ref, 896, ckv)
    half = MLA_ROPE // 2
    zeros = lambda n: jnp.zeros((n, wt_ref.shape[1]), F32)
    shared = jnp.concatenate([wt_ref[kpe[0]:kpe[0] + half, :], wt_ref[dt[0]:dt[1], :], zeros(24),
                              wt_ref[kpe[0] + half:kpe[1], :], wt_ref[glr[0]:glr[1], :], zeros(16)], axis=0)
    mla_ref[1024:1152, :] = shared.astype(BF16)


def _pad_rows(x, n):
    return jnp.concatenate([x, jnp.zeros((n - x.shape[0], x.shape[1]), x.dtype)], axis=0)


def _regroup_w_in(w_in, tc=256):
    wt = jnp.swapaxes(w_in, 1, 2)
    depth, n, k = wt.shape
    widths = (W_SSD, W_S5, W_GLA, W_MLA)
    return pl.pallas_call(
        _regroup_kernel,
        grid=(depth, k // tc),
        in_specs=[pl.BlockSpec((None, n, tc), lambda l, i: (l, 0, i))],
        out_specs=[pl.BlockSpec((None, w, tc), lambda l, i: (l, 0, i)) for w in widths],
        out_shape=[jax.ShapeDtypeStruct((depth, w, k), BF16) for w in widths],
        compiler_params=_params("arbitrary", "arbitrary"),
        name="regroup_w_in",
    )(wt)


def _pad_last(w, n):
    return jnp.pad(w, [(0, 0)] * (w.ndim - 1) + [(0, n - w.shape[-1])])


def kernel(x, positions, norm_w, w_in, w_out, ssd_conv_w, ssd_conv_b, ssd_dt_bias, ssd_a_log, ssd_d, ssd_norm_w, s5_a_re, s5_a_im, s5_log_dt, s5_b_re, s5_b_im, s5_c_re, s5_c_im, s5_d, s5_glu_w, s5_glu_b, gla_gate_w2, gla_gate_b, gla_norm_w, mla_q_norm_w, mla_w_uq, mla_kv_norm_w, mla_w_ukv, mla_q_head_norm_w, mla_k_head_norm_w):
    bsz, s, d = x.shape
    depth = w_in.shape[0]
    t = bsz * s
    h = x.reshape(t, d)
    pos = positions.reshape(t, 1)
    row = lambda v: v[:, None, :]

    inv_freq = ROPE_THETA ** (-jnp.arange(0, MLA_ROPE, 2, dtype=F32) / MLA_ROPE)
    invf = _spread_rope(jnp.concatenate([inv_freq, inv_freq]))[None, :]
    sign = _spread_rope(jnp.concatenate([-jnp.ones((32,), F32), jnp.ones((32,), F32)]))[None, :]

    w_ssd, w_s5, w_gla, w_mla = _regroup_w_in(w_in)
    norm_w3 = row(norm_w)

    ssd_cb, ssd_nw = row(ssd_conv_b), row(ssd_norm_w)
    at_dt = lambda v: row(jnp.pad(v, ((0, 0), (DT_LANE, 128 - DT_LANE - SSD_HEADS))))
    ssd_dtb, ssd_alog = at_dt(ssd_dt_bias), at_dt(ssd_a_log)
    ssd_dsk = row(jnp.repeat(ssd_d, SSD_HEAD_DIM, axis=1))

    s5_are = s5_a_re.reshape(depth, 1, S5_NS)
    s5_aim = s5_a_im.reshape(depth, 1, S5_NS)
    s5_ldt = row(jnp.repeat(s5_log_dt, S5_STATE, axis=1))
    s5_bre, s5_bim = s5_b_re.transpose(0, 1, 3, 2), s5_b_im.transpose(0, 1, 3, 2)
    s5_cre, s5_cim = s5_c_re.transpose(0, 1, 3, 2), s5_c_im.transpose(0, 1, 3, 2)
    s5_dsk, s5_gw, s5_gb = row(s5_d), s5_glu_w.astype(BF16), row(s5_glu_b)

    gla_w2 = jnp.pad(gla_gate_w2, ((0, 0), (GLR_LANE, 128 - GLR_LANE - gla_gate_w2.shape[1]), (0, 0))).astype(BF16)
    gla_gb, gla_nw = row(gla_gate_b), row(gla_norm_w)

    head_pad = lambda w: jnp.concatenate([w[..., :MLA_NOPE], _spread_rope(w[..., MLA_NOPE:])], axis=-1)
    wuq = head_pad(mla_w_uq.reshape(depth, MLA_Q_RANK, MLA_HEADS, MLA_NOPE + MLA_ROPE))
    wuq = wuq.reshape(depth, MLA_Q_RANK, MLA_HEADS * MLA_HEAD_PAD).astype(BF16)
    wukv = mla_w_ukv.astype(BF16)
    qnw, kvnw = row(mla_q_norm_w), row(mla_kv_norm_w)
    qhw, khw = row(head_pad(mla_q_head_norm_w)), row(head_pad(mla_k_head_norm_w))

    cos_t, sin_t = _rope_tables(pos, invf, sign)
    u = _rmsnorm_bf16(h, norm_w3, 0)
    for l in range(depth):
        q, k, vt, mla_gate, narrow = _mla_prep(l, u, w_mla, cos_t, sin_t, qnw, wuq, kvnw, wukv, qhw, khw)

        y_a = _ssd_branch(l, u, w_ssd, narrow, ssd_conv_w, ssd_cb, ssd_dtb, ssd_alog, ssd_dsk, ssd_nw, bsz, s)

        bb, cc, tab = _s5_prep(l, s5_are, s5_aim, s5_ldt, s5_bre, s5_bim, s5_cre, s5_cim)
        y_b = _s5_branch(l, u, w_s5, bb, cc, tab, s5_dsk, s5_gw, s5_gb, bsz, s)

        y_c = _gla_branch(l, u, w_gla, narrow, gla_w2, gla_gb, gla_nw, bsz, s)

        y_d = _flash(q, k, vt, mla_gate, bsz, s)

        h, u = _out_proj((y_a, y_b, y_c, y_d), w_out, h, norm_w3, l, (l + 1) % depth)
    return h.reshape(bsz, s, d)
```
